```python
import math
import jax, jax.numpy as jnp
from jax import lax
import numpy as np

D_MODEL = 2048
BATCH = 4
SEQ = 2048
DEPTH = 2
DEC_BATCH = 8
DEC_SEQ = 64
PAST_LEN = 1024

CHUNK = 64
BLOCK = 16
N_EVEN = (DEPTH + 1) // 2
N_ODD = DEPTH // 2

H_A = 4
DK_A = 128
DV_A = 256
ROPE_BASE = 10000.0
H_B = 8
DK_B = 128
DV_B = 128
H_C = 4
DK_C = 256
DV_C = 512
GK_RANK = 16
GATE_NORMALIZER = 16.0
D_FF = 5504
CONV_W = 3
EPS = 1e-6

QA = H_A * DK_A
VA = H_A * DV_A
QB = H_B * DK_B
VB = H_B * DV_B
D_IN_EVEN = 2 * QA + 2 * VA + 2 * QB + 2 * VB
D_MIX_EVEN = VA + VB
SPLITS_EVEN = (QA, 2 * QA, 2 * QA + VA, 2 * QA + 2 * VA,
               2 * QA + 2 * VA + QB, 2 * QA + 2 * VA + 2 * QB, 2 * QA + 2 * VA + 2 * QB + VB)
QC = H_C * DK_C
VC = H_C * DV_C
D_IN_ODD = 2 * QC + 2 * VC + GK_RANK
SPLITS_ODD = (QC, 2 * QC, 2 * QC + VC, 2 * QC + 2 * VC)

kernel_name = "hybrid_retention_hgrn2_gla_convglu_stream_step"


def rmsnorm(x, w):
    xf = x.astype(jnp.float32)
    y = xf * lax.rsqrt(jnp.mean(xf * xf, axis=-1, keepdims=True) + EPS)
    return (y * w.astype(jnp.float32)).astype(x.dtype)


def head_groupnorm(x):
    xf = x.astype(jnp.float32)
    mu = jnp.mean(xf, axis=-1, keepdims=True)
    var = jnp.mean(jnp.square(xf - mu), axis=-1, keepdims=True)
    return ((xf - mu) * lax.rsqrt(var + EPS)).astype(x.dtype)


def rotary(x, pos):
    half = x.shape[-1] // 2
    inv = ROPE_BASE ** (-jnp.arange(half, dtype=jnp.float32) / half)
    ang = pos.astype(jnp.float32)[:, None] * inv[None, :]
    cos = jnp.cos(ang)[None, :, None, :]
    sin = jnp.sin(ang)[None, :, None, :]
    xf = x.astype(jnp.float32)
    x1, x2 = xf[..., :half], xf[..., half:]
    return jnp.concatenate([x1 * cos - x2 * sin, x1 * sin + x2 * cos], axis=-1).astype(x.dtype)


def gated_linear_recurrence(q, k, v, log_g, s0):
    B, T, H, K = q.shape
    V = v.shape[-1]
    n = -(-T // BLOCK)
    pad = n * BLOCK - T

    def prep(a):
        a = jnp.pad(a.astype(jnp.float32), ((0, 0), (0, pad), (0, 0), (0, 0)))
        return a.reshape(B, n, BLOCK, H, a.shape[-1]).transpose(1, 0, 3, 2, 4)

    qb, kb, vb, gb = prep(q), prep(k), prep(v), prep(log_g)
    b = jnp.cumsum(gb, axis=3)
    b_last = b[:, :, :, -1:, :]
    q_in = qb * jnp.exp(b)
    k_in = kb * jnp.exp(-b)
    k_out = kb * jnp.exp(b_last - b)
    causal = jnp.tril(jnp.ones((BLOCK, BLOCK), dtype=bool))
    scores = jnp.where(causal, jnp.einsum('nbhlk,nbhsk->nbhls', q_in, k_in), 0.0)
    o_intra = jnp.einsum('nbhls,nbhsv->nbhlv', scores, vb)

    def step(S, xs):
        q_i, k_o, v_i, dec = xs
        o_inter = jnp.einsum('bhlk,bhkv->bhlv', q_i, S)
        S = jnp.exp(dec)[..., 0, :, None] * S + jnp.einsum('bhlk,bhlv->bhkv', k_o, v_i)
        return S, o_inter

    s_final, o_inter = lax.scan(step, s0.astype(jnp.float32), (q_in, k_out, vb, b_last))
    o = (o_intra + o_inter).transpose(1, 0, 3, 2, 4).reshape(B, n * BLOCK, H, V)[:, :T]
    return o.astype(v.dtype), s_final.astype(s0.dtype)


def even_mixer(h, pos, s_ret, s_hgrn, w_in, w_out, hgrn_gnorm, lb):
    B, T, _ = h.shape
    q_a, k_a, v_a, g_a, q_b, f_b, i_b, g_b = jnp.split(h @ w_in, SPLITS_EVEN, axis=-1)
    q_a = rotary(q_a.reshape(B, T, H_A, DK_A), pos)
    k_a = rotary(k_a.reshape(B, T, H_A, DK_A), pos) * (DK_A ** -0.5)
    log_gamma = jnp.log1p(-jnp.exp2(-5.0 - jnp.arange(H_A, dtype=jnp.float32)))
    log_g_a = jnp.broadcast_to(log_gamma[None, None, :, None], (B, T, H_A, DK_A))
    o_a, s_ret_new = gated_linear_recurrence(q_a, k_a, v_a.reshape(B, T, H_A, DV_A), log_g_a, s_ret)
    o_a = head_groupnorm(o_a).reshape(B, T, VA) * jax.nn.silu(g_a)
    q_b = jax.nn.silu(q_b).reshape(B, T, H_B, DK_B) * (DK_B ** -0.5)
    f = lb + (1.0 - lb) * jax.nn.sigmoid(f_b.astype(jnp.float32))
    k_b = (1.0 - f).reshape(B, T, H_B, DK_B)
    log_f = jnp.log(f).reshape(B, T, H_B, DK_B)
    o_b, s_hgrn_new = gated_linear_recurrence(q_b, k_b, i_b.reshape(B, T, H_B, DV_B), log_f, s_hgrn)
    o_b = rmsnorm(o_b, hgrn_gnorm).reshape(B, T, VB) * jax.nn.silu(g_b)
    y = jnp.concatenate([o_a, o_b], axis=-1) @ w_out
    return y, s_ret_new, s_hgrn_new


def odd_mixer(h, s_gla, w_in, w_gk2, b_gk2, gla_gnorm, w_out):
    B, T, _ = h.shape
    q, k, v, g, gk_low = jnp.split(h @ w_in, SPLITS_ODD, axis=-1)
    q = q.reshape(B, T, H_C, DK_C) * (DK_C ** -0.5)
    k = k.reshape(B, T, H_C, DK_C)
    log_g = jax.nn.log_sigmoid((gk_low @ w_gk2 + b_gk2).astype(jnp.float32)) / GATE_NORMALIZER
    o, s_new = gated_linear_recurrence(q, k, v.reshape(B, T, H_C, DV_C), log_g.reshape(B, T, H_C, DK_C), s_gla)
    o = rmsnorm(o, gla_gnorm).reshape(B, T, VC) * jax.nn.silu(g)
    return o @ w_out, s_new


def conv_ffn(h, conv_buf, w_up, conv_w, conv_b, w_down):
    a, u = jnp.split(h @ w_up, 2, axis=-1)
    T = a.shape[1]
    a_ext = jnp.concatenate([conv_buf.astype(a.dtype), a], axis=1)
    a_conv = conv_b + sum(a_ext[:, j:j + T] * conv_w[j] for j in range(CONV_W))
    y = (jax.nn.silu(a_conv) * u) @ w_down
    return y, a_ext[:, -(CONV_W - 1):]


def run_group(x, pos, s_ret, s_hgrn, s_gla, s_conv, norm_mix, norm_ffn, norm_final,
              w_in_even, w_out_even, hgrn_lb, hgrn_gnorm, w_in_odd, w_gk2, b_gk2, gla_gnorm,
              w_out_odd, ffn_w_up, ffn_conv_w, ffn_conv_b, ffn_w_down):
    lb_all = jnp.cumsum(jax.nn.softmax(hgrn_lb.astype(jnp.float32), axis=0), axis=0)
    new_ret, new_hgrn, new_gla, new_conv = [], [], [], []
    h = x
    for l in range(DEPTH):
        i = l // 2
        hn = rmsnorm(h, norm_mix[l])
        if l % 2 == 0:
            y, sr, sh = even_mixer(hn, pos, s_ret[i], s_hgrn[i], w_in_even[i], w_out_even[i],
                                   hgrn_gnorm[i], lb_all[l])
            new_ret.append(sr)
            new_hgrn.append(sh)
        else:
            y, sg = odd_mixer(hn, s_gla[i], w_in_odd[i], w_gk2[i], b_gk2[i], gla_gnorm[i], w_out_odd[i])
            new_gla.append(sg)
        h = h + y
        y, sc = conv_ffn(rmsnorm(h, norm_ffn[l]), s_conv[l], ffn_w_up[l], ffn_conv_w[l],
                         ffn_conv_b[l], ffn_w_down[l])
        new_conv.append(sc)
        h = h + y
    return (rmsnorm(h, norm_final), jnp.stack(new_ret), jnp.stack(new_hgrn),
            jnp.stack(new_gla), jnp.stack(new_conv))


def setup_inputs(seed: int = 0) -> dict:
    key = jax.random.key(seed)
    ks = jax.random.split(key, 24)

    def nrm(k, shape, scale):
        return jax.random.normal(k, shape, dtype=jnp.float32) * scale

    return {
        "x_prompt": nrm(ks[0], (BATCH, SEQ, D_MODEL), 1.0),
        "x_sample": nrm(ks[1], (DEC_BATCH, DEC_SEQ, D_MODEL), 1.0),
        "state_ret": nrm(ks[2], (N_EVEN, DEC_BATCH, H_A, DK_A, DV_A), 0.5),
        "state_hgrn": nrm(ks[3], (N_EVEN, DEC_BATCH, H_B, DK_B, DV_B), 0.5),
        "state_gla": nrm(ks[4], (N_ODD, DEC_BATCH, H_C, DK_C, DV_C), 0.5),
        "cache_ffn_conv": nrm(ks[5], (DEPTH, DEC_BATCH, CONV_W - 1, D_FF), 1.0),
        "norm_mix": 1.0 + nrm(ks[6], (DEPTH, D_MODEL), 0.02),
        "norm_ffn": 1.0 + nrm(ks[7], (DEPTH, D_MODEL), 0.02),
        "norm_final": 1.0 + nrm(ks[8], (D_MODEL,), 0.02),
        "w_in_even": nrm(ks[9], (N_EVEN, D_MODEL, D_IN_EVEN), D_MODEL ** -0.5),
        "w_out_even": nrm(ks[10], (N_EVEN, D_MIX_EVEN, D_MODEL), D_MIX_EVEN ** -0.5),
        "hgrn_lb": nrm(ks[11], (DEPTH + 1, QB), 0.1),
        "hgrn_gnorm": 1.0 + nrm(ks[12], (N_EVEN, DV_B), 0.02),
        "w_in_odd": nrm(ks[13], (N_ODD, D_MODEL, D_IN_ODD), D_MODEL ** -0.5),
        "w_gk2": nrm(ks[14], (N_ODD, GK_RANK, QC), GK_RANK ** -0.5),
        "b_gk2": nrm(ks[15], (N_ODD, QC), 0.02),
        "gla_gnorm": 1.0 + nrm(ks[16], (N_ODD, DV_C), 0.02),
        "w_out_odd": nrm(ks[17], (N_ODD, VC, D_MODEL), VC ** -0.5),
        "ffn_w_up": nrm(ks[18], (DEPTH, D_MODEL, 2 * D_FF), D_MODEL ** -0.5),
        "ffn_conv_w": nrm(ks[19], (DEPTH, CONV_W, D_FF), CONV_W ** -0.5),
        "ffn_conv_b": nrm(ks[20], (DEPTH, D_FF), 0.02),
        "ffn_w_down": nrm(ks[21], (DEPTH, D_FF, D_MODEL), D_FF ** -0.5),
    }


def reference(x_prompt, x_sample, state_ret, state_hgrn, state_gla, cache_ffn_conv,
              norm_mix, norm_ffn, norm_final, w_in_even, w_out_even, hgrn_lb, hgrn_gnorm,
              w_in_odd, w_gk2, b_gk2, gla_gnorm, w_out_odd, ffn_w_up, ffn_conv_w, ffn_conv_b,
              ffn_w_down):
    dt = x_prompt.dtype
    Bp = x_prompt.shape[0]
    y_prompt, ret_p, hgrn_p, gla_p, conv_p = run_group(
        x_prompt, jnp.arange(SEQ, dtype=jnp.int32),
        jnp.zeros((N_EVEN, Bp, H_A, DK_A, DV_A), dt),
        jnp.zeros((N_EVEN, Bp, H_B, DK_B, DV_B), dt),
        jnp.zeros((N_ODD, Bp, H_C, DK_C, DV_C), dt),
        jnp.zeros((DEPTH, Bp, CONV_W - 1, D_FF), dt),
        norm_mix, norm_ffn, norm_final, w_in_even, w_out_even, hgrn_lb, hgrn_gnorm,
        w_in_odd, w_gk2, b_gk2, gla_gnorm, w_out_odd, ffn_w_up, ffn_conv_w, ffn_conv_b, ffn_w_down)
    y_sample, ret_s, hgrn_s, gla_s, conv_s = run_group(
        x_sample, PAST_LEN + jnp.arange(x_sample.shape[1], dtype=jnp.int32),
        state_ret, state_hgrn, state_gla, cache_ffn_conv,
        norm_mix, norm_ffn, norm_final, w_in_even, w_out_even, hgrn_lb, hgrn_gnorm,
        w_in_odd, w_gk2, b_gk2, gla_gnorm, w_out_odd, ffn_w_up, ffn_conv_w, ffn_conv_b, ffn_w_down)
    return (y_prompt, y_sample, ret_p, ret_s, hgrn_p, hgrn_s, gla_p, gla_s, conv_p, conv_s)
```

```python
import functools

import numpy as np
import jax
import jax.numpy as jnp
from jax import lax
from jax.experimental import pallas as pl
from jax.experimental.pallas import tpu as pltpu

F32 = jnp.float32
BF16 = jnp.bfloat16

EPS = 1e-6
ROPE_BASE = 10000.0
GATE_NORMALIZER = 16.0
PAST_LEN = 1024

H_A, DK_A, DV_A = 4, 128, 256
H_B, DK_B, DV_B = 8, 128, 128
H_C, DK_C, DV_C = 4, 256, 512
GK_RANK = 16
CONV_W = 3

LANES = 128
SUBLANES = 8
VMEM_LIMIT_BYTES = 52 * 1024 * 1024

CHUNK = 64
SUB = 16
SUB_SHIFT = 4
assert 1 << SUB_SHIFT == SUB and CHUNK % SUB == 0

NT_DIMS = (((1,), (1,)), ((), ()))
TN_DIMS = (((0,), (0,)), ((), ()))


def _cparams(n_axes):
    return pltpu.CompilerParams(
        dimension_semantics=("arbitrary",) * n_axes,
        vmem_limit_bytes=VMEM_LIMIT_BYTES,
    )


def _sigmoid(x):
    return 1.0 / (1.0 + jnp.exp(-x))


def _silu(x):
    return x * _sigmoid(x)


def _rmsnorm_rows_to(x_ref, nw_ref, dst_ref, rows):
    rows = min(rows, x_ref.shape[0])
    assert x_ref.shape[0] % rows == 0
    n = x_ref.shape[0] // rows

    def body(c, carry):
        r = pl.multiple_of(c * rows, rows)
        x = x_ref[pl.ds(r, rows), :]
        ms = jnp.mean(x * x, axis=-1, keepdims=True)
        dst_ref[pl.ds(r, rows), :] = ((x * lax.rsqrt(ms + EPS)) * nw_ref[...]).astype(BF16)
        return carry

    lax.fori_loop(0, n, body, 0)


def _norm_matmul_kernel(x_ref, nw_ref, w_ref, o_ref, hn_ref):
    @pl.when(pl.program_id(1) == 0)
    def _():
        _rmsnorm_rows_to(x_ref, nw_ref, hn_ref, 128)

    o_ref[...] = jnp.dot(hn_ref[...], w_ref[...], preferred_element_type=F32)


def _norm_matmul_low_kernel(x_ref, nw_ref, w_ref, wl_ref, o_ref, low_ref, hn_ref):
    @pl.when(pl.program_id(1) == 0)
    def _():
        _rmsnorm_rows_to(x_ref, nw_ref, hn_ref, 128)
        low_ref[...] = jnp.dot(hn_ref[...], wl_ref[...], preferred_element_type=F32)

    o_ref[...] = jnp.dot(hn_ref[...], w_ref[...], preferred_element_type=F32)


def _norm_matmul(x, nw, w, w_low=None, *, tm, tn):
    m, d = x.shape
    n = w.shape[1]
    grid = (m // tm, n // tn)
    in_specs = [
        pl.BlockSpec((tm, d), lambda i, j: (i, 0)),
        pl.BlockSpec((1, d), lambda i, j: (0, 0)),
        pl.BlockSpec((d, tn), lambda i, j: (0, j)),
    ]
    out_shape = [jax.ShapeDtypeStruct((m, n), F32)]
    out_specs = [pl.BlockSpec((tm, tn), lambda i, j: (i, j))]
    args = [x, nw.reshape(1, d), w]
    if w_low is None:
        body = _norm_matmul_kernel
    else:
        body = _norm_matmul_low_kernel
        nl = w_low.shape[1]
        in_specs.append(pl.BlockSpec((d, nl), lambda i, j: (0, 0)))
        out_shape.append(jax.ShapeDtypeStruct((m, nl), F32))
        out_specs.append(pl.BlockSpec((tm, nl), lambda i, j: (i, 0)))
        args.append(w_low)
    outs = pl.pallas_call(
        body,
        grid=grid,
        in_specs=in_specs,
        out_specs=out_specs,
        out_shape=out_shape,
        scratch_shapes=[pltpu.VMEM((tm, d), BF16)],
        compiler_params=_cparams(2),
        name="norm_matmul",
    )(*args)
    return outs if w_low is not None else outs[0]


def _matmul_res_kernel(*refs, n_in):
    a_refs = refs[:n_in]
    w_refs = refs[n_in:2 * n_in]
    res_ref = refs[2 * n_in]
    o_ref = refs[2 * n_in + 1]
    acc = res_ref[...]
    for a_ref, w_ref in zip(a_refs, w_refs):
        acc = acc + jnp.dot(a_ref[...], w_ref[...], preferred_element_type=F32)
    o_ref[...] = acc


def _matmul_res(a_list, w_list, res, *, tm, tn):
    m, n = res.shape
    grid = (m // tm, n // tn)
    in_specs = []
    for a in a_list:
        in_specs.append(pl.BlockSpec((tm, a.shape[1]), lambda i, j: (i, 0)))
    for w in w_list:
        in_specs.append(pl.BlockSpec((w.shape[0], tn), lambda i, j: (0, j)))
    in_specs.append(pl.BlockSpec((tm, tn), lambda i, j: (i, j)))
    return pl.pallas_call(
        functools.partial(_matmul_res_kernel, n_in=len(a_list)),
        grid=grid,
        in_specs=in_specs,
        out_specs=pl.BlockSpec((tm, tn), lambda i, j: (i, j)),
        out_shape=jax.ShapeDtypeStruct((m, n), F32),
        compiler_params=_cparams(2),
        name="matmul_res",
    )(*a_list, *w_list, res)


def _local_cumsum(lg):
    pos = lax.broadcasted_iota(jnp.int32, lg.shape, 0) & (SUB - 1)
    x = lg
    s = 1
    while s < SUB:
        x = x + jnp.where(pos >= s, pltpu.roll(x, s, 0), 0.0)
        s *= 2
    return x


def _chunk_step(q, k, vb, bl, tot, st_ref):
    c = q.shape[0]
    ns = c // SUB
    a = [jnp.zeros_like(tot[0])]
    for m in range(ns):
        a.append(a[-1] + tot[m])
    b_last = a[ns]

    qt, kbar, khat, qin, kout = [], [], [], [], []
    qd = {d: [] for d in range(2, ns)}
    for m in range(ns):
        sl = slice(m * SUB, (m + 1) * SUB)
        blm = bl[sl]
        qtm = q[sl] * jnp.exp(blm)
        km = k[sl]
        khm = km * jnp.exp(tot[m] - blm)
        qt.append(qtm)
        kbar.append(km * jnp.exp(-blm))
        khat.append(khm)
        qin.append(qtm * jnp.exp(a[m]))
        kout.append(khm * jnp.exp(b_last - a[m + 1]))
        for d in range(2, ns):
            if m >= d:
                qd[d].append(qtm * jnp.exp(a[m] - a[m - d + 1]))
            else:
                qd[d].append(jnp.zeros_like(qtm))

    def cat(xs):
        return jnp.concatenate(xs, axis=0).astype(BF16)

    qt_b, kbar_b, khat_b = cat(qt), cat(kbar), cat(khat)
    ri = lax.broadcasted_iota(jnp.int32, (c, c), 0)
    ci = lax.broadcasted_iota(jnp.int32, (c, c), 1)
    dist = (ri >> SUB_SHIFT) - (ci >> SUB_SHIFT)
    s0 = lax.dot_general(qt_b, kbar_b, NT_DIMS, preferred_element_type=F32)
    s1 = lax.dot_general(qt_b, khat_b, NT_DIMS, preferred_element_type=F32)
    scores = jnp.where((dist == 0) & (ri >= ci), s0, jnp.where(dist == 1, s1, 0.0))
    for d in range(2, ns):
        sd = lax.dot_general(cat(qd[d]), khat_b, NT_DIMS, preferred_element_type=F32)
        scores = jnp.where(dist == d, sd, scores)

    st = st_ref[...]
    o = jnp.dot(scores.astype(BF16), vb, preferred_element_type=F32)
    o = o + lax.dot_general(cat(qin), st.astype(BF16), NT_DIMS, preferred_element_type=F32)
    st_ref[...] = st * jnp.exp(b_last) + lax.dot_general(
        vb, cat(kout), TN_DIMS, preferred_element_type=F32)
    return o


def _sub_totals(bl):
    return [bl[(m + 1) * SUB - 1:(m + 1) * SUB, :] for m in range(bl.shape[0] // SUB)]


def _state_prologue(s0_ref, st_ref):
    @pl.when(pl.program_id(2) == 0)
    def _():
        if s0_ref is None:
            st_ref[...] = jnp.zeros_like(st_ref)
        else:
            st_ref[...] = jnp.transpose(s0_ref[0, 0])


def _state_epilogue(sn_ref, st_ref):
    @pl.when(pl.program_id(2) == pl.num_programs(2) - 1)
    def _():
        sn_ref[0, 0] = jnp.transpose(st_ref[...])


def _ret_kernel(*refs, has_s0):
    if has_s0:
        q_ref, k_ref, v_ref, g_ref, cos_ref, sin_ref, lgam_ref, s0_ref, o_ref, sn_ref, st_ref = refs
    else:
        q_ref, k_ref, v_ref, g_ref, cos_ref, sin_ref, lgam_ref, o_ref, sn_ref, st_ref = refs
        s0_ref = None
    _state_prologue(s0_ref, st_ref)
    tt = q_ref.shape[0]
    lgam = lgam_ref[0]
    pos1 = (lax.broadcasted_iota(jnp.int32, (CHUNK, 1), 0) & (SUB - 1)) + 1
    bl = pos1.astype(F32) * lgam
    tot = [lgam * float(SUB)] * (CHUNK // SUB)
    half = DK_A // 2

    def body(c, carry):
        r = pl.multiple_of(c * CHUNK, CHUNK)
        rows = pl.ds(r, CHUNK)
        cos, sin = cos_ref[rows, :], sin_ref[rows, :]
        qr, kr = q_ref[rows, :], k_ref[rows, :]
        q = qr * cos + pltpu.roll(qr, half, 1) * sin
        k = (kr * cos + pltpu.roll(kr, half, 1) * sin) * (DK_A ** -0.5)
        o = _chunk_step(q, k, v_ref[rows, :].astype(BF16), bl, tot, st_ref)
        mu = jnp.mean(o, axis=-1, keepdims=True)
        oc = o - mu
        var = jnp.mean(oc * oc, axis=-1, keepdims=True)
        o_ref[rows, :] = ((oc * lax.rsqrt(var + EPS)) * _silu(g_ref[rows, :])).astype(BF16)
        return carry

    lax.fori_loop(0, tt // CHUNK, body, 0)
    _state_epilogue(sn_ref, st_ref)


def _hgrn_kernel(*refs, has_s0, layer):
    if has_s0:
        q_ref, f_ref, i_ref, g_ref, lbp_ref, gn_ref, s0_ref, o_ref, sn_ref, st_ref = refs
    else:
        q_ref, f_ref, i_ref, g_ref, lbp_ref, gn_ref, o_ref, sn_ref, st_ref = refs
        s0_ref = None
    _state_prologue(s0_ref, st_ref)
    tt = q_ref.shape[0]
    lbp = lbp_ref[0]
    e = jnp.exp(lbp - jnp.max(lbp, axis=0, keepdims=True))
    sm = e / jnp.sum(e, axis=0, keepdims=True)
    lb = jnp.sum(sm[:layer + 1], axis=0, keepdims=True)
    gn = gn_ref[...]

    def body(c, carry):
        r = pl.multiple_of(c * CHUNK, CHUNK)
        rows = pl.ds(r, CHUNK)
        q = _silu(q_ref[rows, :]) * (DK_B ** -0.5)
        f = lb + (1.0 - lb) * _sigmoid(f_ref[rows, :])
        bl = _local_cumsum(jnp.log(f))
        o = _chunk_step(q, 1.0 - f, i_ref[rows, :].astype(BF16), bl, _sub_totals(bl), st_ref)
        ms = jnp.mean(o * o, axis=-1, keepdims=True)
        o_ref[rows, :] = (((o * lax.rsqrt(ms + EPS)) * gn) * _silu(g_ref[rows, :])).astype(BF16)
        return carry

    lax.fori_loop(0, tt // CHUNK, body, 0)
    _state_epilogue(sn_ref, st_ref)


def _gla_kernel(*refs, has_s0):
    if has_s0:
        q_ref, k_ref, v_ref, g_ref, low_ref, wgk_ref, bgk_ref, gn_ref, s0_ref, o_ref, sn_ref, st_ref = refs
    else:
        q_ref, k_ref, v_ref, g_ref, low_ref, wgk_ref, bgk_ref, gn_ref, o_ref, sn_ref, st_ref = refs
        s0_ref = None
    _state_prologue(s0_ref, st_ref)
    tt = q_ref.shape[0]
    gn = gn_ref[...]
    bgk = bgk_ref[...]

    def body(c, carry):
        r = pl.multiple_of(c * CHUNK, CHUNK)
        rows = pl.ds(r, CHUNK)
        x = jnp.dot(low_ref[rows, :].astype(BF16), wgk_ref[...], preferred_element_type=F32) + bgk
        lg = (jnp.minimum(x, 0.0) - jnp.log1p(jnp.exp(-jnp.abs(x)))) * (1.0 / GATE_NORMALIZER)
        bl = _local_cumsum(lg)
        q = q_ref[rows, :] * (DK_C ** -0.5)
        o = _chunk_step(q, k_ref[rows, :], v_ref[rows, :].astype(BF16), bl, _sub_totals(bl), st_ref)
        ms = jnp.mean(o * o, axis=-1, keepdims=True)
        o_ref[rows, :] = (((o * lax.rsqrt(ms + EPS)) * gn) * _silu(g_ref[rows, :])).astype(BF16)
        return carry

    lax.fori_loop(0, tt // CHUNK, body, 0)
    _state_epilogue(sn_ref, st_ref)


def _col_spec(tt, width, nt, first_block):
    return pl.BlockSpec((tt, width), lambda b, h, t: (b * nt + t, first_block + h))


def _recurrence_call(body, grid, in_specs, args, s0, bsz, seq, heads, dk, dv, tt):
    nt = seq // tt
    if s0 is not None:
        in_specs = in_specs + [pl.BlockSpec((1, 1, dk, dv), lambda b, h, t: (b, h, 0, 0))]
        args = args + [s0]
    o, sn = pl.pallas_call(
        body,
        grid=grid,
        in_specs=in_specs,
        out_specs=[
            pl.BlockSpec((tt, dv), lambda b, h, t: (b * nt + t, h)),
            pl.BlockSpec((1, 1, dk, dv), lambda b, h, t: (b, h, 0, 0)),
        ],
        out_shape=[
            jax.ShapeDtypeStruct((bsz * seq, heads * dv), BF16),
            jax.ShapeDtypeStruct((bsz, heads, dk, dv), F32),
        ],
        scratch_shapes=[pltpu.VMEM((dv, dk), F32)],
        compiler_params=_cparams(3),
        name="recurrence",
    )(*args)
    return o, sn


def _retention(p, cos2, sin2, s0, bsz, seq, tt):
    nt = seq // tt
    grid = (bsz, H_A, nt)
    qa, va = H_A * DK_A, H_A * DV_A
    lgam = np.log1p(-np.exp2(-5.0 - np.arange(H_A, dtype=np.float64)))
    lgam = jnp.asarray(np.broadcast_to(lgam[:, None, None], (H_A, 1, DK_A)), F32)
    in_specs = [
        _col_spec(tt, DK_A, nt, 0),
        _col_spec(tt, DK_A, nt, qa // DK_A),
        _col_spec(tt, DV_A, nt, 2 * qa // DV_A),
        _col_spec(tt, DV_A, nt, (2 * qa + va) // DV_A),
        pl.BlockSpec((tt, DK_A), lambda b, h, t: (t, 0)),
        pl.BlockSpec((tt, DK_A), lambda b, h, t: (t, 0)),
        pl.BlockSpec((1, 1, DK_A), lambda b, h, t: (h, 0, 0)),
    ]
    args = [p, p, p, p, cos2, sin2, lgam]
    body = functools.partial(_ret_kernel, has_s0=s0 is not None)
    return _recurrence_call(body, grid, in_specs, args, s0, bsz, seq, H_A, DK_A, DV_A, tt)


def _hgrn(p, lbp, gnorm, s0, bsz, seq, tt, layer):
    nt = seq // tt
    grid = (bsz, H_B, nt)
    base = 2 * H_A * DK_A + 2 * H_A * DV_A
    qb = H_B * DK_B
    in_specs = [
        _col_spec(tt, DK_B, nt, base // DK_B),
        _col_spec(tt, DK_B, nt, (base + qb) // DK_B),
        _col_spec(tt, DV_B, nt, (base + 2 * qb) // DV_B),
        _col_spec(tt, DV_B, nt, (base + 2 * qb + H_B * DV_B) // DV_B),
        pl.BlockSpec((1, lbp.shape[1], DK_B), lambda b, h, t: (h, 0, 0)),
        pl.BlockSpec((1, DV_B), lambda b, h, t: (0, 0)),
    ]
    args = [p, p, p, p, lbp, gnorm.reshape(1, DV_B)]
    body = functools.partial(_hgrn_kernel, has_s0=s0 is not None, layer=layer)
    return _recurrence_call(body, grid, in_specs, args, s0, bsz, seq, H_B, DK_B, DV_B, tt)


def _gla(p, low, wgk, bgk, gnorm, s0, bsz, seq, tt):
    nt = seq // tt
    grid = (bsz, H_C, nt)
    qc, vc = H_C * DK_C, H_C * DV_C
    in_specs = [
        _col_spec(tt, DK_C, nt, 0),
        _col_spec(tt, DK_C, nt, qc // DK_C),
        _col_spec(tt, DV_C, nt, 2 * qc // DV_C),
        _col_spec(tt, DV_C, nt, (2 * qc + vc) // DV_C),
        pl.BlockSpec((tt, low.shape[1]), lambda b, h, t: (b * nt + t, 0)),
        pl.BlockSpec((wgk.shape[0], DK_C), lambda b, h, t: (0, h)),
        pl.BlockSpec((1, DK_C), lambda b, h, t: (0, h)),
        pl.BlockSpec((1, DV_C), lambda b, h, t: (0, 0)),
    ]
    args = [p, p, p, p, low, wgk, bgk.reshape(1, qc), gnorm.reshape(1, DV_C)]
    body = functools.partial(_gla_kernel, has_s0=s0 is not None)
    return _recurrence_call(body, grid, in_specs, args, s0, bsz, seq, H_C, DK_C, DV_C, tt)


def _ffn_kernel(*refs, seq_rows, tiles_per_seq, has_cache, final_norm):
    refs = list(refs)
    h_ref, nw_ref, wa_ref, wu_ref, cw_ref, cb_ref, wd_ref = refs[:7]
    pos = 7
    cache_ref = None
    nf_ref = None
    if has_cache:
        cache_ref = refs[pos]
        pos += 1
    if final_norm:
        nf_ref = refs[pos]
        pos += 1
    out_ref, newc_ref, hn_ref, carry_ref = refs[pos:pos + 4]

    i = pl.program_id(0)
    j = pl.program_id(1)
    tm, tf = h_ref.shape[0], wa_ref.shape[1]

    @pl.when(j == 0)
    def _():
        _rmsnorm_rows_to(h_ref, nw_ref, hn_ref, 128)
        out_ref[...] = h_ref[...]

    hn = hn_ref[...]
    a = jnp.dot(hn, wa_ref[...], preferred_element_type=F32)
    u = jnp.dot(hn, wu_ref[...], preferred_element_type=F32)

    row = lax.broadcasted_iota(jnp.int32, (tm, 1), 0)
    if tiles_per_seq > 1:
        @pl.when(i % tiles_per_seq == 0)
        def _():
            carry_ref[j] = jnp.zeros((CONV_W - 1, tf), F32)

        prev = carry_ref[j]
        p0 = prev[0:1, :]
        p1 = prev[1:2, :]
        rmod = row
    else:
        nseq = tm // seq_rows
        if has_cache:
            p0 = jnp.concatenate(
                [jnp.broadcast_to(cache_ref[s, 0:1, :], (seq_rows, tf)) for s in range(nseq)], axis=0)
            p1 = jnp.concatenate(
                [jnp.broadcast_to(cache_ref[s, 1:2, :], (seq_rows, tf)) for s in range(nseq)], axis=0)
        else:
            p0 = jnp.zeros((1, tf), F32)
            p1 = p0
        rmod = row & (seq_rows - 1)
    a1 = jnp.where(rmod == 0, p1, pltpu.roll(a, 1, 0))
    a2 = jnp.where(rmod == 0, p0, jnp.where(rmod == 1, p1, pltpu.roll(a, 2, 0)))
    cw = cw_ref[...]
    ac = cb_ref[...] + a2 * cw[0:1, :] + a1 * cw[1:2, :] + a * cw[2:3, :]

    if tiles_per_seq > 1:
        carry_ref[j] = a[tm - (CONV_W - 1):, :]
        newc_ref[0] = a[tm - (CONV_W - 1):, :]
    else:
        for s in range(tm // seq_rows):
            newc_ref[s] = a[(s + 1) * seq_rows - (CONV_W - 1):(s + 1) * seq_rows, :]

    y = (_silu(ac) * u).astype(BF16)
    out_ref[...] += jnp.dot(y, wd_ref[...], preferred_element_type=F32)

    if final_norm:
        @pl.when(j == pl.num_programs(1) - 1)
        def _():
            rows = min(128, tm)

            def body(c, carry):
                r = pl.multiple_of(c * rows, rows)
                x = out_ref[pl.ds(r, rows), :]
                ms = jnp.mean(x * x, axis=-1, keepdims=True)
                out_ref[pl.ds(r, rows), :] = (x * lax.rsqrt(ms + EPS)) * nf_ref[...]
                return carry

            lax.fori_loop(0, tm // rows, body, 0)


def _ffn(h, nw, wa, wu, cw, cb, wd, cache, nf, *, bsz, seq, tm, tf):
    m, d = h.shape
    dffp = wa.shape[1]
    nj = dffp // tf
    if seq >= tm:
        tiles_per_seq = seq // tm
        seq_rows = tm
        seqs_per_tile = 1
    else:
        tiles_per_seq = 1
        seq_rows = seq
        seqs_per_tile = tm // seq
    assert seq_rows & (seq_rows - 1) == 0
    grid = (m // tm, nj)
    in_specs = [
        pl.BlockSpec((tm, d), lambda i, j: (i, 0)),
        pl.BlockSpec((1, d), lambda i, j: (0, 0)),
        pl.BlockSpec((d, tf), lambda i, j: (0, j)),
        pl.BlockSpec((d, tf), lambda i, j: (0, j)),
        pl.BlockSpec((CONV_W, tf), lambda i, j: (0, j)),
        pl.BlockSpec((1, tf), lambda i, j: (0, j)),
        pl.BlockSpec((tf, d), lambda i, j: (j, 0)),
    ]
    args = [h, nw.reshape(1, d), wa, wu, cw, cb, wd]
    if cache is not None:
        assert tiles_per_seq == 1
        in_specs.append(pl.BlockSpec((seqs_per_tile, CONV_W - 1, tf), lambda i, j: (i, 0, j)))
        args.append(cache)
    if nf is not None:
        in_specs.append(pl.BlockSpec((1, d), lambda i, j: (0, 0)))
        args.append(nf.reshape(1, d))
    newc_spec = pl.BlockSpec((seqs_per_tile, CONV_W - 1, tf), lambda i, j: (i, 0, j))
    body = functools.partial(
        _ffn_kernel, seq_rows=seq_rows, tiles_per_seq=tiles_per_seq,
        has_cache=cache is not None, final_norm=nf is not None)
    out, newc = pl.pallas_call(
        body,
        grid=grid,
        in_specs=in_specs,
        out_specs=[pl.BlockSpec((tm, d), lambda i, j: (i, 0)), newc_spec],
        out_shape=[
            jax.ShapeDtypeStruct((m, d), F32),
            jax.ShapeDtypeStruct((grid[0] * seqs_per_tile, CONV_W - 1, dffp), F32),
        ],
        scratch_shapes=[
            pltpu.VMEM((tm, d), BF16),
            pltpu.VMEM((nj, CONV_W - 1, tf), F32),
        ],
        compiler_params=_cparams(2),
        name="ffn",
    )(*args)
    if tiles_per_seq > 1:
        newc = newc.reshape(bsz, tiles_per_seq, CONV_W - 1, dffp)[:, -1]
    return out, newc


def _pad_to(x, axis, size):
    pad = size - x.shape[axis]
    if pad == 0:
        return x
    widths = [(0, 0)] * x.ndim
    widths[axis] = (0, pad)
    return jnp.pad(x, widths)


def _rotary_tables(pos):
    half = DK_A // 2
    inv = ROPE_BASE ** (-jnp.arange(half, dtype=F32) / half)
    ang = pos.astype(F32)[:, None] * inv[None, :]
    cos, sin = jnp.cos(ang), jnp.sin(ang)
    return jnp.concatenate([cos, cos], axis=-1), jnp.concatenate([-sin, sin], axis=-1)


def _prepare_weights(w_in_even, w_out_even, hgrn_lb, w_in_odd, w_gk2, w_out_odd,
                     ffn_w_up, ffn_conv_w, ffn_conv_b, ffn_w_down, tf):
    dff = ffn_w_down.shape[1]
    dffp = -(-dff // tf) * tf
    main_odd = w_in_odd.shape[2] - GK_RANK
    w = {
        "in_even": w_in_even.astype(BF16),
        "out_even": w_out_even.astype(BF16),
        "in_odd": w_in_odd[:, :, :main_odd].astype(BF16),
        "low_odd": _pad_to(w_in_odd[:, :, main_odd:], 2, LANES).astype(BF16),
        "gk2": _pad_to(w_gk2, 1, LANES).astype(BF16),
        "out_odd": w_out_odd.astype(BF16),
        "up_a": _pad_to(ffn_w_up[:, :, :dff], 2, dffp).astype(BF16),
        "up_u": _pad_to(ffn_w_up[:, :, dff:], 2, dffp).astype(BF16),
        "conv_w": _pad_to(ffn_conv_w, 2, dffp),
        "conv_b": _pad_to(ffn_conv_b, 1, dffp)[:, None, :],
        "down": _pad_to(ffn_w_down, 1, dffp).astype(BF16),
        "lbp": jnp.transpose(hgrn_lb.reshape(hgrn_lb.shape[0], H_B, DK_B), (1, 0, 2)),
    }
    return w, dff, dffp


def _tiles(m, seq):
    return min(1024, m), 1024, min(512, m), min(512, seq)


def _run_group(x, pos0, s_ret, s_hgrn, s_gla, s_conv, norm_mix, norm_ffn, norm_final,
               hgrn_gnorm, b_gk2, gla_gnorm, w, dff, dffp, tf):
    bsz, seq, d = x.shape
    m = bsz * seq
    depth = norm_mix.shape[0]
    tm, tn, tm_ffn, tt = _tiles(m, seq)
    cos2, sin2 = _rotary_tables(pos0 + jnp.arange(seq, dtype=jnp.int32))

    h = x.reshape(m, d)
    new_ret, new_hgrn, new_gla, new_conv = [], [], [], []
    for l in range(depth):
        i = l // 2
        if l % 2 == 0:
            p = _norm_matmul(h, norm_mix[l], w["in_even"][i], tm=tm, tn=tn)
            o_a, sr = _retention(p, cos2, sin2, None if s_ret is None else s_ret[i], bsz, seq, tt)
            o_b, sh = _hgrn(p, w["lbp"], hgrn_gnorm[i], None if s_hgrn is None else s_hgrn[i],
                            bsz, seq, tt, l)
            va = H_A * DV_A
            h = _matmul_res([o_a, o_b], [w["out_even"][i][:va], w["out_even"][i][va:]], h,
                            tm=tm, tn=tn)
            new_ret.append(sr)
            new_hgrn.append(sh)
        else:
            p, low = _norm_matmul(h, norm_mix[l], w["in_odd"][i], w["low_odd"][i], tm=tm, tn=tn)
            o_c, sg = _gla(p, low, w["gk2"][i], b_gk2[i], gla_gnorm[i],
                           None if s_gla is None else s_gla[i], bsz, seq, tt)
            h = _matmul_res([o_c], [w["out_odd"][i]], h, tm=tm, tn=tn)
            new_gla.append(sg)
        cache = None if s_conv is None else _pad_to(s_conv[l], 2, dffp)
        h, nc = _ffn(h, norm_ffn[l], w["up_a"][l], w["up_u"][l], w["conv_w"][l], w["conv_b"][l],
                     w["down"][l], cache, norm_final if l == depth - 1 else None,
                     bsz=bsz, seq=seq, tm=tm_ffn, tf=tf)
        new_conv.append(nc[:, :, :dff])
    return (h.reshape(bsz, seq, d), jnp.stack(new_ret), jnp.stack(new_hgrn),
            jnp.stack(new_gla), jnp.stack(new_conv))


def kernel(x_prompt, x_sample, state_ret, state_hgrn, state_gla, cache_ffn_conv, norm_mix, norm_ffn, norm_final, w_in_even, w_out_even, hgrn_lb, hgrn_gnorm, w_in_odd, w_gk2, b_gk2, gla_gnorm, w_out_odd, ffn_w_up, ffn_conv_w, ffn_conv_b, ffn_w_down):
    tf = 512
    w, dff, dffp = _prepare_weights(w_in_even, w_out_even, hgrn_lb, w_in_odd, w_gk2, w_out_odd,
                                    ffn_w_up, ffn_conv_w, ffn_conv_b, ffn_w_down, tf)
    common = (norm_mix, norm_ffn, norm_final, hgrn_gnorm, b_gk2, gla_gnorm, w, dff, dffp, tf)
    y_p, ret_p, hgrn_p, gla_p, conv_p = _run_group(x_prompt, 0, None, None, None, None, *common)
    y_s, ret_s, hgrn_s, gla_s, conv_s = _run_group(
        x_sample, PAST_LEN, state_ret, state_hgrn, state_gla, cache_ffn_conv, *common)
    return (y_p, y_s, ret_p, ret_s, hgrn_p, hgrn_s, gla_p, gla_s, conv_p, conv_s)
```

```python
import functools

import numpy as np
import jax
import jax.numpy as jnp
from jax import lax
from jax.experimental import pallas as pl
from jax.experimental.pallas import tpu as pltpu

F32 = jnp.float32
BF16 = jnp.bfloat16

EPS = 1e-6
ROPE_BASE = 10000.0
GATE_NORMALIZER = 16.0
PAST_LEN = 1024

H_A, DK_A, DV_A = 4, 128, 256
H_B, DK_B, DV_B = 8, 128, 128
H_C, DK_C, DV_C = 4, 256, 512
GK_RANK = 16
CONV_W = 3

LANES = 128
SUBLANES = 8
VMEM_LIMIT_BYTES = 52 * 1024 * 1024

CHUNK = 64
SUB = 16
SUB_SHIFT = 4
assert 1 << SUB_SHIFT == SUB and CHUNK % SUB == 0

NT_DIMS = (((1,), (1,)), ((), ()))
TN_DIMS = (((0,), (0,)), ((), ()))


def _cparams(n_axes):
    return pltpu.CompilerParams(
        dimension_semantics=("arbitrary",) * n_axes,
        vmem_limit_bytes=VMEM_LIMIT_BYTES,
    )


def _sigmoid(x):
    return 1.0 / (1.0 + jnp.exp(-x))


def _silu(x):
    return x * _sigmoid(x)


def _rmsnorm_rows_to(x_ref, nw_ref, dst_ref, rows):
    rows = min(rows, x_ref.shape[0])
    assert x_ref.shape[0] % rows == 0
    n = x_ref.shape[0] // rows

    def body(c, carry):
        r = pl.multiple_of(c * rows, rows)
        x = x_ref[pl.ds(r, rows), :]
        ms = jnp.mean(x * x, axis=-1, keepdims=True)
        dst_ref[pl.ds(r, rows), :] = ((x * lax.rsqrt(ms + EPS)) * nw_ref[...]).astype(BF16)
        return carry

    lax.fori_loop(0, n, body, 0)


def _norm_matmul_kernel(x_ref, nw_ref, w_ref, o_ref, hn_ref):
    @pl.when(pl.program_id(1) == 0)
    def _():
        _rmsnorm_rows_to(x_ref, nw_ref, hn_ref, 128)

    o_ref[...] = jnp.dot(hn_ref[...], w_ref[...], preferred_element_type=F32)


def _norm_matmul_low_kernel(x_ref, nw_ref, w_ref, wl_ref, o_ref, low_ref, hn_ref):
    @pl.when(pl.program_id(1) == 0)
    def _():
        _rmsnorm_rows_to(x_ref, nw_ref, hn_ref, 128)
        low_ref[...] = jnp.dot(hn_ref[...], wl_ref[...], preferred_element_type=F32)

    o_ref[...] = jnp.dot(hn_ref[...], w_ref[...], preferred_element_type=F32)


def _norm_matmul(x, nw, w, layer, n, w_low=None, *, tm, tn):
    m, d = x.shape
    assert n % tn == 0 and n <= w.shape[2]
    grid = (m // tm, n // tn)
    in_specs = [
        pl.BlockSpec((tm, d), lambda i, j: (i, 0)),
        pl.BlockSpec((1, d), lambda i, j: (0, 0)),
        pl.BlockSpec((None, d, tn), lambda i, j: (layer, 0, j)),
    ]
    out_shape = [jax.ShapeDtypeStruct((m, n), F32)]
    out_specs = [pl.BlockSpec((tm, tn), lambda i, j: (i, j))]
    args = [x, nw.reshape(1, d), w]
    if w_low is None:
        body = _norm_matmul_kernel
    else:
        body = _norm_matmul_low_kernel
        nl = w_low.shape[2]
        in_specs.append(pl.BlockSpec((None, d, nl), lambda i, j: (layer, 0, 0)))
        out_shape.append(jax.ShapeDtypeStruct((m, nl), F32))
        out_specs.append(pl.BlockSpec((tm, nl), lambda i, j: (i, 0)))
        args.append(w_low)
    outs = pl.pallas_call(
        body,
        grid=grid,
        in_specs=in_specs,
        out_specs=out_specs,
        out_shape=out_shape,
        scratch_shapes=[pltpu.VMEM((tm, d), BF16)],
        compiler_params=_cparams(2),
        name="norm_matmul",
    )(*args)
    return outs if w_low is not None else outs[0]


def _matmul_res_kernel(*refs, n_in):
    a_refs = refs[:n_in]
    w_refs = refs[n_in:2 * n_in]
    res_ref = refs[2 * n_in]
    o_ref = refs[2 * n_in + 1]
    acc = res_ref[...]
    for a_ref, w_ref in zip(a_refs, w_refs):
        acc = acc + jnp.dot(a_ref[...], w_ref[...], preferred_element_type=F32)
    o_ref[...] = acc


def _matmul_res(a_list, w, layer, res, *, tm, tn):
    m, n = res.shape
    ka = a_list[0].shape[1]
    assert all(a.shape[1] == ka for a in a_list) and ka * len(a_list) == w.shape[1]
    grid = (m // tm, n // tn)
    in_specs = [pl.BlockSpec((tm, ka), lambda i, j: (i, 0)) for _ in a_list]
    for rb in range(len(a_list)):
        in_specs.append(pl.BlockSpec((None, ka, tn), lambda i, j, rb=rb: (layer, rb, j)))
    in_specs.append(pl.BlockSpec((tm, tn), lambda i, j: (i, j)))
    return pl.pallas_call(
        functools.partial(_matmul_res_kernel, n_in=len(a_list)),
        grid=grid,
        in_specs=in_specs,
        out_specs=pl.BlockSpec((tm, tn), lambda i, j: (i, j)),
        out_shape=jax.ShapeDtypeStruct((m, n), F32),
        compiler_params=_cparams(2),
        name="matmul_res",
    )(*a_list, *([w] * len(a_list)), res)


def _local_cumsum(lg):
    pos = lax.broadcasted_iota(jnp.int32, lg.shape, 0) & (SUB - 1)
    x = lg
    s = 1
    while s < SUB:
        x = x + jnp.where(pos >= s, pltpu.roll(x, s, 0), 0.0)
        s *= 2
    return x


def _chunk_step(q, k, vb, bl, tot, st_ref):
    c = q.shape[0]
    ns = c // SUB
    a = [jnp.zeros_like(tot[0])]
    for m in range(ns):
        a.append(a[-1] + tot[m])
    b_last = a[ns]

    qt, kbar, khat, qin, kout = [], [], [], [], []
    qd = {d: [] for d in range(2, ns)}
    for m in range(ns):
        sl = slice(m * SUB, (m + 1) * SUB)
        blm = bl[sl]
        qtm = q[sl] * jnp.exp(blm)
        km = k[sl]
        khm = km * jnp.exp(tot[m] - blm)
        qt.append(qtm)
        kbar.append(km * jnp.exp(-blm))
        khat.append(khm)
        qin.append(qtm * jnp.exp(a[m]))
        kout.append(khm * jnp.exp(b_last - a[m + 1]))
        for d in range(2, ns):
            if m >= d:
                qd[d].append(qtm * jnp.exp(a[m] - a[m - d + 1]))
            else:
                qd[d].append(jnp.zeros_like(qtm))

    def cat(xs):
        return jnp.concatenate(xs, axis=0).astype(BF16)

    qt_b, kbar_b, khat_b = cat(qt), cat(kbar), cat(khat)
    ri = lax.broadcasted_iota(jnp.int32, (c, c), 0)
    ci = lax.broadcasted_iota(jnp.int32, (c, c), 1)
    dist = (ri >> SUB_SHIFT) - (ci >> SUB_SHIFT)
    s0 = lax.dot_general(qt_b, kbar_b, NT_DIMS, preferred_element_type=F32)
    s1 = lax.dot_general(qt_b, khat_b, NT_DIMS, preferred_element_type=F32)
    scores = jnp.where((dist == 0) & (ri >= ci), s0, jnp.where(dist == 1, s1, 0.0))
    for d in range(2, ns):
        sd = lax.dot_general(cat(qd[d]), khat_b, NT_DIMS, preferred_element_type=F32)
        scores = jnp.where(dist == d, sd, scores)

    st = st_ref[...]
    o = jnp.dot(scores.astype(BF16), vb, preferred_element_type=F32)
    o = o + lax.dot_general(cat(qin), st.astype(BF16), NT_DIMS, preferred_element_type=F32)
    st_ref[...] = st * jnp.exp(b_last) + lax.dot_general(
        vb, cat(kout), TN_DIMS, preferred_element_type=F32)
    return o


def _sub_totals(bl):
    return [bl[(m + 1) * SUB - 1:(m + 1) * SUB, :] for m in range(bl.shape[0] // SUB)]


def _state_prologue(s0_ref, st_ref):
    @pl.when(pl.program_id(1) == 0)
    def _():
        for h in range(st_ref.shape[0]):
            if s0_ref is None:
                st_ref[h] = jnp.zeros(st_ref.shape[1:], F32)
            else:
                st_ref[h] = jnp.transpose(s0_ref[0, h])


def _state_epilogue(sn_ref, st_ref):
    @pl.when(pl.program_id(1) == pl.num_programs(1) - 1)
    def _():
        for h in range(st_ref.shape[0]):
            sn_ref[0, h] = jnp.transpose(st_ref[h])


def _chunk_loop(n_chunks, body, unroll):
    def step(c, carry):
        body(pl.ds(pl.multiple_of(c * CHUNK, CHUNK), CHUNK))
        return carry

    lax.fori_loop(0, n_chunks, step, 0, unroll=min(unroll, n_chunks))


def _head(ref, rows, h, width):
    return ref[rows, h * width:(h + 1) * width]


def _ret_kernel(*refs, has_s0):
    if has_s0:
        q_ref, k_ref, v_ref, g_ref, cos_ref, sin_ref, s0_ref, o_ref, sn_ref, st_ref = refs
    else:
        q_ref, k_ref, v_ref, g_ref, cos_ref, sin_ref, o_ref, sn_ref, st_ref = refs
        s0_ref = None
    _state_prologue(s0_ref, st_ref)
    pos1 = ((lax.broadcasted_iota(jnp.int32, (CHUNK, 1), 0) & (SUB - 1)) + 1).astype(F32)
    half = DK_A // 2

    def body(rows):
        cos, sin = cos_ref[rows, :], sin_ref[rows, :]
        for h in range(H_A):
            lgam = float(np.log1p(-np.exp2(-5.0 - h)))
            bl = pos1 * lgam
            tot = [jnp.full((1, 1), lgam * SUB, F32)] * (CHUNK // SUB)
            qr, kr = _head(q_ref, rows, h, DK_A), _head(k_ref, rows, h, DK_A)
            q = qr * cos + pltpu.roll(qr, half, 1) * sin
            k = (kr * cos + pltpu.roll(kr, half, 1) * sin) * (DK_A ** -0.5)
            o = _chunk_step(q, k, _head(v_ref, rows, h, DV_A).astype(BF16), bl, tot, st_ref.at[h])
            mu = jnp.mean(o, axis=-1, keepdims=True)
            oc = o - mu
            var = jnp.mean(oc * oc, axis=-1, keepdims=True)
            o_ref[rows, h * DV_A:(h + 1) * DV_A] = (
                (oc * lax.rsqrt(var + EPS)) * _silu(_head(g_ref, rows, h, DV_A))).astype(BF16)

    _chunk_loop(q_ref.shape[0] // CHUNK, body, 2)
    _state_epilogue(sn_ref, st_ref)


def _hgrn_kernel(*refs, has_s0, layer):
    if has_s0:
        q_ref, f_ref, i_ref, g_ref, lbp_ref, gn_ref, s0_ref, o_ref, sn_ref, st_ref = refs
    else:
        q_ref, f_ref, i_ref, g_ref, lbp_ref, gn_ref, o_ref, sn_ref, st_ref = refs
        s0_ref = None
    _state_prologue(s0_ref, st_ref)
    lbp = lbp_ref[...]
    e = jnp.exp(lbp - jnp.max(lbp, axis=0, keepdims=True))
    sm = e / jnp.sum(e, axis=0, keepdims=True)
    lb_all = jnp.sum(sm[:layer + 1], axis=0, keepdims=True)
    gn = gn_ref[...]

    def body(rows):
        for h in range(H_B):
            lb = lb_all[:, h * DK_B:(h + 1) * DK_B]
            q = _silu(_head(q_ref, rows, h, DK_B)) * (DK_B ** -0.5)
            f = lb + (1.0 - lb) * _sigmoid(_head(f_ref, rows, h, DK_B))
            bl = _local_cumsum(jnp.log(f))
            o = _chunk_step(q, 1.0 - f, _head(i_ref, rows, h, DV_B).astype(BF16), bl,
                            _sub_totals(bl), st_ref.at[h])
            ms = jnp.mean(o * o, axis=-1, keepdims=True)
            o_ref[rows, h * DV_B:(h + 1) * DV_B] = (
                ((o * lax.rsqrt(ms + EPS)) * gn) * _silu(_head(g_ref, rows, h, DV_B))).astype(BF16)

    _chunk_loop(q_ref.shape[0] // CHUNK, body, 1)
    _state_epilogue(sn_ref, st_ref)


def _gla_kernel(*refs, has_s0):
    if has_s0:
        q_ref, k_ref, v_ref, g_ref, low_ref, wgk_ref, bgk_ref, gn_ref, s0_ref, o_ref, sn_ref, st_ref = refs
    else:
        q_ref, k_ref, v_ref, g_ref, low_ref, wgk_ref, bgk_ref, gn_ref, o_ref, sn_ref, st_ref = refs
        s0_ref = None
    _state_prologue(s0_ref, st_ref)
    gn = gn_ref[...]
    bgk = bgk_ref[...]

    def body(rows):
        x = jnp.dot(low_ref[rows, :].astype(BF16), wgk_ref[...], preferred_element_type=F32) + bgk
        lg_all = (jnp.minimum(x, 0.0) - jnp.log1p(jnp.exp(-jnp.abs(x)))) * (1.0 / GATE_NORMALIZER)
        for h in range(H_C):
            bl = _local_cumsum(lg_all[:, h * DK_C:(h + 1) * DK_C])
            q = _head(q_ref, rows, h, DK_C) * (DK_C ** -0.5)
            o = _chunk_step(q, _head(k_ref, rows, h, DK_C), _head(v_ref, rows, h, DV_C).astype(BF16),
                            bl, _sub_totals(bl), st_ref.at[h])
            ms = jnp.mean(o * o, axis=-1, keepdims=True)
            o_ref[rows, h * DV_C:(h + 1) * DV_C] = (
                ((o * lax.rsqrt(ms + EPS)) * gn) * _silu(_head(g_ref, rows, h, DV_C))).astype(BF16)

    _chunk_loop(q_ref.shape[0] // CHUNK, body, 1)
    _state_epilogue(sn_ref, st_ref)


def _row_spec(tt, width, nt, block):
    return pl.BlockSpec((tt, width), lambda b, t: (b * nt + t, block))


def _recurrence_call(body, in_specs, args, s0, bsz, seq, heads, dk, dv, tt):
    nt = seq // tt
    if s0 is not None:
        in_specs = in_specs + [pl.BlockSpec((1, heads, dk, dv), lambda b, t: (b, 0, 0, 0))]
        args = args + [s0]
    o, sn = pl.pallas_call(
        body,
        grid=(bsz, nt),
        in_specs=in_specs,
        out_specs=[
            pl.BlockSpec((tt, heads * dv), lambda b, t: (b * nt + t, 0)),
            pl.BlockSpec((1, heads, dk, dv), lambda b, t: (b, 0, 0, 0)),
        ],
        out_shape=[
            jax.ShapeDtypeStruct((bsz * seq, heads * dv), BF16),
            jax.ShapeDtypeStruct((bsz, heads, dk, dv), F32),
        ],
        scratch_shapes=[pltpu.VMEM((heads, dv, dk), F32)],
        compiler_params=_cparams(2),
        name="recurrence",
    )(*args)
    return o, sn


def _retention(p, cos2, sin2, s0, bsz, seq, tt):
    nt = seq // tt
    qa, va = H_A * DK_A, H_A * DV_A
    in_specs = [
        _row_spec(tt, qa, nt, 0),
        _row_spec(tt, qa, nt, 1),
        _row_spec(tt, va, nt, 2 * qa // va),
        _row_spec(tt, va, nt, (2 * qa + va) // va),
        pl.BlockSpec((tt, DK_A), lambda b, t: (t, 0)),
        pl.BlockSpec((tt, DK_A), lambda b, t: (t, 0)),
    ]
    args = [p, p, p, p, cos2, sin2]
    body = functools.partial(_ret_kernel, has_s0=s0 is not None)
    return _recurrence_call(body, in_specs, args, s0, bsz, seq, H_A, DK_A, DV_A, tt)


def _hgrn(p, lbp, gnorm, s0, bsz, seq, tt, layer):
    nt = seq // tt
    base = 2 * H_A * DK_A + 2 * H_A * DV_A
    qb = H_B * DK_B
    assert base % qb == 0 and H_B * DV_B == qb
    in_specs = [_row_spec(tt, qb, nt, base // qb + n) for n in range(4)]
    in_specs += [
        pl.BlockSpec(lbp.shape, lambda b, t: (0, 0)),
        pl.BlockSpec((1, DV_B), lambda b, t: (0, 0)),
    ]
    args = [p, p, p, p, lbp, gnorm.reshape(1, DV_B)]
    body = functools.partial(_hgrn_kernel, has_s0=s0 is not None, layer=layer)
    return _recurrence_call(body, in_specs, args, s0, bsz, seq, H_B, DK_B, DV_B, tt)


def _gla(p, low, wgk, layer, bgk, gnorm, s0, bsz, seq, tt):
    nt = seq // tt
    qc, vc = H_C * DK_C, H_C * DV_C
    in_specs = [
        _row_spec(tt, qc, nt, 0),
        _row_spec(tt, qc, nt, 1),
        _row_spec(tt, vc, nt, 2 * qc // vc),
        _row_spec(tt, vc, nt, (2 * qc + vc) // vc),
        _row_spec(tt, low.shape[1], nt, 0),
        pl.BlockSpec((None,) + wgk.shape[1:], lambda b, t: (layer, 0, 0)),
        pl.BlockSpec((1, qc), lambda b, t: (0, 0)),
        pl.BlockSpec((1, DV_C), lambda b, t: (0, 0)),
    ]
    args = [p, p, p, p, low, wgk, bgk.reshape(1, qc), gnorm.reshape(1, DV_C)]
    body = functools.partial(_gla_kernel, has_s0=s0 is not None)
    return _recurrence_call(body, in_specs, args, s0, bsz, seq, H_C, DK_C, DV_C, tt)


def _ffn_kernel(*refs, seq_rows, tiles_per_seq, has_cache, final_norm):
    refs = list(refs)
    h_ref, nw_ref, wa_ref, wu_ref, cw_ref, cb_ref, wd_ref = refs[:7]
    pos = 7
    cache_ref = None
    nf_ref = None
    if has_cache:
        cache_ref = refs[pos]
        pos += 1
    if final_norm:
        nf_ref = refs[pos]
        pos += 1
    out_ref, newc_ref, hn_ref, carry_ref = refs[pos:pos + 4]

    i = pl.program_id(0)
    j = pl.program_id(1)
    tm, tf = h_ref.shape[0], wa_ref.shape[1]

    @pl.when(j == 0)
    def _():
        _rmsnorm_rows_to(h_ref, nw_ref, hn_ref, 128)
        out_ref[...] = h_ref[...]

    hn = hn_ref[...]
    a = jnp.dot(hn, wa_ref[...], preferred_element_type=F32)
    u = jnp.dot(hn, wu_ref[...], preferred_element_type=F32)

    row = lax.broadcasted_iota(jnp.int32, (tm, 1), 0)
    if tiles_per_seq > 1:
        @pl.when(i % tiles_per_seq == 0)
        def _():
            carry_ref[j] = jnp.zeros((CONV_W - 1, tf), F32)

        prev = carry_ref[j]
        p0 = prev[0:1, :]
        p1 = prev[1:2, :]
        rmod = row
    else:
        nseq = tm // seq_rows
        if has_cache:
            p0 = jnp.concatenate(
                [jnp.broadcast_to(cache_ref[s, 0:1, :], (seq_rows, tf)) for s in range(nseq)], axis=0)
            p1 = jnp.concatenate(
                [jnp.broadcast_to(cache_ref[s, 1:2, :], (seq_rows, tf)) for s in range(nseq)], axis=0)
        else:
            p0 = jnp.zeros((1, tf), F32)
            p1 = p0
        rmod = row & (seq_rows - 1)
    a1 = jnp.where(rmod == 0, p1, pltpu.roll(a, 1, 0))
    a2 = jnp.where(rmod == 0, p0, jnp.where(rmod == 1, p1, pltpu.roll(a, 2, 0)))
    cw = cw_ref[...]
    ac = cb_ref[...] + a2 * cw[0:1, :] + a1 * cw[1:2, :] + a * cw[2:3, :]

    if tiles_per_seq > 1:
        carry_ref[j] = a[tm - (CONV_W - 1):, :]
        newc_ref[0] = a[tm - (CONV_W - 1):, :]
    else:
        for s in range(tm // seq_rows):
            newc_ref[s] = a[(s + 1) * seq_rows - (CONV_W - 1):(s + 1) * seq_rows, :]

    y = (_silu(ac) * u).astype(BF16)
    out_ref[...] += jnp.dot(y, wd_ref[...], preferred_element_type=F32)

    if final_norm:
        @pl.when(j == pl.num_programs(1) - 1)
        def _():
            rows = min(128, tm)

            def body(c, carry):
                r = pl.multiple_of(c * rows, rows)
                x = out_ref[pl.ds(r, rows), :]
                ms = jnp.mean(x * x, axis=-1, keepdims=True)
                out_ref[pl.ds(r, rows), :] = (x * lax.rsqrt(ms + EPS)) * nf_ref[...]
                return carry

            lax.fori_loop(0, tm // rows, body, 0)


def _ffn(h, nw, wa, wu, cw, cb, wd, layer, cache, nf, *, bsz, seq, tm, tf):
    m, d = h.shape
    dffp = wa.shape[2]
    nj = dffp // tf
    if seq >= tm:
        tiles_per_seq = seq // tm
        seq_rows = tm
        seqs_per_tile = 1
    else:
        tiles_per_seq = 1
        seq_rows = seq
        seqs_per_tile = tm // seq
    assert seq_rows & (seq_rows - 1) == 0
    grid = (m // tm, nj)
    in_specs = [
        pl.BlockSpec((tm, d), lambda i, j: (i, 0)),
        pl.BlockSpec((1, d), lambda i, j: (0, 0)),
        pl.BlockSpec((None, d, tf), lambda i, j: (layer, 0, j)),
        pl.BlockSpec((None, d, tf), lambda i, j: (layer, 0, j)),
        pl.BlockSpec((None, CONV_W, tf), lambda i, j: (layer, 0, j)),
        pl.BlockSpec((None, 1, tf), lambda i, j: (layer, 0, j)),
        pl.BlockSpec((None, tf, d), lambda i, j: (layer, j, 0)),
    ]
    args = [h, nw.reshape(1, d), wa, wu, cw, cb, wd]
    if cache is not None:
        assert tiles_per_seq == 1
        in_specs.append(pl.BlockSpec((seqs_per_tile, CONV_W - 1, tf), lambda i, j: (i, 0, j)))
        args.append(cache)
    if nf is not None:
        in_specs.append(pl.BlockSpec((1, d), lambda i, j: (0, 0)))
        args.append(nf.reshape(1, d))
    newc_spec = pl.BlockSpec((seqs_per_tile, CONV_W - 1, tf), lambda i, j: (i, 0, j))
    body = functools.partial(
        _ffn_kernel, seq_rows=seq_rows, tiles_per_seq=tiles_per_seq,
        has_cache=cache is not None, final_norm=nf is not None)
    out, newc = pl.pallas_call(
        body,
        grid=grid,
        in_specs=in_specs,
        out_specs=[pl.BlockSpec((tm, d), lambda i, j: (i, 0)), newc_spec],
        out_shape=[
            jax.ShapeDtypeStruct((m, d), F32),
            jax.ShapeDtypeStruct((grid[0] * seqs_per_tile, CONV_W - 1, dffp), F32),
        ],
        scratch_shapes=[
            pltpu.VMEM((tm, d), BF16),
            pltpu.VMEM((nj, CONV_W - 1, tf), F32),
        ],
        compiler_params=_cparams(2),
        name="ffn",
    )(*args)
    if tiles_per_seq > 1:
        newc = newc.reshape(bsz, tiles_per_seq, CONV_W - 1, dffp)[:, -1]
    return out, newc


def _pad_to(x, axis, size):
    pad = size - x.shape[axis]
    if pad == 0:
        return x
    widths = [(0, 0)] * x.ndim
    widths[axis] = (0, pad)
    return jnp.pad(x, widths)


def _rotary_tables(pos):
    half = DK_A // 2
    inv = ROPE_BASE ** (-jnp.arange(half, dtype=F32) / half)
    ang = pos.astype(F32)[:, None] * inv[None, :]
    cos, sin = jnp.cos(ang), jnp.sin(ang)
    return jnp.concatenate([cos, cos], axis=-1), jnp.concatenate([-sin, sin], axis=-1)


def _prepare_weights(w_in_even, w_out_even, hgrn_lb, w_in_odd, w_gk2, w_out_odd,
                     ffn_w_up, ffn_conv_w, ffn_conv_b, ffn_w_down, tf):
    dff = ffn_w_down.shape[1]
    dffp = -(-dff // tf) * tf
    main_odd = w_in_odd.shape[2] - GK_RANK
    w = {
        "in_even": w_in_even.astype(BF16),
        "out_even": w_out_even.astype(BF16),
        "in_odd": w_in_odd.astype(BF16),
        "low_odd": _pad_to(w_in_odd[:, :, main_odd:], 2, LANES).astype(BF16),
        "gk2": _pad_to(w_gk2, 1, LANES).astype(BF16),
        "out_odd": w_out_odd.astype(BF16),
        "up_a": _pad_to(ffn_w_up[:, :, :dff], 2, dffp).astype(BF16),
        "up_u": _pad_to(ffn_w_up[:, :, dff:], 2, dffp).astype(BF16),
        "conv_w": _pad_to(ffn_conv_w, 2, dffp),
        "conv_b": _pad_to(ffn_conv_b, 1, dffp)[:, None, :],
        "down": _pad_to(ffn_w_down, 1, dffp).astype(BF16),
        "lbp": hgrn_lb,
    }
    return w, dff, dffp


def _tiles(m, seq):
    return min(1024, m), 1024, min(512, m), min(512, seq)


def _run_group(x, pos0, s_ret, s_hgrn, s_gla, s_conv, norm_mix, norm_ffn, norm_final,
               hgrn_gnorm, b_gk2, gla_gnorm, w, dff, dffp, tf):
    bsz, seq, d = x.shape
    m = bsz * seq
    depth = norm_mix.shape[0]
    tm, tn, tm_ffn, tt = _tiles(m, seq)
    cos2, sin2 = _rotary_tables(pos0 + jnp.arange(seq, dtype=jnp.int32))

    h = x.reshape(m, d)
    new_ret, new_hgrn, new_gla, new_conv = [], [], [], []
    for l in range(depth):
        i = l // 2
        if l % 2 == 0:
            p = _norm_matmul(h, norm_mix[l], w["in_even"], i, w["in_even"].shape[2], tm=tm, tn=tn)
            o_a, sr = _retention(p, cos2, sin2, None if s_ret is None else s_ret[i], bsz, seq, tt)
            o_b, sh = _hgrn(p, w["lbp"], hgrn_gnorm[i], None if s_hgrn is None else s_hgrn[i],
                            bsz, seq, tt, l)
            h = _matmul_res([o_a, o_b], w["out_even"], i, h, tm=tm, tn=tn)
            new_ret.append(sr)
            new_hgrn.append(sh)
        else:
            p, low = _norm_matmul(h, norm_mix[l], w["in_odd"], i, w["in_odd"].shape[2] - GK_RANK,
                                  w["low_odd"], tm=tm, tn=tn)
            o_c, sg = _gla(p, low, w["gk2"], i, b_gk2[i], gla_gnorm[i],
                           None if s_gla is None else s_gla[i], bsz, seq, tt)
            h = _matmul_res([o_c], w["out_odd"], i, h, tm=tm, tn=tn)
            new_gla.append(sg)
        cache = None if s_conv is None else _pad_to(s_conv[l], 2, dffp)
        h, nc = _ffn(h, norm_ffn[l], w["up_a"], w["up_u"], w["conv_w"], w["conv_b"], w["down"], l,
                     cache, norm_final if l == depth - 1 else None,
                     bsz=bsz, seq=seq, tm=tm_ffn, tf=tf)
        new_conv.append(nc[:, :, :dff])
    return (h.reshape(bsz, seq, d), jnp.stack(new_ret), jnp.stack(new_hgrn),
            jnp.stack(new_gla), jnp.stack(new_conv))


def kernel(x_prompt, x_sample, state_ret, state_hgrn, state_gla, cache_ffn_conv, norm_mix, norm_ffn, norm_final, w_in_even, w_out_even, hgrn_lb, hgrn_gnorm, w_in_odd, w_gk2, b_gk2, gla_gnorm, w_out_odd, ffn_w_up, ffn_conv_w, ffn_conv_b, ffn_w_down):
    tf = 512
    w, dff, dffp = _prepare_weights(w_in_even, w_out_even, hgrn_lb, w_in_odd, w_gk2, w_out_odd,
                                    ffn_w_up, ffn_conv_w, ffn_conv_b, ffn_w_down, tf)
    common = (norm_mix, norm_ffn, norm_final, hgrn_gnorm, b_gk2, gla_gnorm, w, dff, dffp, tf)
    y_p, ret_p, hgrn_p, gla_p, conv_p = _run_group(x_prompt, 0, None, None, None, None, *common)
    y_s, ret_s, hgrn_s, gla_s, conv_s = _run_group(
        x_sample, PAST_LEN, state_ret, state_hgrn, state_gla, cache_ffn_conv, *common)
    return (y_p, y_s, ret_p, ret_s, hgrn_p, hgrn_s, gla_p, gla_s, conv_p, conv_s)
```

```python
import functools

import numpy as np
import jax
import jax.numpy as jnp
from jax import lax
from jax.experimental import pallas as pl
from jax.experimental.pallas import tpu as pltpu

F32 = jnp.float32
BF16 = jnp.bfloat16

EPS = 1e-6
ROPE_BASE = 10000.0
GATE_NORMALIZER = 16.0
PAST_LEN = 1024

H_A, DK_A, DV_A = 4, 128, 256
H_B, DK_B, DV_B = 8, 128, 128
H_C, DK_C, DV_C = 4, 256, 512
GK_RANK = 16
CONV_W = 3

LANES = 128
SUBLANES = 8
VMEM_LIMIT_BYTES = 52 * 1024 * 1024

CHUNK = 64
SUB = 16
SUB_SHIFT = 4
assert 1 << SUB_SHIFT == SUB and CHUNK % SUB == 0

NT_DIMS = (((1,), (1,)), ((), ()))
TN_DIMS = (((0,), (0,)), ((), ()))


def _cparams(n_axes):
    return pltpu.CompilerParams(
        dimension_semantics=("arbitrary",) * n_axes,
        vmem_limit_bytes=VMEM_LIMIT_BYTES,
    )


def _sigmoid(x):
    return 1.0 / (1.0 + jnp.exp(-x))


def _silu(x):
    return x * _sigmoid(x)


def _rmsnorm_rows_to(x_ref, nw_ref, dst_ref, rows):
    rows = min(rows, x_ref.shape[0])
    assert x_ref.shape[0] % rows == 0
    n = x_ref.shape[0] // rows

    def body(c, carry):
        r = pl.multiple_of(c * rows, rows)
        x = x_ref[pl.ds(r, rows), :]
        ms = jnp.mean(x * x, axis=-1, keepdims=True)
        dst_ref[pl.ds(r, rows), :] = ((x * lax.rsqrt(ms + EPS)) * nw_ref[...]).astype(BF16)
        return carry

    lax.fori_loop(0, n, body, 0)


def _norm_matmul_kernel(x_ref, nw_ref, w_ref, o_ref, hn_ref):
    @pl.when(pl.program_id(1) == 0)
    def _():
        _rmsnorm_rows_to(x_ref, nw_ref, hn_ref, 128)

    o_ref[...] = jnp.dot(hn_ref[...], w_ref[...], preferred_element_type=F32)


def _norm_matmul_low_kernel(x_ref, nw_ref, w_ref, wl_ref, o_ref, low_ref, hn_ref):
    @pl.when(pl.program_id(1) == 0)
    def _():
        _rmsnorm_rows_to(x_ref, nw_ref, hn_ref, 128)
        low_ref[...] = jnp.dot(hn_ref[...], wl_ref[...], preferred_element_type=F32)

    o_ref[...] = jnp.dot(hn_ref[...], w_ref[...], preferred_element_type=F32)


def _norm_matmul(x, nw, w, layer, n, w_low=None, *, tm, tn):
    m, d = x.shape
    assert n % tn == 0 and n <= w.shape[2]
    grid = (m // tm, n // tn)
    in_specs = [
        pl.BlockSpec((tm, d), lambda i, j: (i, 0)),
        pl.BlockSpec((1, d), lambda i, j: (0, 0)),
        pl.BlockSpec((None, d, tn), lambda i, j: (layer, 0, j)),
    ]
    out_shape = [jax.ShapeDtypeStruct((m, n), F32)]
    out_specs = [pl.BlockSpec((tm, tn), lambda i, j: (i, j))]
    args = [x, nw.reshape(1, d), w]
    if w_low is None:
        body = _norm_matmul_kernel
    else:
        body = _norm_matmul_low_kernel
        nl = w_low.shape[2]
        in_specs.append(pl.BlockSpec((None, d, nl), lambda i, j: (layer, 0, 0)))
        out_shape.append(jax.ShapeDtypeStruct((m, nl), F32))
        out_specs.append(pl.BlockSpec((tm, nl), lambda i, j: (i, 0)))
        args.append(w_low)
    outs = pl.pallas_call(
        body,
        grid=grid,
        in_specs=in_specs,
        out_specs=out_specs,
        out_shape=out_shape,
        scratch_shapes=[pltpu.VMEM((tm, d), BF16)],
        compiler_params=_cparams(2),
        name="norm_matmul",
    )(*args)
    return outs if w_low is not None else outs[0]


def _matmul_res_kernel(*refs, n_in):
    a_refs = refs[:n_in]
    w_refs = refs[n_in:2 * n_in]
    res_ref = refs[2 * n_in]
    o_ref = refs[2 * n_in + 1]
    acc = res_ref[...]
    for a_ref, w_ref in zip(a_refs, w_refs):
        acc = acc + jnp.dot(a_ref[...], w_ref[...], preferred_element_type=F32)
    o_ref[...] = acc


def _matmul_res(a_list, w, layer, res, *, tm, tn):
    m, n = res.shape
    ka = a_list[0].shape[1]
    assert all(a.shape[1] == ka for a in a_list) and ka * len(a_list) == w.shape[1]
    grid = (m // tm, n // tn)
    in_specs = [pl.BlockSpec((tm, ka), lambda i, j: (i, 0)) for _ in a_list]
    for rb in range(len(a_list)):
        in_specs.append(pl.BlockSpec((None, ka, tn), lambda i, j, rb=rb: (layer, rb, j)))
    in_specs.append(pl.BlockSpec((tm, tn), lambda i, j: (i, j)))
    return pl.pallas_call(
        functools.partial(_matmul_res_kernel, n_in=len(a_list)),
        grid=grid,
        in_specs=in_specs,
        out_specs=pl.BlockSpec((tm, tn), lambda i, j: (i, j)),
        out_shape=jax.ShapeDtypeStruct((m, n), F32),
        compiler_params=_cparams(2),
        name="matmul_res",
    )(*a_list, *([w] * len(a_list)), res)


def _local_cumsum(lg):
    pos = lax.broadcasted_iota(jnp.int32, lg.shape, 0) & (SUB - 1)
    x = lg
    s = 1
    while s < SUB:
        x = x + jnp.where(pos >= s, pltpu.roll(x, s, 0), 0.0)
        s *= 2
    return x


def _chunk_step(q, k, vb, bl, tot, st_ref):
    c = q.shape[0]
    ns = c // SUB
    a = [jnp.zeros_like(tot[0])]
    for m in range(ns):
        a.append(a[-1] + tot[m])
    b_last = a[ns]

    qt, kbar, khat, qin, kout = [], [], [], [], []
    qd = {d: [] for d in range(2, ns)}
    for m in range(ns):
        sl = slice(m * SUB, (m + 1) * SUB)
        blm = bl[sl]
        qtm = q[sl] * jnp.exp(blm)
        km = k[sl]
        khm = km * jnp.exp(tot[m] - blm)
        qt.append(qtm)
        kbar.append(km * jnp.exp(-blm))
        khat.append(khm)
        qin.append(qtm * jnp.exp(a[m]))
        kout.append(khm * jnp.exp(b_last - a[m + 1]))
        for d in range(2, ns):
            if m >= d:
                qd[d].append(qtm * jnp.exp(a[m] - a[m - d + 1]))
            else:
                qd[d].append(jnp.zeros_like(qtm))

    def cat(xs):
        return jnp.concatenate(xs, axis=0).astype(BF16)

    qt_b, kbar_b, khat_b = cat(qt), cat(kbar), cat(khat)
    ri = lax.broadcasted_iota(jnp.int32, (c, c), 0)
    ci = lax.broadcasted_iota(jnp.int32, (c, c), 1)
    dist = (ri >> SUB_SHIFT) - (ci >> SUB_SHIFT)
    s0 = lax.dot_general(qt_b, kbar_b, NT_DIMS, preferred_element_type=F32)
    s1 = lax.dot_general(qt_b, khat_b, NT_DIMS, preferred_element_type=F32)
    scores = jnp.where((dist == 0) & (ri >= ci), s0, jnp.where(dist == 1, s1, 0.0))
    for d in range(2, ns):
        sd = lax.dot_general(cat(qd[d]), khat_b, NT_DIMS, preferred_element_type=F32)
        scores = jnp.where(dist == d, sd, scores)

    st = st_ref[...]
    o = jnp.dot(scores.astype(BF16), vb, preferred_element_type=F32)
    o = o + lax.dot_general(cat(qin), st.astype(BF16), NT_DIMS, preferred_element_type=F32)
    st_ref[...] = st * jnp.exp(b_last) + lax.dot_general(
        vb, cat(kout), TN_DIMS, preferred_element_type=F32)
    return o


def _sub_totals(bl):
    return [bl[(m + 1) * SUB - 1:(m + 1) * SUB, :] for m in range(bl.shape[0] // SUB)]


def _state_prologue(s0_ref, st_ref):
    @pl.when(pl.program_id(1) == 0)
    def _():
        for h in range(st_ref.shape[0]):
            if s0_ref is None:
                st_ref[h] = jnp.zeros(st_ref.shape[1:], F32)
            else:
                st_ref[h] = jnp.transpose(s0_ref[0, h])


def _state_epilogue(sn_ref, st_ref):
    @pl.when(pl.program_id(1) == pl.num_programs(1) - 1)
    def _():
        for h in range(st_ref.shape[0]):
            sn_ref[0, h] = jnp.transpose(st_ref[h])


def _chunk_loop(n_chunks, body, unroll):
    def step(c, carry):
        body(pl.ds(pl.multiple_of(c * CHUNK, CHUNK), CHUNK))
        return carry

    lax.fori_loop(0, n_chunks, step, 0, unroll=min(unroll, n_chunks))


def _head(ref, rows, h, width):
    return ref[rows, h * width:(h + 1) * width]


def _ret_kernel(*refs, has_s0):
    if has_s0:
        q_ref, k_ref, v_ref, g_ref, cos_ref, sin_ref, s0_ref, o_ref, sn_ref, st_ref = refs
    else:
        q_ref, k_ref, v_ref, g_ref, cos_ref, sin_ref, o_ref, sn_ref, st_ref = refs
        s0_ref = None
    _state_prologue(s0_ref, st_ref)
    pos1 = ((lax.broadcasted_iota(jnp.int32, (CHUNK, 1), 0) & (SUB - 1)) + 1).astype(F32)
    half = DK_A // 2

    def body(rows):
        cos, sin = cos_ref[rows, :], sin_ref[rows, :]
        for h in range(H_A):
            lgam = float(np.log1p(-np.exp2(-5.0 - h)))
            bl = pos1 * lgam
            tot = [jnp.full((1, 1), lgam * SUB, F32)] * (CHUNK // SUB)
            qr, kr = _head(q_ref, rows, h, DK_A), _head(k_ref, rows, h, DK_A)
            q = qr * cos + pltpu.roll(qr, half, 1) * sin
            k = (kr * cos + pltpu.roll(kr, half, 1) * sin) * (DK_A ** -0.5)
            o = _chunk_step(q, k, _head(v_ref, rows, h, DV_A).astype(BF16), bl, tot, st_ref.at[h])
            mu = jnp.mean(o, axis=-1, keepdims=True)
            oc = o - mu
            var = jnp.mean(oc * oc, axis=-1, keepdims=True)
            o_ref[rows, h * DV_A:(h + 1) * DV_A] = (
                (oc * lax.rsqrt(var + EPS)) * _silu(_head(g_ref, rows, h, DV_A))).astype(BF16)

    _chunk_loop(q_ref.shape[0] // CHUNK, body, 2)
    _state_epilogue(sn_ref, st_ref)


def _hgrn_kernel(*refs, has_s0, layer):
    if has_s0:
        q_ref, f_ref, i_ref, g_ref, lbp_ref, gn_ref, s0_ref, o_ref, sn_ref, st_ref = refs
    else:
        q_ref, f_ref, i_ref, g_ref, lbp_ref, gn_ref, o_ref, sn_ref, st_ref = refs
        s0_ref = None
    _state_prologue(s0_ref, st_ref)
    lbp = lbp_ref[...]
    e = jnp.exp(lbp - jnp.max(lbp, axis=0, keepdims=True))
    sm = e / jnp.sum(e, axis=0, keepdims=True)
    lb_all = jnp.sum(sm[:layer + 1], axis=0, keepdims=True)
    gn = gn_ref[...]

    def body(rows):
        for h in range(H_B):
            lb = lb_all[:, h * DK_B:(h + 1) * DK_B]
            q = _silu(_head(q_ref, rows, h, DK_B)) * (DK_B ** -0.5)
            f = lb + (1.0 - lb) * _sigmoid(_head(f_ref, rows, h, DK_B))
            bl = _local_cumsum(jnp.log(f))
            o = _chunk_step(q, 1.0 - f, _head(i_ref, rows, h, DV_B).astype(BF16), bl,
                            _sub_totals(bl), st_ref.at[h])
            ms = jnp.mean(o * o, axis=-1, keepdims=True)
            o_ref[rows, h * DV_B:(h + 1) * DV_B] = (
                ((o * lax.rsqrt(ms + EPS)) * gn) * _silu(_head(g_ref, rows, h, DV_B))).astype(BF16)

    _chunk_loop(q_ref.shape[0] // CHUNK, body, 2)
    _state_epilogue(sn_ref, st_ref)


def _gla_kernel(*refs, has_s0):
    if has_s0:
        q_ref, k_ref, v_ref, g_ref, low_ref, wgk_ref, bgk_ref, gn_ref, s0_ref, o_ref, sn_ref, st_ref = refs
    else:
        q_ref, k_ref, v_ref, g_ref, low_ref, wgk_ref, bgk_ref, gn_ref, o_ref, sn_ref, st_ref = refs
        s0_ref = None
    _state_prologue(s0_ref, st_ref)
    gn = gn_ref[...]
    bgk = bgk_ref[...]

    def body(rows):
        x = jnp.dot(low_ref[rows, :].astype(BF16), wgk_ref[...], preferred_element_type=F32) + bgk
        lg_all = (jnp.minimum(x, 0.0) - jnp.log1p(jnp.exp(-jnp.abs(x)))) * (1.0 / GATE_NORMALIZER)
        for h in range(H_C):
            bl = _local_cumsum(lg_all[:, h * DK_C:(h + 1) * DK_C])
            q = _head(q_ref, rows, h, DK_C) * (DK_C ** -0.5)
            o = _chunk_step(q, _head(k_ref, rows, h, DK_C), _head(v_ref, rows, h, DV_C).astype(BF16),
                            bl, _sub_totals(bl), st_ref.at[h])
            ms = jnp.mean(o * o, axis=-1, keepdims=True)
            o_ref[rows, h * DV_C:(h + 1) * DV_C] = (
                ((o * lax.rsqrt(ms + EPS)) * gn) * _silu(_head(g_ref, rows, h, DV_C))).astype(BF16)

    _chunk_loop(q_ref.shape[0] // CHUNK, body, 2)
    _state_epilogue(sn_ref, st_ref)


def _row_spec(tt, width, nt, block):
    return pl.BlockSpec((tt, width), lambda b, t: (b * nt + t, block))


def _recurrence_call(body, in_specs, args, s0, bsz, seq, heads, dk, dv, tt):
    nt = seq // tt
    if s0 is not None:
        in_specs = in_specs + [pl.BlockSpec((1, heads, dk, dv), lambda b, t: (b, 0, 0, 0))]
        args = args + [s0]
    o, sn = pl.pallas_call(
        body,
        grid=(bsz, nt),
        in_specs=in_specs,
        out_specs=[
            pl.BlockSpec((tt, heads * dv), lambda b, t: (b * nt + t, 0)),
            pl.BlockSpec((1, heads, dk, dv), lambda b, t: (b, 0, 0, 0)),
        ],
        out_shape=[
            jax.ShapeDtypeStruct((bsz * seq, heads * dv), BF16),
            jax.ShapeDtypeStruct((bsz, heads, dk, dv), F32),
        ],
        scratch_shapes=[pltpu.VMEM((heads, dv, dk), F32)],
        compiler_params=_cparams(2),
        name="recurrence",
    )(*args)
    return o, sn


def _retention(p, cos2, sin2, s0, bsz, seq, tt):
    nt = seq // tt
    qa, va = H_A * DK_A, H_A * DV_A
    in_specs = [
        _row_spec(tt, qa, nt, 0),
        _row_spec(tt, qa, nt, 1),
        _row_spec(tt, va, nt, 2 * qa // va),
        _row_spec(tt, va, nt, (2 * qa + va) // va),
        pl.BlockSpec((tt, DK_A), lambda b, t: (t, 0)),
        pl.BlockSpec((tt, DK_A), lambda b, t: (t, 0)),
    ]
    args = [p, p, p, p, cos2, sin2]
    body = functools.partial(_ret_kernel, has_s0=s0 is not None)
    return _recurrence_call(body, in_specs, args, s0, bsz, seq, H_A, DK_A, DV_A, tt)


def _hgrn(p, lbp, gnorm, s0, bsz, seq, tt, layer):
    nt = seq // tt
    base = 2 * H_A * DK_A + 2 * H_A * DV_A
    qb = H_B * DK_B
    assert base % qb == 0 and H_B * DV_B == qb
    in_specs = [_row_spec(tt, qb, nt, base // qb + n) for n in range(4)]
    in_specs += [
        pl.BlockSpec(lbp.shape, lambda b, t: (0, 0)),
        pl.BlockSpec((1, DV_B), lambda b, t: (0, 0)),
    ]
    args = [p, p, p, p, lbp, gnorm.reshape(1, DV_B)]
    body = functools.partial(_hgrn_kernel, has_s0=s0 is not None, layer=layer)
    return _recurrence_call(body, in_specs, args, s0, bsz, seq, H_B, DK_B, DV_B, tt)


def _gla(p, low, wgk, layer, bgk, gnorm, s0, bsz, seq, tt):
    nt = seq // tt
    qc, vc = H_C * DK_C, H_C * DV_C
    in_specs = [
        _row_spec(tt, qc, nt, 0),
        _row_spec(tt, qc, nt, 1),
        _row_spec(tt, vc, nt, 2 * qc // vc),
        _row_spec(tt, vc, nt, (2 * qc + vc) // vc),
        _row_spec(tt, low.shape[1], nt, 0),
        pl.BlockSpec((None,) + wgk.shape[1:], lambda b, t: (layer, 0, 0)),
        pl.BlockSpec((1, qc), lambda b, t: (0, 0)),
        pl.BlockSpec((1, DV_C), lambda b, t: (0, 0)),
    ]
    args = [p, p, p, p, low, wgk, bgk.reshape(1, qc), gnorm.reshape(1, DV_C)]
    body = functools.partial(_gla_kernel, has_s0=s0 is not None)
    return _recurrence_call(body, in_specs, args, s0, bsz, seq, H_C, DK_C, DV_C, tt)


def _ffn_kernel(*refs, seq_rows, tiles_per_seq, has_cache, final_norm):
    refs = list(refs)
    h_ref, nw_ref, wa_ref, wu_ref, cw_ref, cb_ref, wd_ref = refs[:7]
    pos = 7
    cache_ref = None
    nf_ref = None
    if has_cache:
        cache_ref = refs[pos]
        pos += 1
    if final_norm:
        nf_ref = refs[pos]
        pos += 1
    out_ref, newc_ref, hn_ref, carry_ref = refs[pos:pos + 4]

    i = pl.program_id(0)
    j = pl.program_id(1)
    tm, tf = h_ref.shape[0], wa_ref.shape[1]

    @pl.when(j == 0)
    def _():
        _rmsnorm_rows_to(h_ref, nw_ref, hn_ref, 128)
        out_ref[...] = h_ref[...]

    if tiles_per_seq > 1:
        @pl.when(i % tiles_per_seq == 0)
        def _():
            carry_ref[j] = jnp.zeros((CONV_W - 1, tf), F32)

    hn = hn_ref[...]
    a = jnp.dot(hn, wa_ref[...], preferred_element_type=F32)
    u = jnp.dot(hn, wu_ref[...], preferred_element_type=F32)

    nseq = tm // seq_rows
    r8 = lax.broadcasted_iota(jnp.int32, (SUBLANES, 1), 0)
    sh1 = pltpu.roll(a, 1, 0)
    sh2 = pltpu.roll(a, 2, 0)
    a1_parts, a2_parts = [], []
    for s in range(nseq):
        if tiles_per_seq > 1:
            p0, p1 = carry_ref[j, 0:1, :], carry_ref[j, 1:2, :]
        elif has_cache:
            p0, p1 = cache_ref[s, 0:1, :], cache_ref[s, 1:2, :]
        else:
            p0 = p1 = jnp.zeros((1, tf), F32)
        lo, hi = s * seq_rows, (s + 1) * seq_rows
        head = slice(lo, lo + SUBLANES)
        a1_parts += [jnp.where(r8 == 0, p1, sh1[head]), sh1[lo + SUBLANES:hi]]
        a2_parts += [jnp.where(r8 == 0, p0, jnp.where(r8 == 1, p1, sh2[head])),
                     sh2[lo + SUBLANES:hi]]
    a1 = jnp.concatenate(a1_parts, axis=0)
    a2 = jnp.concatenate(a2_parts, axis=0)
    cw = cw_ref[...]
    ac = cb_ref[...] + a2 * cw[0:1, :] + a1 * cw[1:2, :] + a * cw[2:3, :]

    if tiles_per_seq > 1:
        carry_ref[j] = a[tm - (CONV_W - 1):, :]
    for s in range(nseq):
        newc_ref[s] = a[(s + 1) * seq_rows - (CONV_W - 1):(s + 1) * seq_rows, :]

    y = (_silu(ac) * u).astype(BF16)
    out_ref[...] += jnp.dot(y, wd_ref[...], preferred_element_type=F32)

    if final_norm:
        @pl.when(j == pl.num_programs(1) - 1)
        def _():
            rows = min(128, tm)

            def body(c, carry):
                r = pl.multiple_of(c * rows, rows)
                x = out_ref[pl.ds(r, rows), :]
                ms = jnp.mean(x * x, axis=-1, keepdims=True)
                out_ref[pl.ds(r, rows), :] = (x * lax.rsqrt(ms + EPS)) * nf_ref[...]
                return carry

            lax.fori_loop(0, tm // rows, body, 0)


def _ffn(h, nw, wa, wu, cw, cb, wd, layer, cache, nf, *, bsz, seq, tm, tf):
    m, d = h.shape
    dffp = wa.shape[2]
    nj = dffp // tf
    if seq >= tm:
        tiles_per_seq = seq // tm
        seq_rows = tm
        seqs_per_tile = 1
    else:
        tiles_per_seq = 1
        seq_rows = seq
        seqs_per_tile = tm // seq
    assert seq_rows & (seq_rows - 1) == 0
    grid = (m // tm, nj)
    in_specs = [
        pl.BlockSpec((tm, d), lambda i, j: (i, 0)),
        pl.BlockSpec((1, d), lambda i, j: (0, 0)),
        pl.BlockSpec((None, d, tf), lambda i, j: (layer, 0, j)),
        pl.BlockSpec((None, d, tf), lambda i, j: (layer, 0, j)),
        pl.BlockSpec((None, CONV_W, tf), lambda i, j: (layer, 0, j)),
        pl.BlockSpec((None, 1, tf), lambda i, j: (layer, 0, j)),
        pl.BlockSpec((None, tf, d), lambda i, j: (layer, j, 0)),
    ]
    args = [h, nw.reshape(1, d), wa, wu, cw, cb, wd]
    if cache is not None:
        assert tiles_per_seq == 1
        in_specs.append(pl.BlockSpec((seqs_per_tile, CONV_W - 1, tf), lambda i, j: (i, 0, j)))
        args.append(cache)
    if nf is not None:
        in_specs.append(pl.BlockSpec((1, d), lambda i, j: (0, 0)))
        args.append(nf.reshape(1, d))
    newc_spec = pl.BlockSpec((seqs_per_tile, CONV_W - 1, tf), lambda i, j: (i, 0, j))
    body = functools.partial(
        _ffn_kernel, seq_rows=seq_rows, tiles_per_seq=tiles_per_seq,
        has_cache=cache is not None, final_norm=nf is not None)
    out, newc = pl.pallas_call(
        body,
        grid=grid,
        in_specs=in_specs,
        out_specs=[pl.BlockSpec((tm, d), lambda i, j: (i, 0)), newc_spec],
        out_shape=[
            jax.ShapeDtypeStruct((m, d), F32),
            jax.ShapeDtypeStruct((grid[0] * seqs_per_tile, CONV_W - 1, dffp), F32),
        ],
        scratch_shapes=[
            pltpu.VMEM((tm, d), BF16),
            pltpu.VMEM((nj, CONV_W - 1, tf), F32),
        ],
        compiler_params=_cparams(2),
        name="ffn",
    )(*args)
    if tiles_per_seq > 1:
        newc = newc.reshape(bsz, tiles_per_seq, CONV_W - 1, dffp)[:, -1]
    return out, newc


def _pad_to(x, axis, size):
    pad = size - x.shape[axis]
    if pad == 0:
        return x
    widths = [(0, 0)] * x.ndim
    widths[axis] = (0, pad)
    return jnp.pad(x, widths)


def _rotary_tables(pos):
    half = DK_A // 2
    inv = ROPE_BASE ** (-jnp.arange(half, dtype=F32) / half)
    ang = pos.astype(F32)[:, None] * inv[None, :]
    cos, sin = jnp.cos(ang), jnp.sin(ang)
    return jnp.concatenate([cos, cos], axis=-1), jnp.concatenate([-sin, sin], axis=-1)


def _prepare_weights(w_in_even, w_out_even, hgrn_lb, w_in_odd, w_gk2, w_out_odd,
                     ffn_w_up, ffn_conv_w, ffn_conv_b, ffn_w_down, tf):
    dff = ffn_w_down.shape[1]
    dffp = -(-dff // tf) * tf
    main_odd = w_in_odd.shape[2] - GK_RANK
    w = {
        "in_even": w_in_even.astype(BF16),
        "out_even": w_out_even.astype(BF16),
        "in_odd": w_in_odd.astype(BF16),
        "low_odd": _pad_to(w_in_odd[:, :, main_odd:], 2, LANES).astype(BF16),
        "gk2": _pad_to(w_gk2, 1, LANES).astype(BF16),
        "out_odd": w_out_odd.astype(BF16),
        "up_a": _pad_to(ffn_w_up[:, :, :dff], 2, dffp).astype(BF16),
        "up_u": _pad_to(ffn_w_up[:, :, dff:], 2, dffp).astype(BF16),
        "conv_w": _pad_to(ffn_conv_w, 2, dffp),
        "conv_b": _pad_to(ffn_conv_b, 1, dffp)[:, None, :],
        "down": _pad_to(ffn_w_down, 1, dffp).astype(BF16),
        "lbp": hgrn_lb,
    }
    return w, dff, dffp


def _tiles(m, seq):
    return min(1024, m), 1024, min(512, m), min(512, seq)


def _run_group(x, pos0, s_ret, s_hgrn, s_gla, s_conv, norm_mix, norm_ffn, norm_final,
               hgrn_gnorm, b_gk2, gla_gnorm, w, dff, dffp, tf):
    bsz, seq, d = x.shape
    m = bsz * seq
    depth = norm_mix.shape[0]
    tm, tn, tm_ffn, tt = _tiles(m, seq)
    cos2, sin2 = _rotary_tables(pos0 + jnp.arange(seq, dtype=jnp.int32))

    h = x.reshape(m, d)
    new_ret, new_hgrn, new_gla, new_conv = [], [], [], []
    for l in range(depth):
        i = l // 2
        if l % 2 == 0:
            p = _norm_matmul(h, norm_mix[l], w["in_even"], i, w["in_even"].shape[2], tm=tm, tn=tn)
            o_a, sr = _retention(p, cos2, sin2, None if s_ret is None else s_ret[i], bsz, seq, tt)
            o_b, sh = _hgrn(p, w["lbp"], hgrn_gnorm[i], None if s_hgrn is None else s_hgrn[i],
                            bsz, seq, tt, l)
            h = _matmul_res([o_a, o_b], w["out_even"], i, h, tm=tm, tn=tn)
            new_ret.append(sr)
            new_hgrn.append(sh)
        else:
            p, low = _norm_matmul(h, norm_mix[l], w["in_odd"], i, w["in_odd"].shape[2] - GK_RANK,
                                  w["low_odd"], tm=tm, tn=tn)
            o_c, sg = _gla(p, low, w["gk2"], i, b_gk2[i], gla_gnorm[i],
                           None if s_gla is None else s_gla[i], bsz, seq, tt)
            h = _matmul_res([o_c], w["out_odd"], i, h, tm=tm, tn=tn)
            new_gla.append(sg)
        cache = None if s_conv is None else _pad_to(s_conv[l], 2, dffp)
        h, nc = _ffn(h, norm_ffn[l], w["up_a"], w["up_u"], w["conv_w"], w["conv_b"], w["down"], l,
                     cache, norm_final if l == depth - 1 else None,
                     bsz=bsz, seq=seq, tm=tm_ffn, tf=tf)
        new_conv.append(nc[:, :, :dff])
    return (h.reshape(bsz, seq, d), jnp.stack(new_ret), jnp.stack(new_hgrn),
            jnp.stack(new_gla), jnp.stack(new_conv))


def kernel(x_prompt, x_sample, state_ret, state_hgrn, state_gla, cache_ffn_conv, norm_mix, norm_ffn, norm_final, w_in_even, w_out_even, hgrn_lb, hgrn_gnorm, w_in_odd, w_gk2, b_gk2, gla_gnorm, w_out_odd, ffn_w_up, ffn_conv_w, ffn_conv_b, ffn_w_down):
    tf = 512
    w, dff, dffp = _prepare_weights(w_in_even, w_out_even, hgrn_lb, w_in_odd, w_gk2, w_out_odd,
                                    ffn_w_up, ffn_conv_w, ffn_conv_b, ffn_w_down, tf)
    common = (norm_mix, norm_ffn, norm_final, hgrn_gnorm, b_gk2, gla_gnorm, w, dff, dffp, tf)
    y_p, ret_p, hgrn_p, gla_p, conv_p = _run_group(x_prompt, 0, None, None, None, None, *common)
    y_s, ret_s, hgrn_s, gla_s, conv_s = _run_group(
        x_sample, PAST_LEN, state_ret, state_hgrn, state_gla, cache_ffn_conv, *common)
    return (y_p, y_s, ret_p, ret_s, hgrn_p, hgrn_s, gla_p, gla_s, conv_p, conv_s)
```

```python
import functools

import numpy as np
import jax
import jax.numpy as jnp
from jax import lax
from jax.experimental import pallas as pl
from jax.experimental.pallas import tpu as pltpu

F32 = jnp.float32
BF16 = jnp.bfloat16

EPS = 1e-6
ROPE_BASE = 10000.0
GATE_NORMALIZER = 16.0
PAST_LEN = 1024

H_A, DK_A, DV_A = 4, 128, 256
H_B, DK_B, DV_B = 8, 128, 128
H_C, DK_C, DV_C = 4, 256, 512
GK_RANK = 16
CONV_W = 3

LANES = 128
SUBLANES = 8
VMEM_LIMIT_BYTES = 52 * 1024 * 1024

CHUNK = 64
SUB = 16
SUB_SHIFT = 4
assert 1 << SUB_SHIFT == SUB and CHUNK % SUB == 0

NT_DIMS = (((1,), (1,)), ((), ()))
TN_DIMS = (((0,), (0,)), ((), ()))


def _cparams(n_axes):
    return pltpu.CompilerParams(
        dimension_semantics=("arbitrary",) * n_axes,
        vmem_limit_bytes=VMEM_LIMIT_BYTES,
    )


def _sigmoid(x):
    return 1.0 / (1.0 + jnp.exp(-x))


def _silu(x):
    return x * _sigmoid(x)


def _rmsnorm_rows_to(x_ref, nw_ref, dst_ref, rows):
    rows = min(rows, x_ref.shape[0])
    assert x_ref.shape[0] % rows == 0
    n = x_ref.shape[0] // rows

    def body(c, carry):
        r = pl.multiple_of(c * rows, rows)
        x = x_ref[pl.ds(r, rows), :]
        ms = jnp.mean(x * x, axis=-1, keepdims=True)
        dst_ref[pl.ds(r, rows), :] = ((x * lax.rsqrt(ms + EPS)) * nw_ref[...]).astype(BF16)
        return carry

    lax.fori_loop(0, n, body, 0)


def _norm_matmul_kernel(x_ref, nw_ref, w_ref, o_ref, hn_ref):
    @pl.when(pl.program_id(1) == 0)
    def _():
        _rmsnorm_rows_to(x_ref, nw_ref, hn_ref, 128)

    o_ref[...] = jnp.dot(hn_ref[...], w_ref[...], preferred_element_type=F32)


def _norm_matmul_low_kernel(x_ref, nw_ref, w_ref, wl_ref, o_ref, low_ref, hn_ref):
    @pl.when(pl.program_id(1) == 0)
    def _():
        _rmsnorm_rows_to(x_ref, nw_ref, hn_ref, 128)
        low_ref[...] = jnp.dot(hn_ref[...], wl_ref[...], preferred_element_type=F32)

    o_ref[...] = jnp.dot(hn_ref[...], w_ref[...], preferred_element_type=F32)


def _norm_matmul(x, nw, w, layer, n, w_low=None, *, tm, tn):
    m, d = x.shape
    assert n % tn == 0 and n <= w.shape[2]
    grid = (m // tm, n // tn)
    in_specs = [
        pl.BlockSpec((tm, d), lambda i, j: (i, 0)),
        pl.BlockSpec((1, d), lambda i, j: (0, 0)),
        pl.BlockSpec((None, d, tn), lambda i, j: (layer, 0, j)),
    ]
    out_shape = [jax.ShapeDtypeStruct((m, n), F32)]
    out_specs = [pl.BlockSpec((tm, tn), lambda i, j: (i, j))]
    args = [x, nw.reshape(1, d), w]
    if w_low is None:
        body = _norm_matmul_kernel
    else:
        body = _norm_matmul_low_kernel
        nl = w_low.shape[2]
        in_specs.append(pl.BlockSpec((None, d, nl), lambda i, j: (layer, 0, 0)))
        out_shape.append(jax.ShapeDtypeStruct((m, nl), F32))
        out_specs.append(pl.BlockSpec((tm, nl), lambda i, j: (i, 0)))
        args.append(w_low)
    outs = pl.pallas_call(
        body,
        grid=grid,
        in_specs=in_specs,
        out_specs=out_specs,
        out_shape=out_shape,
        scratch_shapes=[pltpu.VMEM((tm, d), BF16)],
        compiler_params=_cparams(2),
        name="norm_matmul",
    )(*args)
    return outs if w_low is not None else outs[0]


def _matmul_res_kernel(*refs, n_in):
    a_refs = refs[:n_in]
    w_refs = refs[n_in:2 * n_in]
    res_ref = refs[2 * n_in]
    o_ref = refs[2 * n_in + 1]
    acc = res_ref[...]
    for a_ref, w_ref in zip(a_refs, w_refs):
        acc = acc + jnp.dot(a_ref[...], w_ref[...], preferred_element_type=F32)
    o_ref[...] = acc


def _matmul_res(a_list, w, layer, res, *, tm, tn):
    m, n = res.shape
    ka = a_list[0].shape[1]
    assert all(a.shape[1] == ka for a in a_list) and ka * len(a_list) == w.shape[1]
    grid = (m // tm, n // tn)
    in_specs = [pl.BlockSpec((tm, ka), lambda i, j: (i, 0)) for _ in a_list]
    for rb in range(len(a_list)):
        in_specs.append(pl.BlockSpec((None, ka, tn), lambda i, j, rb=rb: (layer, rb, j)))
    in_specs.append(pl.BlockSpec((tm, tn), lambda i, j: (i, j)))
    return pl.pallas_call(
        functools.partial(_matmul_res_kernel, n_in=len(a_list)),
        grid=grid,
        in_specs=in_specs,
        out_specs=pl.BlockSpec((tm, tn), lambda i, j: (i, j)),
        out_shape=jax.ShapeDtypeStruct((m, n), F32),
        compiler_params=_cparams(2),
        name="matmul_res",
    )(*a_list, *([w] * len(a_list)), res)


def _local_cumsum(lg):
    pos = lax.broadcasted_iota(jnp.int32, lg.shape, 0) & (SUB - 1)
    x = lg
    s = 1
    while s < SUB:
        x = x + jnp.where(pos >= s, pltpu.roll(x, s, 0), 0.0)
        s *= 2
    return x


def _chunk_operands(q, k, bl, tot):
    ns = q.shape[0] // SUB
    a = [jnp.zeros_like(tot[0])]
    for m in range(ns):
        a.append(a[-1] + tot[m])
    b_last = a[ns]

    qt, kbar, khat, qin, kout = [], [], [], [], []
    qd = {d: [] for d in range(2, ns)}
    for m in range(ns):
        sl = slice(m * SUB, (m + 1) * SUB)
        blm = bl[sl]
        qtm = q[sl] * jnp.exp(blm)
        km = k[sl]
        khm = km * jnp.exp(tot[m] - blm)
        qt.append(qtm)
        kbar.append(km * jnp.exp(-blm))
        khat.append(khm)
        qin.append(qtm * jnp.exp(a[m]))
        kout.append(khm * jnp.exp(b_last - a[m + 1]))
        for d in range(2, ns):
            if m >= d:
                qd[d].append(qtm * jnp.exp(a[m] - a[m - d + 1]))
            else:
                qd[d].append(jnp.zeros_like(qtm))

    def cat(xs):
        return jnp.concatenate(xs, axis=0).astype(BF16)

    return dict(qt=cat(qt), kbar=cat(kbar), khat=cat(khat), qin=cat(qin), kout=cat(kout),
                qd=[cat(qd[d]) for d in range(2, ns)], decay=jnp.exp(b_last))


def _chunk_steps(heads, st_ref):
    c = heads[0][0].shape[0]
    ns = c // SUB
    ops = [_chunk_operands(q, k, bl, tot) for (q, k, _, bl, tot) in heads]
    raw = []
    for op in ops:
        s = [lax.dot_general(op["qt"], op["kbar"], NT_DIMS, preferred_element_type=F32),
             lax.dot_general(op["qt"], op["khat"], NT_DIMS, preferred_element_type=F32)]
        s += [lax.dot_general(qd, op["khat"], NT_DIMS, preferred_element_type=F32) for qd in op["qd"]]
        raw.append(s)
    ri = lax.broadcasted_iota(jnp.int32, (c, c), 0)
    ci = lax.broadcasted_iota(jnp.int32, (c, c), 1)
    dist = (ri >> SUB_SHIFT) - (ci >> SUB_SHIFT)
    outs = []
    for h, (op, s) in enumerate(zip(ops, raw)):
        scores = jnp.where((dist == 0) & (ri >= ci), s[0], jnp.where(dist == 1, s[1], 0.0))
        for d in range(2, ns):
            scores = jnp.where(dist == d, s[d], scores)
        vb = heads[h][2]
        o = jnp.dot(scores.astype(BF16), vb, preferred_element_type=F32)
        o = o + lax.dot_general(op["qin"], st_ref[h].astype(BF16), NT_DIMS,
                                preferred_element_type=F32)
        outs.append(o)
    for h, op in enumerate(ops):
        st_ref[h] = st_ref[h] * op["decay"] + lax.dot_general(
            heads[h][2], op["kout"], TN_DIMS, preferred_element_type=F32)
    return outs


def _sub_totals(bl):
    return [bl[(m + 1) * SUB - 1:(m + 1) * SUB, :] for m in range(bl.shape[0] // SUB)]


def _state_prologue(s0_ref, st_ref):
    @pl.when(pl.program_id(1) == 0)
    def _():
        for h in range(st_ref.shape[0]):
            if s0_ref is None:
                st_ref[h] = jnp.zeros(st_ref.shape[1:], F32)
            else:
                st_ref[h] = jnp.transpose(s0_ref[0, h])


def _state_epilogue(sn_ref, st_ref):
    @pl.when(pl.program_id(1) == pl.num_programs(1) - 1)
    def _():
        for h in range(st_ref.shape[0]):
            sn_ref[0, h] = jnp.transpose(st_ref[h])


def _chunk_loop(n_chunks, body, unroll):
    def step(c, carry):
        body(pl.ds(pl.multiple_of(c * CHUNK, CHUNK), CHUNK))
        return carry

    lax.fori_loop(0, n_chunks, step, 0, unroll=min(unroll, n_chunks))


def _head(ref, rows, h, width):
    return ref[rows, h * width:(h + 1) * width]


def _ret_chunk(get, put, cos, sin, st_ref):
    pos1 = ((lax.broadcasted_iota(jnp.int32, (CHUNK, 1), 0) & (SUB - 1)) + 1).astype(F32)
    half = DK_A // 2
    heads = []
    for h in range(H_A):
        lgam = float(np.log1p(-np.exp2(-5.0 - h)))
        bl = pos1 * lgam
        tot = [jnp.full((1, 1), lgam * SUB, F32)] * (CHUNK // SUB)
        qr, kr = get("q", h), get("k", h)
        q = qr * cos + pltpu.roll(qr, half, 1) * sin
        k = (kr * cos + pltpu.roll(kr, half, 1) * sin) * (DK_A ** -0.5)
        heads.append((q, k, get("v", h).astype(BF16), bl, tot))
    for h, o in enumerate(_chunk_steps(heads, st_ref)):
        mu = jnp.mean(o, axis=-1, keepdims=True)
        oc = o - mu
        var = jnp.mean(oc * oc, axis=-1, keepdims=True)
        put(h, ((oc * lax.rsqrt(var + EPS)) * _silu(get("g", h))).astype(BF16))


def _hgrn_lower_bound(lbp_ref, layer):
    lbp = lbp_ref[...]
    e = jnp.exp(lbp - jnp.max(lbp, axis=0, keepdims=True))
    sm = e / jnp.sum(e, axis=0, keepdims=True)
    return jnp.sum(sm[:layer + 1], axis=0, keepdims=True)


def _hgrn_chunk(get, put, lb_all, gn, st_ref):
    heads = []
    for h in range(H_B):
        lb = lb_all[:, h * DK_B:(h + 1) * DK_B]
        q = _silu(get("q", h)) * (DK_B ** -0.5)
        f = lb + (1.0 - lb) * _sigmoid(get("f", h))
        bl = _local_cumsum(jnp.log(f))
        heads.append((q, 1.0 - f, get("i", h).astype(BF16), bl, _sub_totals(bl)))
    for h, o in enumerate(_chunk_steps(heads, st_ref)):
        ms = jnp.mean(o * o, axis=-1, keepdims=True)
        put(h, (((o * lax.rsqrt(ms + EPS)) * gn) * _silu(get("g", h))).astype(BF16))


def _gla_chunk(get, put, low, wgk, bgk, gn, st_ref):
    x = jnp.dot(low.astype(BF16), wgk, preferred_element_type=F32) + bgk
    lg_all = (jnp.minimum(x, 0.0) - jnp.log1p(jnp.exp(-jnp.abs(x)))) * (1.0 / GATE_NORMALIZER)
    heads = []
    for h in range(H_C):
        bl = _local_cumsum(lg_all[:, h * DK_C:(h + 1) * DK_C])
        q = get("q", h) * (DK_C ** -0.5)
        heads.append((q, get("k", h), get("v", h).astype(BF16), bl, _sub_totals(bl)))
    for h, o in enumerate(_chunk_steps(heads, st_ref)):
        ms = jnp.mean(o * o, axis=-1, keepdims=True)
        put(h, (((o * lax.rsqrt(ms + EPS)) * gn) * _silu(get("g", h))).astype(BF16))


def _ref_getter(refs, widths, rows):
    return lambda name, h: _head(refs[name], rows, h, widths[name])


def _ref_putter(o_ref, width, rows):
    def put(h, x):
        o_ref[rows, h * width:(h + 1) * width] = x
    return put


def _ret_kernel(*refs, has_s0):
    if has_s0:
        q_ref, k_ref, v_ref, g_ref, cos_ref, sin_ref, s0_ref, o_ref, sn_ref, st_ref = refs
    else:
        q_ref, k_ref, v_ref, g_ref, cos_ref, sin_ref, o_ref, sn_ref, st_ref = refs
        s0_ref = None
    _state_prologue(s0_ref, st_ref)
    srcs = {"q": q_ref, "k": k_ref, "v": v_ref, "g": g_ref}
    widths = {"q": DK_A, "k": DK_A, "v": DV_A, "g": DV_A}

    def body(rows):
        _ret_chunk(_ref_getter(srcs, widths, rows), _ref_putter(o_ref, DV_A, rows),
                   cos_ref[rows, :], sin_ref[rows, :], st_ref)

    _chunk_loop(q_ref.shape[0] // CHUNK, body, 2)
    _state_epilogue(sn_ref, st_ref)


def _hgrn_kernel(*refs, has_s0, layer):
    if has_s0:
        q_ref, f_ref, i_ref, g_ref, lbp_ref, gn_ref, s0_ref, o_ref, sn_ref, st_ref = refs
    else:
        q_ref, f_ref, i_ref, g_ref, lbp_ref, gn_ref, o_ref, sn_ref, st_ref = refs
        s0_ref = None
    _state_prologue(s0_ref, st_ref)
    lb_all = _hgrn_lower_bound(lbp_ref, layer)
    gn = gn_ref[...]
    srcs = {"q": q_ref, "f": f_ref, "i": i_ref, "g": g_ref}
    widths = {"q": DK_B, "f": DK_B, "i": DV_B, "g": DV_B}

    def body(rows):
        _hgrn_chunk(_ref_getter(srcs, widths, rows), _ref_putter(o_ref, DV_B, rows),
                    lb_all, gn, st_ref)

    _chunk_loop(q_ref.shape[0] // CHUNK, body, 2)
    _state_epilogue(sn_ref, st_ref)


def _gla_kernel(*refs, has_s0):
    if has_s0:
        q_ref, k_ref, v_ref, g_ref, low_ref, wgk_ref, bgk_ref, gn_ref, s0_ref, o_ref, sn_ref, st_ref = refs
    else:
        q_ref, k_ref, v_ref, g_ref, low_ref, wgk_ref, bgk_ref, gn_ref, o_ref, sn_ref, st_ref = refs
        s0_ref = None
    _state_prologue(s0_ref, st_ref)
    gn = gn_ref[...]
    bgk = bgk_ref[...]
    srcs = {"q": q_ref, "k": k_ref, "v": v_ref, "g": g_ref}
    widths = {"q": DK_C, "k": DK_C, "v": DV_C, "g": DV_C}

    def body(rows):
        _gla_chunk(_ref_getter(srcs, widths, rows), _ref_putter(o_ref, DV_C, rows),
                   low_ref[rows, :], wgk_ref[...], bgk, gn, st_ref)

    _chunk_loop(q_ref.shape[0] // CHUNK, body, 2)
    _state_epilogue(sn_ref, st_ref)


def _row_spec(tt, width, nt, block):
    return pl.BlockSpec((tt, width), lambda b, t: (b * nt + t, block))


def _recurrence_call(body, in_specs, args, s0, bsz, seq, heads, dk, dv, tt):
    nt = seq // tt
    if s0 is not None:
        in_specs = in_specs + [pl.BlockSpec((1, heads, dk, dv), lambda b, t: (b, 0, 0, 0))]
        args = args + [s0]
    o, sn = pl.pallas_call(
        body,
        grid=(bsz, nt),
        in_specs=in_specs,
        out_specs=[
            pl.BlockSpec((tt, heads * dv), lambda b, t: (b * nt + t, 0)),
            pl.BlockSpec((1, heads, dk, dv), lambda b, t: (b, 0, 0, 0)),
        ],
        out_shape=[
            jax.ShapeDtypeStruct((bsz * seq, heads * dv), BF16),
            jax.ShapeDtypeStruct((bsz, heads, dk, dv), F32),
        ],
        scratch_shapes=[pltpu.VMEM((heads, dv, dk), F32)],
        compiler_params=_cparams(2),
        name="recurrence",
    )(*args)
    return o, sn


def _retention(p, cos2, sin2, s0, bsz, seq, tt):
    nt = seq // tt
    qa, va = H_A * DK_A, H_A * DV_A
    in_specs = [
        _row_spec(tt, qa, nt, 0),
        _row_spec(tt, qa, nt, 1),
        _row_spec(tt, va, nt, 2 * qa // va),
        _row_spec(tt, va, nt, (2 * qa + va) // va),
        pl.BlockSpec((tt, DK_A), lambda b, t: (t, 0)),
        pl.BlockSpec((tt, DK_A), lambda b, t: (t, 0)),
    ]
    args = [p, p, p, p, cos2, sin2]
    body = functools.partial(_ret_kernel, has_s0=s0 is not None)
    return _recurrence_call(body, in_specs, args, s0, bsz, seq, H_A, DK_A, DV_A, tt)


def _hgrn(p, lbp, gnorm, s0, bsz, seq, tt, layer):
    nt = seq // tt
    base = 2 * H_A * DK_A + 2 * H_A * DV_A
    qb = H_B * DK_B
    assert base % qb == 0 and H_B * DV_B == qb
    in_specs = [_row_spec(tt, qb, nt, base // qb + n) for n in range(4)]
    in_specs += [
        pl.BlockSpec(lbp.shape, lambda b, t: (0, 0)),
        pl.BlockSpec((1, DV_B), lambda b, t: (0, 0)),
    ]
    args = [p, p, p, p, lbp, gnorm.reshape(1, DV_B)]
    body = functools.partial(_hgrn_kernel, has_s0=s0 is not None, layer=layer)
    return _recurrence_call(body, in_specs, args, s0, bsz, seq, H_B, DK_B, DV_B, tt)


def _gla(p, low, wgk, layer, bgk, gnorm, s0, bsz, seq, tt):
    nt = seq // tt
    qc, vc = H_C * DK_C, H_C * DV_C
    in_specs = [
        _row_spec(tt, qc, nt, 0),
        _row_spec(tt, qc, nt, 1),
        _row_spec(tt, vc, nt, 2 * qc // vc),
        _row_spec(tt, vc, nt, (2 * qc + vc) // vc),
        _row_spec(tt, low.shape[1], nt, 0),
        pl.BlockSpec((None,) + wgk.shape[1:], lambda b, t: (layer, 0, 0)),
        pl.BlockSpec((1, qc), lambda b, t: (0, 0)),
        pl.BlockSpec((1, DV_C), lambda b, t: (0, 0)),
    ]
    args = [p, p, p, p, low, wgk, bgk.reshape(1, qc), gnorm.reshape(1, DV_C)]
    body = functools.partial(_gla_kernel, has_s0=s0 is not None)
    return _recurrence_call(body, in_specs, args, s0, bsz, seq, H_C, DK_C, DV_C, tt)


def _ffn_kernel(*refs, seq_rows, tiles_per_seq, has_cache, final_norm):
    refs = list(refs)
    h_ref, nw_ref, wa_ref, wu_ref, cw_ref, cb_ref, wd_ref = refs[:7]
    pos = 7
    cache_ref = None
    nf_ref = None
    if has_cache:
        cache_ref = refs[pos]
        pos += 1
    if final_norm:
        nf_ref = refs[pos]
        pos += 1
    out_ref, newc_ref, hn_ref, carry_ref = refs[pos:pos + 4]

    i = pl.program_id(0)
    j = pl.program_id(1)
    tm, tf = h_ref.shape[0], wa_ref.shape[1]

    @pl.when(j == 0)
    def _():
        _rmsnorm_rows_to(h_ref, nw_ref, hn_ref, 128)
        out_ref[...] = h_ref[...]

    if tiles_per_seq > 1:
        @pl.when(i % tiles_per_seq == 0)
        def _():
            carry_ref[j] = jnp.zeros((CONV_W - 1, tf), F32)

    hn = hn_ref[...]
    a = jnp.dot(hn, wa_ref[...], preferred_element_type=F32)
    u = jnp.dot(hn, wu_ref[...], preferred_element_type=F32)

    nseq = tm // seq_rows
    r8 = lax.broadcasted_iota(jnp.int32, (SUBLANES, 1), 0)
    sh1 = pltpu.roll(a, 1, 0)
    sh2 = pltpu.roll(a, 2, 0)
    a1_parts, a2_parts = [], []
    for s in range(nseq):
        if tiles_per_seq > 1:
            p0, p1 = carry_ref[j, 0:1, :], carry_ref[j, 1:2, :]
        elif has_cache:
            p0, p1 = cache_ref[s, 0:1, :], cache_ref[s, 1:2, :]
        else:
            p0 = p1 = jnp.zeros((1, tf), F32)
        lo, hi = s * seq_rows, (s + 1) * seq_rows
        head = slice(lo, lo + SUBLANES)
        a1_parts += [jnp.where(r8 == 0, p1, sh1[head]), sh1[lo + SUBLANES:hi]]
        a2_parts += [jnp.where(r8 == 0, p0, jnp.where(r8 == 1, p1, sh2[head])),
                     sh2[lo + SUBLANES:hi]]
    a1 = jnp.concatenate(a1_parts, axis=0)
    a2 = jnp.concatenate(a2_parts, axis=0)
    cw = cw_ref[...]
    ac = cb_ref[...] + a2 * cw[0:1, :] + a1 * cw[1:2, :] + a * cw[2:3, :]

    if tiles_per_seq > 1:
        carry_ref[j] = a[tm - (CONV_W - 1):, :]
    for s in range(nseq):
        newc_ref[s] = a[(s + 1) * seq_rows - (CONV_W - 1):(s + 1) * seq_rows, :]

    y = (_silu(ac) * u).astype(BF16)
    out_ref[...] += jnp.dot(y, wd_ref[...], preferred_element_type=F32)

    if final_norm:
        @pl.when(j == pl.num_programs(1) - 1)
        def _():
            rows = min(128, tm)

            def body(c, carry):
                r = pl.multiple_of(c * rows, rows)
                x = out_ref[pl.ds(r, rows), :]
                ms = jnp.mean(x * x, axis=-1, keepdims=True)
                out_ref[pl.ds(r, rows), :] = (x * lax.rsqrt(ms + EPS)) * nf_ref[...]
                return carry

            lax.fori_loop(0, tm // rows, body, 0)


def _ffn(h, nw, wa, wu, cw, cb, wd, layer, cache, nf, *, bsz, seq, tm, tf):
    m, d = h.shape
    dffp = wa.shape[2]
    nj = dffp // tf
    if seq >= tm:
        tiles_per_seq = seq // tm
        seq_rows = tm
        seqs_per_tile = 1
    else:
        tiles_per_seq = 1
        seq_rows = seq
        seqs_per_tile = tm // seq
    assert seq_rows & (seq_rows - 1) == 0
    grid = (m // tm, nj)
    in_specs = [
        pl.BlockSpec((tm, d), lambda i, j: (i, 0)),
        pl.BlockSpec((1, d), lambda i, j: (0, 0)),
        pl.BlockSpec((None, d, tf), lambda i, j: (layer, 0, j)),
        pl.BlockSpec((None, d, tf), lambda i, j: (layer, 0, j)),
        pl.BlockSpec((None, CONV_W, tf), lambda i, j: (layer, 0, j)),
        pl.BlockSpec((None, 1, tf), lambda i, j: (layer, 0, j)),
        pl.BlockSpec((None, tf, d), lambda i, j: (layer, j, 0)),
    ]
    args = [h, nw.reshape(1, d), wa, wu, cw, cb, wd]
    if cache is not None:
        assert tiles_per_seq == 1
        in_specs.append(pl.BlockSpec((seqs_per_tile, CONV_W - 1, tf), lambda i, j: (i, 0, j)))
        args.append(cache)
    if nf is not None:
        in_specs.append(pl.BlockSpec((1, d), lambda i, j: (0, 0)))
        args.append(nf.reshape(1, d))
    newc_spec = pl.BlockSpec((seqs_per_tile, CONV_W - 1, tf), lambda i, j: (i, 0, j))
    body = functools.partial(
        _ffn_kernel, seq_rows=seq_rows, tiles_per_seq=tiles_per_seq,
        has_cache=cache is not None, final_norm=nf is not None)
    out, newc = pl.pallas_call(
        body,
        grid=grid,
        in_specs=in_specs,
        out_specs=[pl.BlockSpec((tm, d), lambda i, j: (i, 0)), newc_spec],
        out_shape=[
            jax.ShapeDtypeStruct((m, d), F32),
            jax.ShapeDtypeStruct((grid[0] * seqs_per_tile, CONV_W - 1, dffp), F32),
        ],
        scratch_shapes=[
            pltpu.VMEM((tm, d), BF16),
            pltpu.VMEM((nj, CONV_W - 1, tf), F32),
        ],
        compiler_params=_cparams(2),
        name="ffn",
    )(*args)
    if tiles_per_seq > 1:
        newc = newc.reshape(bsz, tiles_per_seq, CONV_W - 1, dffp)[:, -1]
    return out, newc


def _pad_to(x, axis, size):
    pad = size - x.shape[axis]
    if pad == 0:
        return x
    widths = [(0, 0)] * x.ndim
    widths[axis] = (0, pad)
    return jnp.pad(x, widths)


def _rotary_tables(pos):
    half = DK_A // 2
    inv = ROPE_BASE ** (-jnp.arange(half, dtype=F32) / half)
    ang = pos.astype(F32)[:, None] * inv[None, :]
    cos, sin = jnp.cos(ang), jnp.sin(ang)
    return jnp.concatenate([cos, cos], axis=-1), jnp.concatenate([-sin, sin], axis=-1)


def _prepare_weights(w_in_even, w_out_even, hgrn_lb, w_in_odd, w_gk2, w_out_odd,
                     ffn_w_up, ffn_conv_w, ffn_conv_b, ffn_w_down, tf):
    dff = ffn_w_down.shape[1]
    dffp = -(-dff // tf) * tf
    main_odd = w_in_odd.shape[2] - GK_RANK
    w = {
        "in_even": w_in_even.astype(BF16),
        "out_even": w_out_even.astype(BF16),
        "in_odd": w_in_odd.astype(BF16),
        "low_odd": _pad_to(w_in_odd[:, :, main_odd:], 2, LANES).astype(BF16),
        "gk2": _pad_to(w_gk2, 1, LANES).astype(BF16),
        "out_odd": w_out_odd.astype(BF16),
        "up_a": _pad_to(ffn_w_up[:, :, :dff], 2, dffp).astype(BF16),
        "up_u": _pad_to(ffn_w_up[:, :, dff:], 2, dffp).astype(BF16),
        "conv_w": _pad_to(ffn_conv_w, 2, dffp),
        "conv_b": _pad_to(ffn_conv_b, 1, dffp)[:, None, :],
        "down": _pad_to(ffn_w_down, 1, dffp).astype(BF16),
        "lbp": hgrn_lb,
    }
    return w, dff, dffp


def _tiles(m, seq):
    return min(1024, m), 1024, min(512, m), min(512, seq)


def _run_group(x, pos0, s_ret, s_hgrn, s_gla, s_conv, norm_mix, norm_ffn, norm_final,
               hgrn_gnorm, b_gk2, gla_gnorm, w, dff, dffp, tf):
    bsz, seq, d = x.shape
    m = bsz * seq
    depth = norm_mix.shape[0]
    tm, tn, tm_ffn, tt = _tiles(m, seq)
    cos2, sin2 = _rotary_tables(pos0 + jnp.arange(seq, dtype=jnp.int32))

    h = x.reshape(m, d)
    new_ret, new_hgrn, new_gla, new_conv = [], [], [], []
    for l in range(depth):
        i = l // 2
        if l % 2 == 0:
            p = _norm_matmul(h, norm_mix[l], w["in_even"], i, w["in_even"].shape[2], tm=tm, tn=tn)
            o_a, sr = _retention(p, cos2, sin2, None if s_ret is None else s_ret[i], bsz, seq, tt)
            o_b, sh = _hgrn(p, w["lbp"], hgrn_gnorm[i], None if s_hgrn is None else s_hgrn[i],
                            bsz, seq, tt, l)
            h = _matmul_res([o_a, o_b], w["out_even"], i, h, tm=tm, tn=tn)
            new_ret.append(sr)
            new_hgrn.append(sh)
        else:
            p, low = _norm_matmul(h, norm_mix[l], w["in_odd"], i, w["in_odd"].shape[2] - GK_RANK,
                                  w["low_odd"], tm=tm, tn=tn)
            o_c, sg = _gla(p, low, w["gk2"], i, b_gk2[i], gla_gnorm[i],
                           None if s_gla is None else s_gla[i], bsz, seq, tt)
            h = _matmul_res([o_c], w["out_odd"], i, h, tm=tm, tn=tn)
            new_gla.append(sg)
        cache = None if s_conv is None else _pad_to(s_conv[l], 2, dffp)
        h, nc = _ffn(h, norm_ffn[l], w["up_a"], w["up_u"], w["conv_w"], w["conv_b"], w["down"], l,
                     cache, norm_final if l == depth - 1 else None,
                     bsz=bsz, seq=seq, tm=tm_ffn, tf=tf)
        new_conv.append(nc[:, :, :dff])
    return (h.reshape(bsz, seq, d), jnp.stack(new_ret), jnp.stack(new_hgrn),
            jnp.stack(new_gla), jnp.stack(new_conv))


def kernel(x_prompt, x_sample, state_ret, state_hgrn, state_gla, cache_ffn_conv, norm_mix, norm_ffn, norm_final, w_in_even, w_out_even, hgrn_lb, hgrn_gnorm, w_in_odd, w_gk2, b_gk2, gla_gnorm, w_out_odd, ffn_w_up, ffn_conv_w, ffn_conv_b, ffn_w_down):
    tf = 512
    w, dff, dffp = _prepare_weights(w_in_even, w_out_even, hgrn_lb, w_in_odd, w_gk2, w_out_odd,
                                    ffn_w_up, ffn_conv_w, ffn_conv_b, ffn_w_down, tf)
    common = (norm_mix, norm_ffn, norm_final, hgrn_gnorm, b_gk2, gla_gnorm, w, dff, dffp, tf)
    y_p, ret_p, hgrn_p, gla_p, conv_p = _run_group(x_prompt, 0, None, None, None, None, *common)
    y_s, ret_s, hgrn_s, gla_s, conv_s = _run_group(
        x_sample, PAST_LEN, state_ret, state_hgrn, state_gla, cache_ffn_conv, *common)
    return (y_p, y_s, ret_p, ret_s, hgrn_p, hgrn_s, gla_p, gla_s, conv_p, conv_s)
```

```python
import functools

import numpy as np
import jax
import jax.numpy as jnp
from jax import lax
from jax.experimental import pallas as pl
from jax.experimental.pallas import tpu as pltpu

F32 = jnp.float32
BF16 = jnp.bfloat16

EPS = 1e-6
ROPE_BASE = 10000.0
GATE_NORMALIZER = 16.0
PAST_LEN = 1024

H_A, DK_A, DV_A = 4, 128, 256
H_B, DK_B, DV_B = 8, 128, 128
H_C, DK_C, DV_C = 4, 256, 512
GK_RANK = 16
CONV_W = 3

LANES = 128
SUBLANES = 8
VMEM_LIMIT_BYTES = 56 * 1024 * 1024

CHUNK = 64
SUB = 16
SUB_SHIFT = 4
assert 1 << SUB_SHIFT == SUB and CHUNK % SUB == 0

NT_DIMS = (((1,), (1,)), ((), ()))
TN_DIMS = (((0,), (0,)), ((), ()))


def _cparams(n_axes):
    return pltpu.CompilerParams(
        dimension_semantics=("arbitrary",) * n_axes,
        vmem_limit_bytes=VMEM_LIMIT_BYTES,
    )


def _sigmoid(x):
    return 1.0 / (1.0 + jnp.exp(-x))


def _silu(x):
    return x * _sigmoid(x)


def _rmsnorm_rows_to(x_ref, nw_ref, dst_ref, rows):
    rows = min(rows, x_ref.shape[0])
    assert x_ref.shape[0] % rows == 0
    n = x_ref.shape[0] // rows

    def body(c, carry):
        r = pl.multiple_of(c * rows, rows)
        x = x_ref[pl.ds(r, rows), :]
        ms = jnp.mean(x * x, axis=-1, keepdims=True)
        dst_ref[pl.ds(r, rows), :] = ((x * lax.rsqrt(ms + EPS)) * nw_ref[...]).astype(BF16)
        return carry

    lax.fori_loop(0, n, body, 0)


def _cast_job(src, lead, rows, cin, rb, *, col=0, cout=None, rows_out=None):
    cout = cin if cout is None else cout
    rows_out = rows if rows_out is None else rows_out
    assert rows % rb == 0 and rows_out % rb == 0
    return dict(src=src, lead=tuple(lead), rb=rb, col=col, cin=cin, cout=cout,
                n_src=rows // rb, n_dst=rows_out // rb)


def _cast_job_io(job, grid):
    assert job["n_dst"] <= grid[0] * grid[1]
    lead = job["lead"]

    def src_index(i, j):
        return lead + (jnp.minimum(i * grid[1] + j, job["n_src"] - 1), job["col"])

    def dst_index(i, j):
        return (jnp.minimum(i * grid[1] + j, job["n_dst"] - 1), 0)

    return (pl.BlockSpec((None,) * len(lead) + (job["rb"], job["cin"]), src_index),
            pl.BlockSpec((job["rb"], job["cout"]), dst_index),
            jax.ShapeDtypeStruct((job["n_dst"] * job["rb"], job["cout"]), BF16))


def _run_cast_jobs(jobs, src_refs, dst_refs):
    step = pl.program_id(0) * pl.num_programs(1) + pl.program_id(1)
    for job, src_ref, dst_ref in zip(jobs, src_refs, dst_refs):
        y = src_ref[...].astype(BF16)
        if job["n_dst"] > job["n_src"]:
            y = jnp.where(step < job["n_src"], y, jnp.zeros_like(y))
        if job["cout"] > job["cin"]:
            dst_ref[:, :job["cin"]] = y
            dst_ref[:, job["cin"]:] = jnp.zeros((job["rb"], job["cout"] - job["cin"]), BF16)
        else:
            dst_ref[...] = y


def _norm_matmul_kernel(*refs, has_low, jobs):
    n_in = 3 + has_low
    n_out = 1 + has_low
    nj = len(jobs)
    x_ref, nw_ref, w_ref = refs[:3]
    src_refs = refs[n_in:n_in + nj]
    o_ref = refs[n_in + nj]
    dst_refs = refs[n_in + nj + n_out:n_in + nj + n_out + nj]
    hn_ref = refs[-1]

    @pl.when(pl.program_id(1) == 0)
    def _():
        _rmsnorm_rows_to(x_ref, nw_ref, hn_ref, 128)
        if has_low:
            low_ref = refs[n_in + nj + 1]
            low_ref[...] = jnp.dot(hn_ref[...], refs[3][...], preferred_element_type=F32)

    o_ref[...] = jnp.dot(hn_ref[...], w_ref[...], preferred_element_type=F32)
    _run_cast_jobs(jobs, src_refs, dst_refs)


def _norm_matmul(x, nw, w, n, w_low=None, jobs=(), *, tm, tn):
    m, d = x.shape
    assert n % tn == 0 and n <= w.shape[1]
    grid = (m // tm, n // tn)
    jobs = list(jobs)
    job_io = [_cast_job_io(job, grid) for job in jobs]
    in_specs = [
        pl.BlockSpec((tm, d), lambda i, j: (i, 0)),
        pl.BlockSpec((1, d), lambda i, j: (0, 0)),
        pl.BlockSpec((d, tn), lambda i, j: (0, j)),
    ]
    out_shape = [jax.ShapeDtypeStruct((m, n), F32)]
    out_specs = [pl.BlockSpec((tm, tn), lambda i, j: (i, j))]
    args = [x, nw.reshape(1, d), w]
    if w_low is not None:
        nl = w_low.shape[1]
        in_specs.append(pl.BlockSpec((d, nl), lambda i, j: (0, 0)))
        out_shape.append(jax.ShapeDtypeStruct((m, nl), F32))
        out_specs.append(pl.BlockSpec((tm, nl), lambda i, j: (i, 0)))
        args.append(w_low)
    in_specs += [io[0] for io in job_io]
    out_specs += [io[1] for io in job_io]
    out_shape += [io[2] for io in job_io]
    args += [job["src"] for job in jobs]
    outs = pl.pallas_call(
        functools.partial(_norm_matmul_kernel, has_low=w_low is not None, jobs=jobs),
        grid=grid,
        in_specs=in_specs,
        out_specs=out_specs,
        out_shape=out_shape,
        scratch_shapes=[pltpu.VMEM((tm, d), BF16)],
        compiler_params=_cparams(2),
        name="norm_matmul",
    )(*args)
    n_main = 1 + (w_low is not None)
    return outs[0], (outs[1] if w_low is not None else None), list(outs[n_main:])


def _matmul_res_kernel(*refs, n_in):
    a_refs = refs[:n_in]
    w_refs = refs[n_in:2 * n_in]
    res_ref = refs[2 * n_in]
    o_ref = refs[2 * n_in + 1]
    acc = res_ref[...]
    for a_ref, w_ref in zip(a_refs, w_refs):
        acc = acc + jnp.dot(a_ref[...], w_ref[...], preferred_element_type=F32)
    o_ref[...] = acc


def _matmul_res(a_list, w, res, *, tm, tn):
    m, n = res.shape
    ka = a_list[0].shape[1]
    assert all(a.shape[1] == ka for a in a_list) and ka * len(a_list) == w.shape[0]
    grid = (m // tm, n // tn)
    in_specs = [pl.BlockSpec((tm, ka), lambda i, j: (i, 0)) for _ in a_list]
    for rb in range(len(a_list)):
        in_specs.append(pl.BlockSpec((ka, tn), lambda i, j, rb=rb: (rb, j)))
    in_specs.append(pl.BlockSpec((tm, tn), lambda i, j: (i, j)))
    return pl.pallas_call(
        functools.partial(_matmul_res_kernel, n_in=len(a_list)),
        grid=grid,
        in_specs=in_specs,
        out_specs=pl.BlockSpec((tm, tn), lambda i, j: (i, j)),
        out_shape=jax.ShapeDtypeStruct((m, n), F32),
        compiler_params=_cparams(2),
        name="matmul_res",
    )(*a_list, *([w] * len(a_list)), res)


def _local_cumsum(lg):
    pos = lax.broadcasted_iota(jnp.int32, lg.shape, 0) & (SUB - 1)
    x = lg
    s = 1
    while s < SUB:
        x = x + jnp.where(pos >= s, pltpu.roll(x, s, 0), 0.0)
        s *= 2
    return x


def _chunk_operands(q, k, bl, tot):
    ns = q.shape[0] // SUB
    a = [jnp.zeros_like(tot[0])]
    for m in range(ns):
        a.append(a[-1] + tot[m])
    b_last = a[ns]

    qt, kbar, khat, qin, kout = [], [], [], [], []
    qd = {d: [] for d in range(2, ns)}
    for m in range(ns):
        sl = slice(m * SUB, (m + 1) * SUB)
        blm = bl[sl]
        qtm = q[sl] * jnp.exp(blm)
        km = k[sl]
        khm = km * jnp.exp(tot[m] - blm)
        qt.append(qtm)
        kbar.append(km * jnp.exp(-blm))
        khat.append(khm)
        qin.append(qtm * jnp.exp(a[m]))
        kout.append(khm * jnp.exp(b_last - a[m + 1]))
        for d in range(2, ns):
            if m >= d:
                qd[d].append(qtm * jnp.exp(a[m] - a[m - d + 1]))
            else:
                qd[d].append(jnp.zeros_like(qtm))

    def cat(xs):
        return jnp.concatenate(xs, axis=0).astype(BF16)

    return dict(qt=cat(qt), kbar=cat(kbar), khat=cat(khat), qin=cat(qin), kout=cat(kout),
                qd=[cat(qd[d]) for d in range(2, ns)], decay=jnp.exp(b_last))


def _chunk_steps(heads, st_ref):
    c = heads[0][0].shape[0]
    ns = c // SUB
    ops = [_chunk_operands(q, k, bl, tot) for (q, k, _, bl, tot) in heads]
    raw = []
    for op in ops:
        s = [lax.dot_general(op["qt"], op["kbar"], NT_DIMS, preferred_element_type=F32),
             lax.dot_general(op["qt"], op["khat"], NT_DIMS, preferred_element_type=F32)]
        s += [lax.dot_general(qd, op["khat"], NT_DIMS, preferred_element_type=F32) for qd in op["qd"]]
        raw.append(s)
    ri = lax.broadcasted_iota(jnp.int32, (c, c), 0)
    ci = lax.broadcasted_iota(jnp.int32, (c, c), 1)
    dist = (ri >> SUB_SHIFT) - (ci >> SUB_SHIFT)
    outs = []
    for h, (op, s) in enumerate(zip(ops, raw)):
        scores = jnp.where((dist == 0) & (ri >= ci), s[0], jnp.where(dist == 1, s[1], 0.0))
        for d in range(2, ns):
            scores = jnp.where(dist == d, s[d], scores)
        vb = heads[h][2]
        o = jnp.dot(scores.astype(BF16), vb, preferred_element_type=F32)
        o = o + lax.dot_general(op["qin"], st_ref[h].astype(BF16), NT_DIMS,
                                preferred_element_type=F32)
        outs.append(o)
    for h, op in enumerate(ops):
        st_ref[h] = st_ref[h] * op["decay"] + lax.dot_general(
            heads[h][2], op["kout"], TN_DIMS, preferred_element_type=F32)
    return outs


def _sub_totals(bl):
    return [bl[(m + 1) * SUB - 1:(m + 1) * SUB, :] for m in range(bl.shape[0] // SUB)]


def _state_prologue(s0_ref, st_ref):
    @pl.when(pl.program_id(1) == 0)
    def _():
        for h in range(st_ref.shape[0]):
            if s0_ref is None:
                st_ref[h] = jnp.zeros(st_ref.shape[1:], F32)
            else:
                st_ref[h] = jnp.transpose(s0_ref[0, h])


def _state_epilogue(sn_ref, st_ref):
    @pl.when(pl.program_id(1) == pl.num_programs(1) - 1)
    def _():
        for h in range(st_ref.shape[0]):
            sn_ref[0, h] = jnp.transpose(st_ref[h])


def _chunk_loop(n_chunks, body, unroll):
    def step(c, carry):
        body(pl.ds(pl.multiple_of(c * CHUNK, CHUNK), CHUNK))
        return carry

    lax.fori_loop(0, n_chunks, step, 0, unroll=min(unroll, n_chunks))


def _head(ref, rows, h, width):
    return ref[rows, h * width:(h + 1) * width]


def _ret_chunk(get, put, cos, sin, st_ref):
    pos1 = ((lax.broadcasted_iota(jnp.int32, (CHUNK, 1), 0) & (SUB - 1)) + 1).astype(F32)
    half = DK_A // 2
    heads = []
    for h in range(H_A):
        lgam = float(np.log1p(-np.exp2(-5.0 - h)))
        bl = pos1 * lgam
        tot = [jnp.full((1, 1), lgam * SUB, F32)] * (CHUNK // SUB)
        qr, kr = get("q", h), get("k", h)
        q = qr * cos + pltpu.roll(qr, half, 1) * sin
        k = (kr * cos + pltpu.roll(kr, half, 1) * sin) * (DK_A ** -0.5)
        heads.append((q, k, get("v", h).astype(BF16), bl, tot))
    for h, o in enumerate(_chunk_steps(heads, st_ref)):
        mu = jnp.mean(o, axis=-1, keepdims=True)
        oc = o - mu
        var = jnp.mean(oc * oc, axis=-1, keepdims=True)
        put(h, ((oc * lax.rsqrt(var + EPS)) * _silu(get("g", h))).astype(BF16))


def _hgrn_lower_bound(lbp_ref, layer):
    lbp = lbp_ref[...]
    e = jnp.exp(lbp - jnp.max(lbp, axis=0, keepdims=True))
    sm = e / jnp.sum(e, axis=0, keepdims=True)
    return jnp.sum(sm[:layer + 1], axis=0, keepdims=True)


def _hgrn_chunk(get, put, lb_all, gn, st_ref):
    heads = []
    for h in range(H_B):
        lb = lb_all[:, h * DK_B:(h + 1) * DK_B]
        q = _silu(get("q", h)) * (DK_B ** -0.5)
        f = lb + (1.0 - lb) * _sigmoid(get("f", h))
        bl = _local_cumsum(jnp.log(f))
        heads.append((q, 1.0 - f, get("i", h).astype(BF16), bl, _sub_totals(bl)))
    for h, o in enumerate(_chunk_steps(heads, st_ref)):
        ms = jnp.mean(o * o, axis=-1, keepdims=True)
        put(h, (((o * lax.rsqrt(ms + EPS)) * gn) * _silu(get("g", h))).astype(BF16))


def _gla_chunk(get, put, low, wgk, bgk, gn, st_ref):
    x = jnp.dot(low.astype(BF16), wgk, preferred_element_type=F32) + bgk
    lg_all = (jnp.minimum(x, 0.0) - jnp.log1p(jnp.exp(-jnp.abs(x)))) * (1.0 / GATE_NORMALIZER)
    heads = []
    for h in range(H_C):
        bl = _local_cumsum(lg_all[:, h * DK_C:(h + 1) * DK_C])
        q = get("q", h) * (DK_C ** -0.5)
        heads.append((q, get("k", h), get("v", h).astype(BF16), bl, _sub_totals(bl)))
    for h, o in enumerate(_chunk_steps(heads, st_ref)):
        ms = jnp.mean(o * o, axis=-1, keepdims=True)
        put(h, (((o * lax.rsqrt(ms + EPS)) * gn) * _silu(get("g", h))).astype(BF16))


def _ref_getter(refs, widths, rows):
    return lambda name, h: _head(refs[name], rows, h, widths[name])


def _ref_putter(o_ref, width, rows):
    def put(h, x):
        o_ref[rows, h * width:(h + 1) * width] = x
    return put


def _ret_kernel(*refs, has_s0):
    if has_s0:
        q_ref, k_ref, v_ref, g_ref, cos_ref, sin_ref, s0_ref, o_ref, sn_ref, st_ref = refs
    else:
        q_ref, k_ref, v_ref, g_ref, cos_ref, sin_ref, o_ref, sn_ref, st_ref = refs
        s0_ref = None
    _state_prologue(s0_ref, st_ref)
    srcs = {"q": q_ref, "k": k_ref, "v": v_ref, "g": g_ref}
    widths = {"q": DK_A, "k": DK_A, "v": DV_A, "g": DV_A}

    def body(rows):
        _ret_chunk(_ref_getter(srcs, widths, rows), _ref_putter(o_ref, DV_A, rows),
                   cos_ref[rows, :], sin_ref[rows, :], st_ref)

    _chunk_loop(q_ref.shape[0] // CHUNK, body, 2)
    _state_epilogue(sn_ref, st_ref)


def _hgrn_kernel(*refs, has_s0, layer):
    if has_s0:
        q_ref, f_ref, i_ref, g_ref, lbp_ref, gn_ref, s0_ref, o_ref, sn_ref, st_ref = refs
    else:
        q_ref, f_ref, i_ref, g_ref, lbp_ref, gn_ref, o_ref, sn_ref, st_ref = refs
        s0_ref = None
    _state_prologue(s0_ref, st_ref)
    lb_all = _hgrn_lower_bound(lbp_ref, layer)
    gn = gn_ref[...]
    srcs = {"q": q_ref, "f": f_ref, "i": i_ref, "g": g_ref}
    widths = {"q": DK_B, "f": DK_B, "i": DV_B, "g": DV_B}

    def body(rows):
        _hgrn_chunk(_ref_getter(srcs, widths, rows), _ref_putter(o_ref, DV_B, rows),
                    lb_all, gn, st_ref)

    _chunk_loop(q_ref.shape[0] // CHUNK, body, 2)
    _state_epilogue(sn_ref, st_ref)


def _gla_kernel(*refs, has_s0):
    if has_s0:
        q_ref, k_ref, v_ref, g_ref, low_ref, wgk_ref, bgk_ref, gn_ref, s0_ref, o_ref, sn_ref, st_ref = refs
    else:
        q_ref, k_ref, v_ref, g_ref, low_ref, wgk_ref, bgk_ref, gn_ref, o_ref, sn_ref, st_ref = refs
        s0_ref = None
    _state_prologue(s0_ref, st_ref)
    gn = gn_ref[...]
    bgk = bgk_ref[...]
    srcs = {"q": q_ref, "k": k_ref, "v": v_ref, "g": g_ref}
    widths = {"q": DK_C, "k": DK_C, "v": DV_C, "g": DV_C}

    def body(rows):
        _gla_chunk(_ref_getter(srcs, widths, rows), _ref_putter(o_ref, DV_C, rows),
                   low_ref[rows, :], wgk_ref[...], bgk, gn, st_ref)

    _chunk_loop(q_ref.shape[0] // CHUNK, body, 2)
    _state_epilogue(sn_ref, st_ref)


def _row_spec(tt, width, nt, block):
    return pl.BlockSpec((tt, width), lambda b, t: (b * nt + t, block))


def _recurrence_call(body, in_specs, args, s0, bsz, seq, heads, dk, dv, tt):
    nt = seq // tt
    if s0 is not None:
        in_specs = in_specs + [pl.BlockSpec((1, heads, dk, dv), lambda b, t: (b, 0, 0, 0))]
        args = args + [s0]
    o, sn = pl.pallas_call(
        body,
        grid=(bsz, nt),
        in_specs=in_specs,
        out_specs=[
            pl.BlockSpec((tt, heads * dv), lambda b, t: (b * nt + t, 0)),
            pl.BlockSpec((1, heads, dk, dv), lambda b, t: (b, 0, 0, 0)),
        ],
        out_shape=[
            jax.ShapeDtypeStruct((bsz * seq, heads * dv), BF16),
            jax.ShapeDtypeStruct((bsz, heads, dk, dv), F32),
        ],
        scratch_shapes=[pltpu.VMEM((heads, dv, dk), F32)],
        compiler_params=_cparams(2),
        name="recurrence",
    )(*args)
    return o, sn


def _retention(p, cos2, sin2, s0, bsz, seq, tt):
    nt = seq // tt
    qa, va = H_A * DK_A, H_A * DV_A
    in_specs = [
        _row_spec(tt, qa, nt, 0),
        _row_spec(tt, qa, nt, 1),
        _row_spec(tt, va, nt, 2 * qa // va),
        _row_spec(tt, va, nt, (2 * qa + va) // va),
        pl.BlockSpec((tt, DK_A), lambda b, t: (t, 0)),
        pl.BlockSpec((tt, DK_A), lambda b, t: (t, 0)),
    ]
    args = [p, p, p, p, cos2, sin2]
    body = functools.partial(_ret_kernel, has_s0=s0 is not None)
    return _recurrence_call(body, in_specs, args, s0, bsz, seq, H_A, DK_A, DV_A, tt)


def _hgrn(p, lbp, gnorm, s0, bsz, seq, tt, layer):
    nt = seq // tt
    base = 2 * H_A * DK_A + 2 * H_A * DV_A
    qb = H_B * DK_B
    assert base % qb == 0 and H_B * DV_B == qb
    in_specs = [_row_spec(tt, qb, nt, base // qb + n) for n in range(4)]
    in_specs += [
        pl.BlockSpec(lbp.shape, lambda b, t: (0, 0)),
        pl.BlockSpec((1, DV_B), lambda b, t: (0, 0)),
    ]
    args = [p, p, p, p, lbp, gnorm.reshape(1, DV_B)]
    body = functools.partial(_hgrn_kernel, has_s0=s0 is not None, layer=layer)
    return _recurrence_call(body, in_specs, args, s0, bsz, seq, H_B, DK_B, DV_B, tt)


def _gla(p, low, wgk, bgk, gnorm, s0, bsz, seq, tt):
    nt = seq // tt
    qc, vc = H_C * DK_C, H_C * DV_C
    in_specs = [
        _row_spec(tt, qc, nt, 0),
        _row_spec(tt, qc, nt, 1),
        _row_spec(tt, vc, nt, 2 * qc // vc),
        _row_spec(tt, vc, nt, (2 * qc + vc) // vc),
        _row_spec(tt, low.shape[1], nt, 0),
        pl.BlockSpec(wgk.shape, lambda b, t: (0, 0)),
        pl.BlockSpec((1, qc), lambda b, t: (0, 0)),
        pl.BlockSpec((1, DV_C), lambda b, t: (0, 0)),
    ]
    args = [p, p, p, p, low, wgk, bgk.reshape(1, qc), gnorm.reshape(1, DV_C)]
    body = functools.partial(_gla_kernel, has_s0=s0 is not None)
    return _recurrence_call(body, in_specs, args, s0, bsz, seq, H_C, DK_C, DV_C, tt)


def _ffn_kernel(*refs, seq_rows, tiles_per_seq, has_cache, final_norm, jobs):
    refs = list(refs)
    h_ref, nw_ref, wa_ref, wu_ref, cw_ref, cb_ref, wd_ref = refs[:7]
    pos = 7
    cache_ref = None
    nf_ref = None
    if has_cache:
        cache_ref = refs[pos]
        pos += 1
    if final_norm:
        nf_ref = refs[pos]
        pos += 1
    nj = len(jobs)
    src_refs = refs[pos:pos + nj]
    out_ref, newc_ref = refs[pos + nj:pos + nj + 2]
    dst_refs = refs[pos + nj + 2:pos + 2 * nj + 2]
    hn_ref, carry_ref = refs[pos + 2 * nj + 2:]
    _run_cast_jobs(jobs, src_refs, dst_refs)

    i = pl.program_id(0)
    j = pl.program_id(1)
    tm, tf = h_ref.shape[0], wa_ref.shape[1]

    @pl.when(j == 0)
    def _():
        _rmsnorm_rows_to(h_ref, nw_ref, hn_ref, 128)
        out_ref[...] = h_ref[...]

    if tiles_per_seq > 1:
        @pl.when(i % tiles_per_seq == 0)
        def _():
            carry_ref[j] = jnp.zeros((CONV_W - 1, tf), F32)

    hn = hn_ref[...]
    a = jnp.dot(hn, wa_ref[...], preferred_element_type=F32)
    u = jnp.dot(hn, wu_ref[...], preferred_element_type=F32)

    nseq = tm // seq_rows
    r8 = lax.broadcasted_iota(jnp.int32, (SUBLANES, 1), 0)
    sh1 = pltpu.roll(a, 1, 0)
    sh2 = pltpu.roll(a, 2, 0)
    a1_parts, a2_parts = [], []
    for s in range(nseq):
        if tiles_per_seq > 1:
            p0, p1 = carry_ref[j, 0:1, :], carry_ref[j, 1:2, :]
        elif has_cache:
            p0, p1 = cache_ref[s, 0:1, :], cache_ref[s, 1:2, :]
        else:
            p0 = p1 = jnp.zeros((1, tf), F32)
        lo, hi = s * seq_rows, (s + 1) * seq_rows
        head = slice(lo, lo + SUBLANES)
        a1_parts += [jnp.where(r8 == 0, p1, sh1[head]), sh1[lo + SUBLANES:hi]]
        a2_parts += [jnp.where(r8 == 0, p0, jnp.where(r8 == 1, p1, sh2[head])),
                     sh2[lo + SUBLANES:hi]]
    a1 = jnp.concatenate(a1_parts, axis=0)
    a2 = jnp.concatenate(a2_parts, axis=0)
    cw = cw_ref[...]
    ac = cb_ref[...] + a2 * cw[0:1, :] + a1 * cw[1:2, :] + a * cw[2:3, :]

    if tiles_per_seq > 1:
        carry_ref[j] = a[tm - (CONV_W - 1):, :]
    for s in range(nseq):
        newc_ref[s] = a[(s + 1) * seq_rows - (CONV_W - 1):(s + 1) * seq_rows, :]

    y = (_silu(ac) * u).astype(BF16)
    out_ref[...] += jnp.dot(y, wd_ref[...], preferred_element_type=F32)

    if final_norm:
        @pl.when(j == pl.num_programs(1) - 1)
        def _():
            rows = min(128, tm)

            def body(c, carry):
                r = pl.multiple_of(c * rows, rows)
                x = out_ref[pl.ds(r, rows), :]
                ms = jnp.mean(x * x, axis=-1, keepdims=True)
                out_ref[pl.ds(r, rows), :] = (x * lax.rsqrt(ms + EPS)) * nf_ref[...]
                return carry

            lax.fori_loop(0, tm // rows, body, 0)


def _ffn(h, nw, wa, wu, cw, cb, wd, cache, nf, jobs=(), *, bsz, seq, tm, tf):
    m, d = h.shape
    dffp = wa.shape[1]
    nj = dffp // tf
    jobs = list(jobs)
    if seq >= tm:
        tiles_per_seq = seq // tm
        seq_rows = tm
        seqs_per_tile = 1
    else:
        tiles_per_seq = 1
        seq_rows = seq
        seqs_per_tile = tm // seq
    assert seq_rows & (seq_rows - 1) == 0
    grid = (m // tm, nj)
    in_specs = [
        pl.BlockSpec((tm, d), lambda i, j: (i, 0)),
        pl.BlockSpec((1, d), lambda i, j: (0, 0)),
        pl.BlockSpec((d, tf), lambda i, j: (0, j)),
        pl.BlockSpec((d, tf), lambda i, j: (0, j)),
        pl.BlockSpec((CONV_W, tf), lambda i, j: (0, j)),
        pl.BlockSpec((1, tf), lambda i, j: (0, j)),
        pl.BlockSpec((tf, d), lambda i, j: (j, 0)),
    ]
    args = [h, nw.reshape(1, d), wa, wu, cw, cb, wd]
    if cache is not None:
        assert tiles_per_seq == 1
        in_specs.append(pl.BlockSpec((seqs_per_tile, CONV_W - 1, tf), lambda i, j: (i, 0, j)))
        args.append(cache)
    if nf is not None:
        in_specs.append(pl.BlockSpec((1, d), lambda i, j: (0, 0)))
        args.append(nf.reshape(1, d))
    newc_spec = pl.BlockSpec((seqs_per_tile, CONV_W - 1, tf), lambda i, j: (i, 0, j))
    job_io = [_cast_job_io(job, grid) for job in jobs]
    in_specs += [io[0] for io in job_io]
    args += [job["src"] for job in jobs]
    body = functools.partial(
        _ffn_kernel, seq_rows=seq_rows, tiles_per_seq=tiles_per_seq,
        has_cache=cache is not None, final_norm=nf is not None, jobs=jobs)
    outs = pl.pallas_call(
        body,
        grid=grid,
        in_specs=in_specs,
        out_specs=[pl.BlockSpec((tm, d), lambda i, j: (i, 0)), newc_spec] + [io[1] for io in job_io],
        out_shape=[
            jax.ShapeDtypeStruct((m, d), F32),
            jax.ShapeDtypeStruct((grid[0] * seqs_per_tile, CONV_W - 1, dffp), F32),
        ] + [io[2] for io in job_io],
        scratch_shapes=[
            pltpu.VMEM((tm, d), BF16),
            pltpu.VMEM((nj, CONV_W - 1, tf), F32),
        ],
        compiler_params=_cparams(2),
        name="ffn",
    )(*args)
    out, newc = outs[0], outs[1]
    if tiles_per_seq > 1:
        newc = newc.reshape(bsz, tiles_per_seq, CONV_W - 1, dffp)[:, -1]
    return out, newc, list(outs[2:])


def _pad_to(x, axis, size):
    pad = size - x.shape[axis]
    if pad == 0:
        return x
    widths = [(0, 0)] * x.ndim
    widths[axis] = (0, pad)
    return jnp.pad(x, widths)


def _rotary_tables(pos):
    half = DK_A // 2
    inv = ROPE_BASE ** (-jnp.arange(half, dtype=F32) / half)
    ang = pos.astype(F32)[:, None] * inv[None, :]
    cos, sin = jnp.cos(ang), jnp.sin(ang)
    return jnp.concatenate([cos, cos], axis=-1), jnp.concatenate([-sin, sin], axis=-1)


FFN_TILE = 512


def _in_proj_source(raw, l):
    return (raw["w_in_even"], l // 2) if l % 2 == 0 else (raw["w_in_odd"], l // 2)


def _initial_weights(raw):
    w = {("in", 0): raw["w_in_even"][0].astype(BF16)}
    depth = raw["norm_mix"].shape[0]
    for l in range(1, depth, 2):
        src = raw["w_in_odd"][l // 2]
        w[("low", l)] = _pad_to(src[:, src.shape[1] - GK_RANK:], 1, LANES).astype(BF16)
        w[("gk2", l)] = _pad_to(raw["w_gk2"][l // 2], 0, LANES).astype(BF16)
    return w


def _cast_now(job):
    src = job["src"][job["lead"]]
    x = src[:, job["col"] * job["cin"]:(job["col"] + 1) * job["cin"]].astype(BF16)
    return _pad_to(_pad_to(x, 1, job["cout"]), 0, job["n_dst"] * job["rb"])


def _missing_jobs(w, wanted, n_steps):
    keys, jobs = [], []
    for key, job in wanted:
        if key in w:
            continue
        if job["n_dst"] <= n_steps:
            keys.append(key)
            jobs.append(job)
        else:
            w[key] = _cast_now(job)
    return keys, jobs


def _layer_weight_jobs(raw, l, dffp):
    w_out = raw["w_out_even"] if l % 2 == 0 else raw["w_out_odd"]
    up, down = raw["ffn_w_up"], raw["ffn_w_down"]
    dff = down.shape[1]
    return [
        (("out", l), _cast_job(w_out, (l // 2,), w_out.shape[1], w_out.shape[2], 64)),
        (("up_a", l), _cast_job(up, (l,), up.shape[1], dff, 64, col=0, cout=dffp)),
        (("up_u", l), _cast_job(up, (l,), up.shape[1], dff, 64, col=1, cout=dffp)),
        (("down", l), _cast_job(down, (l,), dff, down.shape[2], 128, rows_out=dffp)),
    ]


def _tiles(m, seq):
    return min(1024, m), 1024, min(512, m), 2048, min(512, m), min(512, seq)


def _run_group(x, pos0, s_ret, s_hgrn, s_gla, s_conv, raw, w):
    bsz, seq, d = x.shape
    m = bsz * seq
    depth = raw["norm_mix"].shape[0]
    dff = raw["ffn_w_down"].shape[1]
    dffp = -(-dff // FFN_TILE) * FFN_TILE
    tm, tn, tm_out, tn_out, tm_ffn, tt = _tiles(m, seq)
    cos2, sin2 = _rotary_tables(pos0 + jnp.arange(seq, dtype=jnp.int32))
    conv_w = _pad_to(raw["ffn_conv_w"], 2, dffp)
    conv_b = _pad_to(raw["ffn_conv_b"], 1, dffp)[:, None, :]

    h = x.reshape(m, d)
    new_ret, new_hgrn, new_gla, new_conv = [], [], [], []
    for l in range(depth):
        i = l // 2
        w_in = w[("in", l)]
        n = w_in.shape[1] if l % 2 == 0 else w_in.shape[1] - GK_RANK
        keys, jobs = _missing_jobs(w, _layer_weight_jobs(raw, l, dffp), (m // tm) * (n // tn))
        p, low, casts = _norm_matmul(h, raw["norm_mix"][l], w_in, n, w.get(("low", l)), jobs, tm=tm, tn=tn)
        w.update(zip(keys, casts))
        if l % 2 == 0:
            o_a, sr = _retention(p, cos2, sin2, None if s_ret is None else s_ret[i], bsz, seq, tt)
            o_b, sh = _hgrn(p, raw["hgrn_lb"], raw["hgrn_gnorm"][i],
                            None if s_hgrn is None else s_hgrn[i], bsz, seq, tt, l)
            o_list = [o_a, o_b]
            new_ret.append(sr)
            new_hgrn.append(sh)
        else:
            o_c, sg = _gla(p, low, w[("gk2", l)], raw["b_gk2"][i], raw["gla_gnorm"][i],
                           None if s_gla is None else s_gla[i], bsz, seq, tt)
            o_list = [o_c]
            new_gla.append(sg)
        h = _matmul_res(o_list, w[("out", l)], h, tm=tm_out, tn=tn_out)

        wanted = []
        if l + 1 < depth:
            src, idx = _in_proj_source(raw, l + 1)
            wanted = [(("in", l + 1), _cast_job(src, (idx,), src.shape[1], src.shape[2], 16))]
        keys, jobs = _missing_jobs(w, wanted, (m // tm_ffn) * (dffp // FFN_TILE))
        cache = None if s_conv is None else _pad_to(s_conv[l], 2, dffp)
        h, nc, casts = _ffn(h, raw["norm_ffn"][l], w[("up_a", l)], w[("up_u", l)], conv_w[l], conv_b[l],
                            w[("down", l)], cache, raw["norm_final"] if l == depth - 1 else None, jobs,
                            bsz=bsz, seq=seq, tm=tm_ffn, tf=FFN_TILE)
        w.update(zip(keys, casts))
        new_conv.append(nc[:, :, :dff])
    return (h.reshape(bsz, seq, d), jnp.stack(new_ret), jnp.stack(new_hgrn),
            jnp.stack(new_gla), jnp.stack(new_conv))


def kernel(x_prompt, x_sample, state_ret, state_hgrn, state_gla, cache_ffn_conv, norm_mix, norm_ffn, norm_final, w_in_even, w_out_even, hgrn_lb, hgrn_gnorm, w_in_odd, w_gk2, b_gk2, gla_gnorm, w_out_odd, ffn_w_up, ffn_conv_w, ffn_conv_b, ffn_w_down):
    raw = dict(norm_mix=norm_mix, norm_ffn=norm_ffn, norm_final=norm_final, w_in_even=w_in_even,
               w_out_even=w_out_even, hgrn_lb=hgrn_lb, hgrn_gnorm=hgrn_gnorm, w_in_odd=w_in_odd,
               w_gk2=w_gk2, b_gk2=b_gk2, gla_gnorm=gla_gnorm, w_out_odd=w_out_odd, ffn_w_up=ffn_w_up,
               ffn_conv_w=ffn_conv_w, ffn_conv_b=ffn_conv_b, ffn_w_down=ffn_w_down)
    w = _initial_weights(raw)
    y_p, ret_p, hgrn_p, gla_p, conv_p = _run_group(x_prompt, 0, None, None, None, None, raw, w)
    y_s, ret_s, hgrn_s, gla_s, conv_s = _run_group(
        x_sample, PAST_LEN, state_ret, state_hgrn, state_gla, cache_ffn_conv, raw, w)
    return (y_p, y_s, ret_p, ret_s, hgrn_p, hgrn_s, gla_p, gla_s, conv_p, conv_s)
```

```python
import functools

import numpy as np
import jax
import jax.numpy as jnp
from jax import lax
from jax.experimental import pallas as pl
from jax.experimental.pallas import tpu as pltpu

F32 = jnp.float32
BF16 = jnp.bfloat16

EPS = 1e-6
ROPE_BASE = 10000.0
GATE_NORMALIZER = 16.0
PAST_LEN = 1024

H_A, DK_A, DV_A = 4, 128, 256
H_B, DK_B, DV_B = 8, 128, 128
H_C, DK_C, DV_C = 4, 256, 512
GK_RANK = 16
CONV_W = 3

LANES = 128
SUBLANES = 8
VMEM_LIMIT_BYTES = 56 * 1024 * 1024

CHUNK = 64
SUB = 16
SUB_SHIFT = 4
assert 1 << SUB_SHIFT == SUB and CHUNK % SUB == 0

NT_DIMS = (((1,), (1,)), ((), ()))
TN_DIMS = (((0,), (0,)), ((), ()))


def _cparams(n_axes):
    return pltpu.CompilerParams(
        dimension_semantics=("arbitrary",) * n_axes,
        vmem_limit_bytes=VMEM_LIMIT_BYTES,
    )


def _sigmoid(x):
    return 1.0 / (1.0 + jnp.exp(-x))


def _silu(x):
    return x * _sigmoid(x)


def _rmsnorm_rows_to(x_ref, nw_ref, dst_ref, rows):
    rows = min(rows, x_ref.shape[0])
    assert x_ref.shape[0] % rows == 0
    n = x_ref.shape[0] // rows

    def body(c, carry):
        r = pl.multiple_of(c * rows, rows)
        x = x_ref[pl.ds(r, rows), :]
        ms = jnp.mean(x * x, axis=-1, keepdims=True)
        dst_ref[pl.ds(r, rows), :] = ((x * lax.rsqrt(ms + EPS)) * nw_ref[...]).astype(BF16)
        return carry

    lax.fori_loop(0, n, body, 0)


def _cast_job(src, lead, rows, cin, rb, *, col=0, cout=None, rows_out=None):
    cout = cin if cout is None else cout
    rows_out = rows if rows_out is None else rows_out
    assert rows % rb == 0 and rows_out % rb == 0
    return dict(src=src, lead=tuple(lead), rb=rb, col=col, cin=cin, cout=cout,
                n_src=rows // rb, n_dst=rows_out // rb, transpose=False)


def _transpose_cast_job(src, lead, rows, cin, rb):
    n = -(-rows // rb)
    return dict(src=src, lead=tuple(lead), rb=rb, col=0, cin=cin, cout=cin, rows=rows,
                n_src=n, n_dst=n, transpose=True)


def _cast_job_io(job, grid):
    assert job["n_dst"] <= grid[0] * grid[1]
    lead = job["lead"]

    def src_index(i, j):
        return lead + (jnp.minimum(i * grid[1] + j, job["n_src"] - 1), job["col"])

    def dst_index(i, j):
        blk = jnp.minimum(i * grid[1] + j, job["n_dst"] - 1)
        return (0, blk) if job["transpose"] else (blk, 0)

    rb, cin, cout, n_dst = job["rb"], job["cin"], job["cout"], job["n_dst"]
    dst_block, dst_shape = ((cin, rb), (cin, n_dst * rb)) if job["transpose"] else ((rb, cout), (n_dst * rb, cout))
    return (pl.BlockSpec((None,) * len(lead) + (rb, cin), src_index),
            pl.BlockSpec(dst_block, dst_index),
            jax.ShapeDtypeStruct(dst_shape, BF16))


def _run_cast_jobs(jobs, src_refs, dst_refs):
    step = pl.program_id(0) * pl.num_programs(1) + pl.program_id(1)
    for job, src_ref, dst_ref in zip(jobs, src_refs, dst_refs):
        if job["transpose"]:
            first_row = jnp.minimum(step, job["n_src"] - 1) * job["rb"]
            row = lax.broadcasted_iota(jnp.int32, (job["rb"], 1), 0)
            x = jnp.where(row < job["rows"] - first_row, src_ref[...], 0.0)
            dst_ref[...] = jnp.transpose(x).astype(BF16)
            continue
        y = src_ref[...].astype(BF16)
        if job["n_dst"] > job["n_src"]:
            y = jnp.where(step < job["n_src"], y, jnp.zeros_like(y))
        if job["cout"] > job["cin"]:
            dst_ref[:, :job["cin"]] = y
            dst_ref[:, job["cin"]:] = jnp.zeros((job["rb"], job["cout"] - job["cin"]), BF16)
        else:
            dst_ref[...] = y


def _norm_matmul_kernel(*refs, has_low, jobs):
    n_in = 3 + has_low
    n_out = 1 + has_low
    nj = len(jobs)
    x_ref, nw_ref, w_ref = refs[:3]
    src_refs = refs[n_in:n_in + nj]
    o_ref = refs[n_in + nj]
    dst_refs = refs[n_in + nj + n_out:n_in + nj + n_out + nj]
    hn_ref = refs[-1]

    @pl.when(pl.program_id(1) == 0)
    def _():
        _rmsnorm_rows_to(x_ref, nw_ref, hn_ref, 128)
        if has_low:
            low_ref = refs[n_in + nj + 1]
            low_ref[...] = jnp.dot(hn_ref[...], refs[3][...], preferred_element_type=F32)

    o_ref[...] = jnp.dot(hn_ref[...], w_ref[...], preferred_element_type=F32)
    _run_cast_jobs(jobs, src_refs, dst_refs)


def _norm_matmul(x, nw, w, n, n_low=0, jobs=(), *, tm, tn):
    m, d = x.shape
    assert n % tn == 0 and n + n_low <= w.shape[1]
    w_low = w if n_low else None
    grid = (m // tm, n // tn)
    jobs = list(jobs)
    job_io = [_cast_job_io(job, grid) for job in jobs]
    in_specs = [
        pl.BlockSpec((tm, d), lambda i, j: (i, 0)),
        pl.BlockSpec((1, d), lambda i, j: (0, 0)),
        pl.BlockSpec((d, tn), lambda i, j: (0, j)),
    ]
    out_shape = [jax.ShapeDtypeStruct((m, n), F32)]
    out_specs = [pl.BlockSpec((tm, tn), lambda i, j: (i, j))]
    args = [x, nw.reshape(1, d), w]
    if w_low is not None:
        nl = n_low
        assert n % nl == 0
        in_specs.append(pl.BlockSpec((d, nl), lambda i, j: (0, n // nl)))
        out_shape.append(jax.ShapeDtypeStruct((m, nl), F32))
        out_specs.append(pl.BlockSpec((tm, nl), lambda i, j: (i, 0)))
        args.append(w_low)
    in_specs += [io[0] for io in job_io]
    out_specs += [io[1] for io in job_io]
    out_shape += [io[2] for io in job_io]
    args += [job["src"] for job in jobs]
    outs = pl.pallas_call(
        functools.partial(_norm_matmul_kernel, has_low=w_low is not None, jobs=jobs),
        grid=grid,
        in_specs=in_specs,
        out_specs=out_specs,
        out_shape=out_shape,
        scratch_shapes=[pltpu.VMEM((tm, d), BF16)],
        compiler_params=_cparams(2),
        name="norm_matmul",
    )(*args)
    n_main = 1 + (w_low is not None)
    return outs[0], (outs[1] if w_low is not None else None), list(outs[n_main:])


def _matmul_res_kernel(*refs, n_in):
    a_refs = refs[:n_in]
    w_refs = refs[n_in:2 * n_in]
    res_ref = refs[2 * n_in]
    o_ref = refs[2 * n_in + 1]
    acc = res_ref[...]
    for a_ref, w_ref in zip(a_refs, w_refs):
        acc = acc + jnp.dot(a_ref[...], w_ref[...], preferred_element_type=F32)
    o_ref[...] = acc


def _matmul_res(a_list, w, res, *, tm, tn):
    m, n = res.shape
    ka = a_list[0].shape[1]
    assert all(a.shape[1] == ka for a in a_list) and ka * len(a_list) == w.shape[0]
    grid = (m // tm, n // tn)
    in_specs = [pl.BlockSpec((tm, ka), lambda i, j: (i, 0)) for _ in a_list]
    for rb in range(len(a_list)):
        in_specs.append(pl.BlockSpec((ka, tn), lambda i, j, rb=rb: (rb, j)))
    in_specs.append(pl.BlockSpec((tm, tn), lambda i, j: (i, j)))
    return pl.pallas_call(
        functools.partial(_matmul_res_kernel, n_in=len(a_list)),
        grid=grid,
        in_specs=in_specs,
        out_specs=pl.BlockSpec((tm, tn), lambda i, j: (i, j)),
        out_shape=jax.ShapeDtypeStruct((m, n), F32),
        compiler_params=_cparams(2),
        name="matmul_res",
    )(*a_list, *([w] * len(a_list)), res)


def _local_cumsum(lg):
    pos = lax.broadcasted_iota(jnp.int32, lg.shape, 0) & (SUB - 1)
    x = lg
    s = 1
    while s < SUB:
        x = x + jnp.where(pos >= s, pltpu.roll(x, s, 0), 0.0)
        s *= 2
    return x


def _chunk_operands(q, k, bl, tot):
    ns = q.shape[0] // SUB
    a = [jnp.zeros_like(tot[0])]
    for m in range(ns):
        a.append(a[-1] + tot[m])
    b_last = a[ns]

    qt, kbar, khat, qin, kout = [], [], [], [], []
    qd = {d: [] for d in range(2, ns)}
    for m in range(ns):
        sl = slice(m * SUB, (m + 1) * SUB)
        blm = bl[sl]
        qtm = q[sl] * jnp.exp(blm)
        km = k[sl]
        khm = km * jnp.exp(tot[m] - blm)
        qt.append(qtm)
        kbar.append(km * jnp.exp(-blm))
        khat.append(khm)
        qin.append(qtm * jnp.exp(a[m]))
        kout.append(khm * jnp.exp(b_last - a[m + 1]))
        for d in range(2, ns):
            if m >= d:
                qd[d].append(qtm * jnp.exp(a[m] - a[m - d + 1]))
            else:
                qd[d].append(jnp.zeros_like(qtm))

    def cat(xs):
        return jnp.concatenate(xs, axis=0).astype(BF16)

    return dict(qt=cat(qt), kbar=cat(kbar), khat=cat(khat), qin=cat(qin), kout=cat(kout),
                qd=[cat(qd[d]) for d in range(2, ns)], decay=jnp.exp(b_last))


def _chunk_steps(heads, st_ref):
    c = heads[0][0].shape[0]
    ns = c // SUB
    ops = [_chunk_operands(q, k, bl, tot) for (q, k, _, bl, tot) in heads]
    raw = []
    for op in ops:
        s = [lax.dot_general(op["qt"], op["kbar"], NT_DIMS, preferred_element_type=F32),
             lax.dot_general(op["qt"], op["khat"], NT_DIMS, preferred_element_type=F32)]
        s += [lax.dot_general(qd, op["khat"], NT_DIMS, preferred_element_type=F32) for qd in op["qd"]]
        raw.append(s)
    ri = lax.broadcasted_iota(jnp.int32, (c, c), 0)
    ci = lax.broadcasted_iota(jnp.int32, (c, c), 1)
    dist = (ri >> SUB_SHIFT) - (ci >> SUB_SHIFT)
    outs = []
    for h, (op, s) in enumerate(zip(ops, raw)):
        scores = jnp.where((dist == 0) & (ri >= ci), s[0], jnp.where(dist == 1, s[1], 0.0))
        for d in range(2, ns):
            scores = jnp.where(dist == d, s[d], scores)
        vb = heads[h][2]
        o = jnp.dot(scores.astype(BF16), vb, preferred_element_type=F32)
        o = o + lax.dot_general(op["qin"], st_ref[h].astype(BF16), NT_DIMS,
                                preferred_element_type=F32)
        outs.append(o)
    for h, op in enumerate(ops):
        st_ref[h] = st_ref[h] * op["decay"] + lax.dot_general(
            heads[h][2], op["kout"], TN_DIMS, preferred_element_type=F32)
    return outs


def _sub_totals(bl):
    return [bl[(m + 1) * SUB - 1:(m + 1) * SUB, :] for m in range(bl.shape[0] // SUB)]


def _state_prologue(s0_ref, st_ref):
    @pl.when(pl.program_id(1) == 0)
    def _():
        for h in range(st_ref.shape[0]):
            if s0_ref is None:
                st_ref[h] = jnp.zeros(st_ref.shape[1:], F32)
            else:
                st_ref[h] = jnp.transpose(s0_ref[0, h])


def _state_epilogue(sn_ref, st_ref):
    @pl.when(pl.program_id(1) == pl.num_programs(1) - 1)
    def _():
        for h in range(st_ref.shape[0]):
            sn_ref[0, h] = jnp.transpose(st_ref[h])


def _chunk_loop(n_chunks, body, unroll):
    def step(c, carry):
        body(pl.ds(pl.multiple_of(c * CHUNK, CHUNK), CHUNK))
        return carry

    lax.fori_loop(0, n_chunks, step, 0, unroll=min(unroll, n_chunks))


def _head(ref, rows, h, width):
    return ref[rows, h * width:(h + 1) * width]


def _ret_chunk(get, put, cos, sin, st_ref):
    pos1 = ((lax.broadcasted_iota(jnp.int32, (CHUNK, 1), 0) & (SUB - 1)) + 1).astype(F32)
    half = DK_A // 2
    heads = []
    for h in range(H_A):
        lgam = float(np.log1p(-np.exp2(-5.0 - h)))
        bl = pos1 * lgam
        tot = [jnp.full((1, 1), lgam * SUB, F32)] * (CHUNK // SUB)
        qr, kr = get("q", h), get("k", h)
        q = qr * cos + pltpu.roll(qr, half, 1) * sin
        k = (kr * cos + pltpu.roll(kr, half, 1) * sin) * (DK_A ** -0.5)
        heads.append((q, k, get("v", h).astype(BF16), bl, tot))
    for h, o in enumerate(_chunk_steps(heads, st_ref)):
        mu = jnp.mean(o, axis=-1, keepdims=True)
        oc = o - mu
        var = jnp.mean(oc * oc, axis=-1, keepdims=True)
        put(h, ((oc * lax.rsqrt(var + EPS)) * _silu(get("g", h))).astype(BF16))


def _hgrn_lower_bound(lbp_ref, layer):
    lbp = lbp_ref[...]
    e = jnp.exp(lbp - jnp.max(lbp, axis=0, keepdims=True))
    sm = e / jnp.sum(e, axis=0, keepdims=True)
    return jnp.sum(sm[:layer + 1], axis=0, keepdims=True)


def _hgrn_chunk(get, put, lb_all, gn, st_ref):
    heads = []
    for h in range(H_B):
        lb = lb_all[:, h * DK_B:(h + 1) * DK_B]
        q = _silu(get("q", h)) * (DK_B ** -0.5)
        f = lb + (1.0 - lb) * _sigmoid(get("f", h))
        bl = _local_cumsum(jnp.log(f))
        heads.append((q, 1.0 - f, get("i", h).astype(BF16), bl, _sub_totals(bl)))
    for h, o in enumerate(_chunk_steps(heads, st_ref)):
        ms = jnp.mean(o * o, axis=-1, keepdims=True)
        put(h, (((o * lax.rsqrt(ms + EPS)) * gn) * _silu(get("g", h))).astype(BF16))


def _gla_chunk(get, put, low, wgk, bgk, gn, st_ref):
    x = jnp.dot(low.astype(BF16), wgk, preferred_element_type=F32) + bgk
    lg_all = (jnp.minimum(x, 0.0) - jnp.log1p(jnp.exp(-jnp.abs(x)))) * (1.0 / GATE_NORMALIZER)
    heads = []
    for h in range(H_C):
        bl = _local_cumsum(lg_all[:, h * DK_C:(h + 1) * DK_C])
        q = get("q", h) * (DK_C ** -0.5)
        heads.append((q, get("k", h), get("v", h).astype(BF16), bl, _sub_totals(bl)))
    for h, o in enumerate(_chunk_steps(heads, st_ref)):
        ms = jnp.mean(o * o, axis=-1, keepdims=True)
        put(h, (((o * lax.rsqrt(ms + EPS)) * gn) * _silu(get("g", h))).astype(BF16))


def _ref_getter(refs, widths, rows):
    return lambda name, h: _head(refs[name], rows, h, widths[name])


def _ref_putter(o_ref, width, rows):
    def put(h, x):
        o_ref[rows, h * width:(h + 1) * width] = x
    return put


def _ret_kernel(*refs, has_s0):
    if has_s0:
        q_ref, k_ref, v_ref, g_ref, cos_ref, sin_ref, s0_ref, o_ref, sn_ref, st_ref = refs
    else:
        q_ref, k_ref, v_ref, g_ref, cos_ref, sin_ref, o_ref, sn_ref, st_ref = refs
        s0_ref = None
    _state_prologue(s0_ref, st_ref)
    srcs = {"q": q_ref, "k": k_ref, "v": v_ref, "g": g_ref}
    widths = {"q": DK_A, "k": DK_A, "v": DV_A, "g": DV_A}

    def body(rows):
        _ret_chunk(_ref_getter(srcs, widths, rows), _ref_putter(o_ref, DV_A, rows),
                   cos_ref[rows, :], sin_ref[rows, :], st_ref)

    _chunk_loop(q_ref.shape[0] // CHUNK, body, 2)
    _state_epilogue(sn_ref, st_ref)


def _hgrn_kernel(*refs, has_s0, layer):
    if has_s0:
        q_ref, f_ref, i_ref, g_ref, lbp_ref, gn_ref, s0_ref, o_ref, sn_ref, st_ref = refs
    else:
        q_ref, f_ref, i_ref, g_ref, lbp_ref, gn_ref, o_ref, sn_ref, st_ref = refs
        s0_ref = None
    _state_prologue(s0_ref, st_ref)
    lb_all = _hgrn_lower_bound(lbp_ref, layer)
    gn = gn_ref[...]
    srcs = {"q": q_ref, "f": f_ref, "i": i_ref, "g": g_ref}
    widths = {"q": DK_B, "f": DK_B, "i": DV_B, "g": DV_B}

    def body(rows):
        _hgrn_chunk(_ref_getter(srcs, widths, rows), _ref_putter(o_ref, DV_B, rows),
                    lb_all, gn, st_ref)

    _chunk_loop(q_ref.shape[0] // CHUNK, body, 2)
    _state_epilogue(sn_ref, st_ref)


def _gla_kernel(*refs, has_s0):
    if has_s0:
        q_ref, k_ref, v_ref, g_ref, low_ref, wgk_ref, bgk_ref, gn_ref, s0_ref, o_ref, sn_ref, st_ref = refs
    else:
        q_ref, k_ref, v_ref, g_ref, low_ref, wgk_ref, bgk_ref, gn_ref, o_ref, sn_ref, st_ref = refs
        s0_ref = None
    _state_prologue(s0_ref, st_ref)
    gn = gn_ref[...]
    bgk = bgk_ref[...]
    srcs = {"q": q_ref, "k": k_ref, "v": v_ref, "g": g_ref}
    widths = {"q": DK_C, "k": DK_C, "v": DV_C, "g": DV_C}

    def body(rows):
        _gla_chunk(_ref_getter(srcs, widths, rows), _ref_putter(o_ref, DV_C, rows),
                   low_ref[rows, :], wgk_ref[...], bgk, gn, st_ref)

    _chunk_loop(q_ref.shape[0] // CHUNK, body, 2)
    _state_epilogue(sn_ref, st_ref)


def _row_spec(tt, width, nt, block):
    return pl.BlockSpec((tt, width), lambda b, t: (b * nt + t, block))


def _recurrence_call(body, in_specs, args, s0, bsz, seq, heads, dk, dv, tt):
    nt = seq // tt
    if s0 is not None:
        in_specs = in_specs + [pl.BlockSpec((1, heads, dk, dv), lambda b, t: (b, 0, 0, 0))]
        args = args + [s0]
    o, sn = pl.pallas_call(
        body,
        grid=(bsz, nt),
        in_specs=in_specs,
        out_specs=[
            pl.BlockSpec((tt, heads * dv), lambda b, t: (b * nt + t, 0)),
            pl.BlockSpec((1, heads, dk, dv), lambda b, t: (b, 0, 0, 0)),
        ],
        out_shape=[
            jax.ShapeDtypeStruct((bsz * seq, heads * dv), BF16),
            jax.ShapeDtypeStruct((bsz, heads, dk, dv), F32),
        ],
        scratch_shapes=[pltpu.VMEM((heads, dv, dk), F32)],
        compiler_params=_cparams(2),
        name="recurrence",
    )(*args)
    return o, sn


def _retention(p, cos2, sin2, s0, bsz, seq, tt):
    nt = seq // tt
    qa, va = H_A * DK_A, H_A * DV_A
    in_specs = [
        _row_spec(tt, qa, nt, 0),
        _row_spec(tt, qa, nt, 1),
        _row_spec(tt, va, nt, 2 * qa // va),
        _row_spec(tt, va, nt, (2 * qa + va) // va),
        pl.BlockSpec((tt, DK_A), lambda b, t: (t, 0)),
        pl.BlockSpec((tt, DK_A), lambda b, t: (t, 0)),
    ]
    args = [p, p, p, p, cos2, sin2]
    body = functools.partial(_ret_kernel, has_s0=s0 is not None)
    return _recurrence_call(body, in_specs, args, s0, bsz, seq, H_A, DK_A, DV_A, tt)


def _hgrn(p, lbp, gnorm, s0, bsz, seq, tt, layer):
    nt = seq // tt
    base = 2 * H_A * DK_A + 2 * H_A * DV_A
    qb = H_B * DK_B
    assert base % qb == 0 and H_B * DV_B == qb
    in_specs = [_row_spec(tt, qb, nt, base // qb + n) for n in range(4)]
    in_specs += [
        pl.BlockSpec(lbp.shape, lambda b, t: (0, 0)),
        pl.BlockSpec((1, DV_B), lambda b, t: (0, 0)),
    ]
    args = [p, p, p, p, lbp, gnorm.reshape(1, DV_B)]
    body = functools.partial(_hgrn_kernel, has_s0=s0 is not None, layer=layer)
    return _recurrence_call(body, in_specs, args, s0, bsz, seq, H_B, DK_B, DV_B, tt)


def _gla(p, low, wgk, bgk, gnorm, s0, bsz, seq, tt):
    nt = seq // tt
    qc, vc = H_C * DK_C, H_C * DV_C
    in_specs = [
        _row_spec(tt, qc, nt, 0),
        _row_spec(tt, qc, nt, 1),
        _row_spec(tt, vc, nt, 2 * qc // vc),
        _row_spec(tt, vc, nt, (2 * qc + vc) // vc),
        _row_spec(tt, low.shape[1], nt, 0),
        pl.BlockSpec(wgk.shape, lambda b, t: (0, 0)),
        pl.BlockSpec((1, qc), lambda b, t: (0, 0)),
        pl.BlockSpec((1, DV_C), lambda b, t: (0, 0)),
    ]
    args = [p, p, p, p, low, wgk, bgk.reshape(1, qc), gnorm.reshape(1, DV_C)]
    body = functools.partial(_gla_kernel, has_s0=s0 is not None)
    return _recurrence_call(body, in_specs, args, s0, bsz, seq, H_C, DK_C, DV_C, tt)


def _ffn_kernel(*refs, seq_rows, tiles_per_seq, has_cache, final_norm, jobs):
    refs = list(refs)
    h_ref, nw_ref, wa_ref, wu_ref, cw_ref, cb_ref, wd_ref = refs[:7]
    pos = 7
    cache_ref = None
    nf_ref = None
    if has_cache:
        cache_ref = refs[pos]
        pos += 1
    if final_norm:
        nf_ref = refs[pos]
        pos += 1
    nj = len(jobs)
    src_refs = refs[pos:pos + nj]
    out_ref, newc_ref = refs[pos + nj:pos + nj + 2]
    dst_refs = refs[pos + nj + 2:pos + 2 * nj + 2]
    hn_ref, carry_ref = refs[pos + 2 * nj + 2:]
    _run_cast_jobs(jobs, src_refs, dst_refs)

    i = pl.program_id(0)
    j = pl.program_id(1)
    tm, tf = h_ref.shape[0], wa_ref.shape[1]

    @pl.when(j == 0)
    def _():
        _rmsnorm_rows_to(h_ref, nw_ref, hn_ref, 128)
        out_ref[...] = h_ref[...]

    if tiles_per_seq > 1:
        @pl.when(i % tiles_per_seq == 0)
        def _():
            carry_ref[j] = jnp.zeros((CONV_W - 1, tf), F32)

    hn = hn_ref[...]
    a = jnp.dot(hn, wa_ref[...], preferred_element_type=F32)
    u = jnp.dot(hn, wu_ref[...], preferred_element_type=F32)

    nseq = tm // seq_rows
    r8 = lax.broadcasted_iota(jnp.int32, (SUBLANES, 1), 0)
    sh1 = pltpu.roll(a, 1, 0)
    sh2 = pltpu.roll(a, 2, 0)
    a1_parts, a2_parts = [], []
    for s in range(nseq):
        if tiles_per_seq > 1:
            p0, p1 = carry_ref[j, 0:1, :], carry_ref[j, 1:2, :]
        elif has_cache:
            p0, p1 = cache_ref[s, 0:1, :], cache_ref[s, 1:2, :]
        else:
            p0 = p1 = jnp.zeros((1, tf), F32)
        lo, hi = s * seq_rows, (s + 1) * seq_rows
        head = slice(lo, lo + SUBLANES)
        a1_parts += [jnp.where(r8 == 0, p1, sh1[head]), sh1[lo + SUBLANES:hi]]
        a2_parts += [jnp.where(r8 == 0, p0, jnp.where(r8 == 1, p1, sh2[head])),
                     sh2[lo + SUBLANES:hi]]
    a1 = jnp.concatenate(a1_parts, axis=0)
    a2 = jnp.concatenate(a2_parts, axis=0)
    cw = cw_ref[...]
    ac = cb_ref[...] + a2 * cw[0:1, :] + a1 * cw[1:2, :] + a * cw[2:3, :]

    if tiles_per_seq > 1:
        carry_ref[j] = a[tm - (CONV_W - 1):, :]
    for s in range(nseq):
        newc_ref[s] = a[(s + 1) * seq_rows - (CONV_W - 1):(s + 1) * seq_rows, :]

    y = (_silu(ac) * u).astype(BF16)
    out_ref[...] += jnp.dot(y, wd_ref[...], preferred_element_type=F32)

    if final_norm:
        @pl.when(j == pl.num_programs(1) - 1)
        def _():
            rows = min(128, tm)

            def body(c, carry):
                r = pl.multiple_of(c * rows, rows)
                x = out_ref[pl.ds(r, rows), :]
                ms = jnp.mean(x * x, axis=-1, keepdims=True)
                out_ref[pl.ds(r, rows), :] = (x * lax.rsqrt(ms + EPS)) * nf_ref[...]
                return carry

            lax.fori_loop(0, tm // rows, body, 0)


def _ffn(h, nw, wa, wu, cw, cb, wd, cache, nf, jobs=(), *, bsz, seq, tm, tf):
    m, d = h.shape
    dffp = wa.shape[1]
    nj = dffp // tf
    jobs = list(jobs)
    if seq >= tm:
        tiles_per_seq = seq // tm
        seq_rows = tm
        seqs_per_tile = 1
    else:
        tiles_per_seq = 1
        seq_rows = seq
        seqs_per_tile = tm // seq
    assert seq_rows & (seq_rows - 1) == 0
    grid = (m // tm, nj)
    in_specs = [
        pl.BlockSpec((tm, d), lambda i, j: (i, 0)),
        pl.BlockSpec((1, d), lambda i, j: (0, 0)),
        pl.BlockSpec((d, tf), lambda i, j: (0, j)),
        pl.BlockSpec((d, tf), lambda i, j: (0, j)),
        pl.BlockSpec((CONV_W, tf), lambda i, j: (0, j)),
        pl.BlockSpec((1, tf), lambda i, j: (0, j)),
        pl.BlockSpec((tf, d), lambda i, j: (j, 0)),
    ]
    args = [h, nw.reshape(1, d), wa, wu, cw, cb, wd]
    if cache is not None:
        assert tiles_per_seq == 1
        in_specs.append(pl.BlockSpec((seqs_per_tile, CONV_W - 1, tf), lambda i, j: (i, 0, j)))
        args.append(cache)
    if nf is not None:
        in_specs.append(pl.BlockSpec((1, d), lambda i, j: (0, 0)))
        args.append(nf.reshape(1, d))
    newc_spec = pl.BlockSpec((seqs_per_tile, CONV_W - 1, tf), lambda i, j: (i, 0, j))
    job_io = [_cast_job_io(job, grid) for job in jobs]
    in_specs += [io[0] for io in job_io]
    args += [job["src"] for job in jobs]
    body = functools.partial(
        _ffn_kernel, seq_rows=seq_rows, tiles_per_seq=tiles_per_seq,
        has_cache=cache is not None, final_norm=nf is not None, jobs=jobs)
    outs = pl.pallas_call(
        body,
        grid=grid,
        in_specs=in_specs,
        out_specs=[pl.BlockSpec((tm, d), lambda i, j: (i, 0)), newc_spec] + [io[1] for io in job_io],
        out_shape=[
            jax.ShapeDtypeStruct((m, d), F32),
            jax.ShapeDtypeStruct((grid[0] * seqs_per_tile, CONV_W - 1, dffp), F32),
        ] + [io[2] for io in job_io],
        scratch_shapes=[
            pltpu.VMEM((tm, d), BF16),
            pltpu.VMEM((nj, CONV_W - 1, tf), F32),
        ],
        compiler_params=_cparams(2),
        name="ffn",
    )(*args)
    out, newc = outs[0], outs[1]
    if tiles_per_seq > 1:
        newc = newc.reshape(bsz, tiles_per_seq, CONV_W - 1, dffp)[:, -1]
    return out, newc, list(outs[2:])


def _pad_to(x, axis, size):
    pad = size - x.shape[axis]
    if pad == 0:
        return x
    widths = [(0, 0)] * x.ndim
    widths[axis] = (0, pad)
    return jnp.pad(x, widths)


def _rotary_tables(pos):
    half = DK_A // 2
    inv = ROPE_BASE ** (-jnp.arange(half, dtype=F32) / half)
    ang = pos.astype(F32)[:, None] * inv[None, :]
    cos, sin = jnp.cos(ang), jnp.sin(ang)
    return jnp.concatenate([cos, cos], axis=-1), jnp.concatenate([-sin, sin], axis=-1)


FFN_TILE = 512


def _in_proj_job(raw, l):
    if l % 2 == 0:
        src = raw["w_in_even"]
        return _cast_job(src, (l // 2,), src.shape[1], src.shape[2], 16)
    src = jnp.swapaxes(raw["w_in_odd"], 1, 2)
    return _transpose_cast_job(src, (l // 2,), src.shape[1], src.shape[2], LANES)


def _initial_weights(raw):
    w = {("in", 0): raw["w_in_even"][0].astype(BF16)}
    depth = raw["norm_mix"].shape[0]
    for l in range(1, depth, 2):
        w[("gk2", l)] = _pad_to(raw["w_gk2"][l // 2], 0, LANES).astype(BF16)
    return w


def _cast_now(job):
    src = job["src"][job["lead"]]
    if job["transpose"]:
        return _pad_to(src.T.astype(BF16), 1, job["n_dst"] * job["rb"])
    x = src[:, job["col"] * job["cin"]:(job["col"] + 1) * job["cin"]].astype(BF16)
    return _pad_to(_pad_to(x, 1, job["cout"]), 0, job["n_dst"] * job["rb"])


def _missing_jobs(w, wanted, n_steps):
    keys, jobs = [], []
    for key, job in wanted:
        if key in w:
            continue
        if job["n_dst"] <= n_steps:
            keys.append(key)
            jobs.append(job)
        else:
            w[key] = _cast_now(job)
    return keys, jobs


def _row_block(rows, rows_out, n_steps):
    fits = [rb for rb in (16, 32, 64, 128) if rows % rb == 0 and rows_out % rb == 0]
    return next((rb for rb in fits if rows_out // rb <= n_steps), fits[-1])


def _layer_weight_jobs(raw, l, dffp, n_steps):
    w_out = raw["w_out_even"] if l % 2 == 0 else raw["w_out_odd"]
    up, down = raw["ffn_w_up"], raw["ffn_w_down"]
    dff = down.shape[1]
    d = up.shape[1]
    rb, rb_down = _row_block(d, d, n_steps), _row_block(dff, dffp, n_steps)
    return [
        (("out", l), _cast_job(w_out, (l // 2,), w_out.shape[1], w_out.shape[2], rb)),
        (("up_a", l), _cast_job(up, (l,), d, dff, rb, col=0, cout=dffp)),
        (("up_u", l), _cast_job(up, (l,), d, dff, rb, col=1, cout=dffp)),
        (("down", l), _cast_job(down, (l,), dff, down.shape[2], rb_down, rows_out=dffp)),
    ]


def _tiles(m, seq):
    return min(1024, m), 1024, min(512, m), 2048, min(512, m), min(512, seq)


def _run_group(x, pos0, s_ret, s_hgrn, s_gla, s_conv, raw, w):
    bsz, seq, d = x.shape
    m = bsz * seq
    depth = raw["norm_mix"].shape[0]
    dff = raw["ffn_w_down"].shape[1]
    dffp = -(-dff // FFN_TILE) * FFN_TILE
    tm, tn, tm_out, tn_out, tm_ffn, tt = _tiles(m, seq)
    cos2, sin2 = _rotary_tables(pos0 + jnp.arange(seq, dtype=jnp.int32))
    conv_w = _pad_to(raw["ffn_conv_w"], 2, dffp)
    conv_b = _pad_to(raw["ffn_conv_b"], 1, dffp)[:, None, :]

    h = x.reshape(m, d)
    new_ret, new_hgrn, new_gla, new_conv = [], [], [], []
    for l in range(depth):
        i = l // 2
        w_in = w[("in", l)]
        n_low = 0 if l % 2 == 0 else LANES
        n = w_in.shape[1] - n_low
        n_steps = (m // tm) * (n // tn)
        keys, jobs = _missing_jobs(w, _layer_weight_jobs(raw, l, dffp, n_steps), n_steps)
        p, low, casts = _norm_matmul(h, raw["norm_mix"][l], w_in, n, n_low, jobs, tm=tm, tn=tn)
        w.update(zip(keys, casts))
        if l % 2 == 0:
            o_a, sr = _retention(p, cos2, sin2, None if s_ret is None else s_ret[i], bsz, seq, tt)
            o_b, sh = _hgrn(p, raw["hgrn_lb"], raw["hgrn_gnorm"][i],
                            None if s_hgrn is None else s_hgrn[i], bsz, seq, tt, l)
            o_list = [o_a, o_b]
            new_ret.append(sr)
            new_hgrn.append(sh)
        else:
            o_c, sg = _gla(p, low, w[("gk2", l)], raw["b_gk2"][i], raw["gla_gnorm"][i],
                           None if s_gla is None else s_gla[i], bsz, seq, tt)
            o_list = [o_c]
            new_gla.append(sg)
        h = _matmul_res(o_list, w[("out", l)], h, tm=tm_out, tn=tn_out)

        n_steps = (m // tm_ffn) * (dffp // FFN_TILE)
        wanted = []
        if l + 1 < depth:
            wanted = [(("in", l + 1), _in_proj_job(raw, l + 1))] + _layer_weight_jobs(raw, l + 1, dffp, n_steps)
        keys, jobs = _missing_jobs(w, wanted, n_steps)
        cache = None if s_conv is None else _pad_to(s_conv[l], 2, dffp)
        h, nc, casts = _ffn(h, raw["norm_ffn"][l], w[("up_a", l)], w[("up_u", l)], conv_w[l], conv_b[l],
                            w[("down", l)], cache, raw["norm_final"] if l == depth - 1 else None, jobs,
                            bsz=bsz, seq=seq, tm=tm_ffn, tf=FFN_TILE)
        w.update(zip(keys, casts))
        new_conv.append(nc[:, :, :dff])
    return (h.reshape(bsz, seq, d), jnp.stack(new_ret), jnp.stack(new_hgrn),
            jnp.stack(new_gla), jnp.stack(new_conv))


def kernel(x_prompt, x_sample, state_ret, state_hgrn, state_gla, cache_ffn_conv, norm_mix, norm_ffn, norm_final, w_in_even, w_out_even, hgrn_lb, hgrn_gnorm, w_in_odd, w_gk2, b_gk2, gla_gnorm, w_out_odd, ffn_w_up, ffn_conv_w, ffn_conv_b, ffn_w_down):
    raw = dict(norm_mix=norm_mix, norm_ffn=norm_ffn, norm_final=norm_final, w_in_even=w_in_even,
               w_out_even=w_out_even, hgrn_lb=hgrn_lb, hgrn_gnorm=hgrn_gnorm, w_in_odd=w_in_odd,
               w_gk2=w_gk2, b_gk2=b_gk2, gla_gnorm=gla_gnorm, w_out_odd=w_out_odd, ffn_w_up=ffn_w_up,
               ffn_conv_w=ffn_conv_w, ffn_conv_b=ffn_conv_b, ffn_w_down=ffn_w_down)
    w = _initial_weights(raw)
    y_p, ret_p, hgrn_p, gla_p, conv_p = _run_group(x_prompt, 0, None, None, None, None, raw, w)
    y_s, ret_s, hgrn_s, gla_s, conv_s = _run_group(
        x_sample, PAST_LEN, state_ret, state_hgrn, state_gla, cache_ffn_conv, raw, w)
    return (y_p, y_s, ret_p, ret_s, hgrn_p, hgrn_s, gla_p, gla_s, conv_p, conv_s)
```

```python
import functools

import numpy as np
import jax
import jax.numpy as jnp
from jax import lax
from jax.experimental import pallas as pl
from jax.experimental.pallas import tpu as pltpu

F32 = jnp.float32
BF16 = jnp.bfloat16

EPS = 1e-6
ROPE_BASE = 10000.0
GATE_NORMALIZER = 16.0
PAST_LEN = 1024

H_A, DK_A, DV_A = 4, 128, 256
H_B, DK_B, DV_B = 8, 128, 128
H_C, DK_C, DV_C = 4, 256, 512
GK_RANK = 16
CONV_W = 3

LANES = 128
SUBLANES = 8
VMEM_LIMIT_BYTES = 56 * 1024 * 1024

CHUNK = 64
SUB = 16
SUB_SHIFT = 4
assert 1 << SUB_SHIFT == SUB and CHUNK % SUB == 0

NT_DIMS = (((1,), (1,)), ((), ()))
TN_DIMS = (((0,), (0,)), ((), ()))


def _cparams(n_axes):
    return pltpu.CompilerParams(
        dimension_semantics=("arbitrary",) * n_axes,
        vmem_limit_bytes=VMEM_LIMIT_BYTES,
    )


def _sigmoid(x):
    return 1.0 / (1.0 + jnp.exp(-x))


def _silu(x):
    return x * _sigmoid(x)


def _rmsnorm_rows_to(x_ref, nw_ref, dst_ref, rows):
    rows = min(rows, x_ref.shape[0])
    assert x_ref.shape[0] % rows == 0
    n = x_ref.shape[0] // rows

    def body(c, carry):
        r = pl.multiple_of(c * rows, rows)
        x = x_ref[pl.ds(r, rows), :]
        ms = jnp.mean(x * x, axis=-1, keepdims=True)
        dst_ref[pl.ds(r, rows), :] = ((x * lax.rsqrt(ms + EPS)) * nw_ref[...]).astype(BF16)
        return carry

    lax.fori_loop(0, n, body, 0)


def _cast_job(src, lead, rows, cin, rb, *, halves=1, cout=None, rows_out=None):
    cout = cin if cout is None else cout
    rows_out = rows if rows_out is None else rows_out
    assert rows % rb == 0 and rows_out % rb == 0
    return dict(src=src, lead=tuple(lead), rb=rb, cin=cin, cout=cout, halves=halves,
                n_src=rows // rb, n_dst=rows_out // rb, transpose=False)


def _transpose_cast_job(src, lead, rows, cin, rb):
    n = -(-rows // rb)
    return dict(src=src, lead=tuple(lead), rb=rb, cin=cin, cout=cin, halves=1, rows=rows,
                n_src=n, n_dst=n, transpose=True)


def _cast_job_io(job, grid):
    assert job["n_dst"] <= grid[0] * grid[1]
    lead = job["lead"]
    rb, cin, cout, n_dst, halves = job["rb"], job["cin"], job["cout"], job["n_dst"], job["halves"]

    def src_index(i, j):
        return lead + (jnp.minimum(i * grid[1] + j, job["n_src"] - 1), 0)

    def dst_index(i, j):
        blk = jnp.minimum(i * grid[1] + j, n_dst - 1)
        if job["transpose"]:
            return (0, blk)
        return (0, blk, 0) if halves > 1 else (blk, 0)

    if job["transpose"]:
        dst_block, dst_shape = (cin, rb), (cin, n_dst * rb)
    elif halves > 1:
        dst_block, dst_shape = (halves, rb, cout), (halves, n_dst * rb, cout)
    else:
        dst_block, dst_shape = (rb, cout), (n_dst * rb, cout)
    return (pl.BlockSpec((None,) * len(lead) + (rb, halves * cin), src_index),
            pl.BlockSpec(dst_block, dst_index),
            jax.ShapeDtypeStruct(dst_shape, BF16))


def _run_cast_jobs(jobs, src_refs, dst_refs):
    step = pl.program_id(0) * pl.num_programs(1) + pl.program_id(1)
    for job, src_ref, dst_ref in zip(jobs, src_refs, dst_refs):
        if job["transpose"]:
            first_row = jnp.minimum(step, job["n_src"] - 1) * job["rb"]
            row = lax.broadcasted_iota(jnp.int32, (job["rb"], 1), 0)
            x = jnp.where(row < job["rows"] - first_row, src_ref[...], 0.0)
            dst_ref[...] = jnp.transpose(x).astype(BF16)
            continue
        cin, cout, halves = job["cin"], job["cout"], job["halves"]
        for s in range(halves):
            y = src_ref[:, s * cin:(s + 1) * cin].astype(BF16)
            if job["n_dst"] > job["n_src"]:
                y = jnp.where(step < job["n_src"], y, jnp.zeros_like(y))
            dst = dst_ref.at[s] if halves > 1 else dst_ref
            if cout > cin:
                dst[:, :cin] = y
                dst[:, cin:] = jnp.zeros((job["rb"], cout - cin), BF16)
            else:
                dst[...] = y


def _norm_matmul_kernel(*refs, has_low, jobs):
    n_in = 3 + has_low
    n_out = 1 + has_low
    nj = len(jobs)
    x_ref, nw_ref, w_ref = refs[:3]
    src_refs = refs[n_in:n_in + nj]
    o_ref = refs[n_in + nj]
    dst_refs = refs[n_in + nj + n_out:n_in + nj + n_out + nj]
    hn_ref = refs[-1]

    @pl.when(pl.program_id(1) == 0)
    def _():
        _rmsnorm_rows_to(x_ref, nw_ref, hn_ref, 128)
        if has_low:
            low_ref = refs[n_in + nj + 1]
            low_ref[...] = jnp.dot(hn_ref[...], refs[3][...], preferred_element_type=F32)

    o_ref[...] = jnp.dot(hn_ref[...], w_ref[...], preferred_element_type=F32)
    _run_cast_jobs(jobs, src_refs, dst_refs)


def _norm_matmul(x, nw, w, n, n_low=0, jobs=(), *, tm, tn):
    m, d = x.shape
    assert n % tn == 0 and n + n_low <= w.shape[1]
    w_low = w if n_low else None
    grid = (m // tm, n // tn)
    jobs = list(jobs)
    job_io = [_cast_job_io(job, grid) for job in jobs]
    in_specs = [
        pl.BlockSpec((tm, d), lambda i, j: (i, 0)),
        pl.BlockSpec((1, d), lambda i, j: (0, 0)),
        pl.BlockSpec((d, tn), lambda i, j: (0, j)),
    ]
    out_shape = [jax.ShapeDtypeStruct((m, n), F32)]
    out_specs = [pl.BlockSpec((tm, tn), lambda i, j: (i, j))]
    args = [x, nw.reshape(1, d), w]
    if w_low is not None:
        nl = n_low
        assert n % nl == 0
        in_specs.append(pl.BlockSpec((d, nl), lambda i, j: (0, n // nl)))
        out_shape.append(jax.ShapeDtypeStruct((m, nl), F32))
        out_specs.append(pl.BlockSpec((tm, nl), lambda i, j: (i, 0)))
        args.append(w_low)
    in_specs += [io[0] for io in job_io]
    out_specs += [io[1] for io in job_io]
    out_shape += [io[2] for io in job_io]
    args += [job["src"] for job in jobs]
    outs = pl.pallas_call(
        functools.partial(_norm_matmul_kernel, has_low=w_low is not None, jobs=jobs),
        grid=grid,
        in_specs=in_specs,
        out_specs=out_specs,
        out_shape=out_shape,
        scratch_shapes=[pltpu.VMEM((tm, d), BF16)],
        compiler_params=_cparams(2),
        name="norm_matmul",
    )(*args)
    n_main = 1 + (w_low is not None)
    return outs[0], (outs[1] if w_low is not None else None), list(outs[n_main:])


def _matmul_res_kernel(*refs, n_in):
    a_refs = refs[:n_in]
    w_refs = refs[n_in:2 * n_in]
    res_ref = refs[2 * n_in]
    o_ref = refs[2 * n_in + 1]
    acc = res_ref[...]
    for a_ref, w_ref in zip(a_refs, w_refs):
        acc = acc + jnp.dot(a_ref[...], w_ref[...], preferred_element_type=F32)
    o_ref[...] = acc


def _matmul_res(a_list, w, res, *, tm, tn):
    m, n = res.shape
    ka = a_list[0].shape[1]
    assert all(a.shape[1] == ka for a in a_list) and ka * len(a_list) == w.shape[0]
    grid = (m // tm, n // tn)
    in_specs = [pl.BlockSpec((tm, ka), lambda i, j: (i, 0)) for _ in a_list]
    for rb in range(len(a_list)):
        in_specs.append(pl.BlockSpec((ka, tn), lambda i, j, rb=rb: (rb, j)))
    in_specs.append(pl.BlockSpec((tm, tn), lambda i, j: (i, j)))
    return pl.pallas_call(
        functools.partial(_matmul_res_kernel, n_in=len(a_list)),
        grid=grid,
        in_specs=in_specs,
        out_specs=pl.BlockSpec((tm, tn), lambda i, j: (i, j)),
        out_shape=jax.ShapeDtypeStruct((m, n), F32),
        compiler_params=_cparams(2),
        name="matmul_res",
    )(*a_list, *([w] * len(a_list)), res)


def _local_cumsum(lg):
    pos = lax.broadcasted_iota(jnp.int32, lg.shape, 0) & (SUB - 1)
    x = lg
    s = 1
    while s < SUB:
        x = x + jnp.where(pos >= s, pltpu.roll(x, s, 0), 0.0)
        s *= 2
    return x


def _chunk_operands(q, k, bl, tot):
    ns = q.shape[0] // SUB
    a = [jnp.zeros_like(tot[0])]
    for m in range(ns):
        a.append(a[-1] + tot[m])
    b_last = a[ns]

    qt, kbar, khat, qin, kout = [], [], [], [], []
    qd = {d: [] for d in range(2, ns)}
    for m in range(ns):
        sl = slice(m * SUB, (m + 1) * SUB)
        blm = bl[sl]
        qtm = q[sl] * jnp.exp(blm)
        km = k[sl]
        khm = km * jnp.exp(tot[m] - blm)
        qt.append(qtm)
        kbar.append(km * jnp.exp(-blm))
        khat.append(khm)
        qin.append(qtm * jnp.exp(a[m]))
        kout.append(khm * jnp.exp(b_last - a[m + 1]))
        for d in range(2, ns):
            if m >= d:
                qd[d].append(qtm * jnp.exp(a[m] - a[m - d + 1]))
            else:
                qd[d].append(jnp.zeros_like(qtm))

    def cat(xs):
        return jnp.concatenate(xs, axis=0).astype(BF16)

    return dict(qt=cat(qt), kbar=cat(kbar), khat=cat(khat), qin=cat(qin), kout=cat(kout),
                qd=[cat(qd[d]) for d in range(2, ns)], decay=jnp.exp(b_last))


def _chunk_steps(heads, st_ref):
    c = heads[0][0].shape[0]
    ns = c // SUB
    ops = [_chunk_operands(q, k, bl, tot) for (q, k, _, bl, tot) in heads]
    raw = []
    for op in ops:
        s = [lax.dot_general(op["qt"], op["kbar"], NT_DIMS, preferred_element_type=F32),
             lax.dot_general(op["qt"], op["khat"], NT_DIMS, preferred_element_type=F32)]
        s += [lax.dot_general(qd, op["khat"], NT_DIMS, preferred_element_type=F32) for qd in op["qd"]]
        raw.append(s)
    ri = lax.broadcasted_iota(jnp.int32, (c, c), 0)
    ci = lax.broadcasted_iota(jnp.int32, (c, c), 1)
    dist = (ri >> SUB_SHIFT) - (ci >> SUB_SHIFT)
    outs = []
    for h, (op, s) in enumerate(zip(ops, raw)):
        scores = jnp.where((dist == 0) & (ri >= ci), s[0], jnp.where(dist == 1, s[1], 0.0))
        for d in range(2, ns):
            scores = jnp.where(dist == d, s[d], scores)
        vb = heads[h][2]
        o = jnp.dot(scores.astype(BF16), vb, preferred_element_type=F32)
        o = o + lax.dot_general(op["qin"], st_ref[h].astype(BF16), NT_DIMS,
                                preferred_element_type=F32)
        outs.append(o)
    for h, op in enumerate(ops):
        st_ref[h] = st_ref[h] * op["decay"] + lax.dot_general(
            heads[h][2], op["kout"], TN_DIMS, preferred_element_type=F32)
    return outs


def _sub_totals(bl):
    return [bl[(m + 1) * SUB - 1:(m + 1) * SUB, :] for m in range(bl.shape[0] // SUB)]


def _state_prologue(s0_ref, st_ref):
    @pl.when(pl.program_id(1) == 0)
    def _():
        for h in range(st_ref.shape[0]):
            if s0_ref is None:
                st_ref[h] = jnp.zeros(st_ref.shape[1:], F32)
            else:
                st_ref[h] = jnp.transpose(s0_ref[0, h])


def _state_epilogue(sn_ref, st_ref):
    @pl.when(pl.program_id(1) == pl.num_programs(1) - 1)
    def _():
        for h in range(st_ref.shape[0]):
            sn_ref[0, h] = jnp.transpose(st_ref[h])


def _chunk_loop(n_chunks, body, unroll):
    def step(c, carry):
        body(pl.ds(pl.multiple_of(c * CHUNK, CHUNK), CHUNK))
        return carry

    lax.fori_loop(0, n_chunks, step, 0, unroll=min(unroll, n_chunks))


def _head(ref, rows, h, width):
    return ref[rows, h * width:(h + 1) * width]


def _ret_chunk(get, put, cos, sin, st_ref):
    pos1 = ((lax.broadcasted_iota(jnp.int32, (CHUNK, 1), 0) & (SUB - 1)) + 1).astype(F32)
    half = DK_A // 2
    heads = []
    for h in range(H_A):
        lgam = float(np.log1p(-np.exp2(-5.0 - h)))
        bl = pos1 * lgam
        tot = [jnp.full((1, 1), lgam * SUB, F32)] * (CHUNK // SUB)
        qr, kr = get("q", h), get("k", h)
        q = qr * cos + pltpu.roll(qr, half, 1) * sin
        k = (kr * cos + pltpu.roll(kr, half, 1) * sin) * (DK_A ** -0.5)
        heads.append((q, k, get("v", h).astype(BF16), bl, tot))
    for h, o in enumerate(_chunk_steps(heads, st_ref)):
        mu = jnp.mean(o, axis=-1, keepdims=True)
        oc = o - mu
        var = jnp.mean(oc * oc, axis=-1, keepdims=True)
        put(h, ((oc * lax.rsqrt(var + EPS)) * _silu(get("g", h))).astype(BF16))


def _hgrn_lower_bound(lbp_ref, layer):
    lbp = lbp_ref[...]
    e = jnp.exp(lbp - jnp.max(lbp, axis=0, keepdims=True))
    sm = e / jnp.sum(e, axis=0, keepdims=True)
    return jnp.sum(sm[:layer + 1], axis=0, keepdims=True)


def _hgrn_chunk(get, put, lb_all, gn, st_ref):
    heads = []
    for h in range(H_B):
        lb = lb_all[:, h * DK_B:(h + 1) * DK_B]
        q = _silu(get("q", h)) * (DK_B ** -0.5)
        f = lb + (1.0 - lb) * _sigmoid(get("f", h))
        bl = _local_cumsum(jnp.log(f))
        heads.append((q, 1.0 - f, get("i", h).astype(BF16), bl, _sub_totals(bl)))
    for h, o in enumerate(_chunk_steps(heads, st_ref)):
        ms = jnp.mean(o * o, axis=-1, keepdims=True)
        put(h, (((o * lax.rsqrt(ms + EPS)) * gn) * _silu(get("g", h))).astype(BF16))


def _gla_chunk(get, put, low, wgk, bgk, gn, st_ref):
    x = jnp.dot(low.astype(BF16), wgk, preferred_element_type=F32) + bgk
    lg_all = (jnp.minimum(x, 0.0) - jnp.log1p(jnp.exp(-jnp.abs(x)))) * (1.0 / GATE_NORMALIZER)
    heads = []
    for h in range(H_C):
        bl = _local_cumsum(lg_all[:, h * DK_C:(h + 1) * DK_C])
        q = get("q", h) * (DK_C ** -0.5)
        heads.append((q, get("k", h), get("v", h).astype(BF16), bl, _sub_totals(bl)))
    for h, o in enumerate(_chunk_steps(heads, st_ref)):
        ms = jnp.mean(o * o, axis=-1, keepdims=True)
        put(h, (((o * lax.rsqrt(ms + EPS)) * gn) * _silu(get("g", h))).astype(BF16))


def _ref_getter(refs, widths, rows):
    return lambda name, h: _head(refs[name], rows, h, widths[name])


def _ref_putter(o_ref, width, rows):
    def put(h, x):
        o_ref[rows, h * width:(h + 1) * width] = x
    return put


def _ret_kernel(*refs, has_s0):
    if has_s0:
        q_ref, k_ref, v_ref, g_ref, cos_ref, sin_ref, s0_ref, o_ref, sn_ref, st_ref = refs
    else:
        q_ref, k_ref, v_ref, g_ref, cos_ref, sin_ref, o_ref, sn_ref, st_ref = refs
        s0_ref = None
    _state_prologue(s0_ref, st_ref)
    srcs = {"q": q_ref, "k": k_ref, "v": v_ref, "g": g_ref}
    widths = {"q": DK_A, "k": DK_A, "v": DV_A, "g": DV_A}

    def body(rows):
        _ret_chunk(_ref_getter(srcs, widths, rows), _ref_putter(o_ref, DV_A, rows),
                   cos_ref[rows, :], sin_ref[rows, :], st_ref)

    _chunk_loop(q_ref.shape[0] // CHUNK, body, 2)
    _state_epilogue(sn_ref, st_ref)


def _hgrn_kernel(*refs, has_s0, layer):
    if has_s0:
        q_ref, f_ref, i_ref, g_ref, lbp_ref, gn_ref, s0_ref, o_ref, sn_ref, st_ref = refs
    else:
        q_ref, f_ref, i_ref, g_ref, lbp_ref, gn_ref, o_ref, sn_ref, st_ref = refs
        s0_ref = None
    _state_prologue(s0_ref, st_ref)
    lb_all = _hgrn_lower_bound(lbp_ref, layer)
    gn = gn_ref[...]
    srcs = {"q": q_ref, "f": f_ref, "i": i_ref, "g": g_ref}
    widths = {"q": DK_B, "f": DK_B, "i": DV_B, "g": DV_B}

    def body(rows):
        _hgrn_chunk(_ref_getter(srcs, widths, rows), _ref_putter(o_ref, DV_B, rows),
                    lb_all, gn, st_ref)

    _chunk_loop(q_ref.shape[0] // CHUNK, body, 2)
    _state_epilogue(sn_ref, st_ref)


def _gla_kernel(*refs, has_s0):
    if has_s0:
        q_ref, k_ref, v_ref, g_ref, low_ref, wgk_ref, bgk_ref, gn_ref, s0_ref, o_ref, sn_ref, st_ref = refs
    else:
        q_ref, k_ref, v_ref, g_ref, low_ref, wgk_ref, bgk_ref, gn_ref, o_ref, sn_ref, st_ref = refs
        s0_ref = None
    _state_prologue(s0_ref, st_ref)
    gn = gn_ref[...]
    bgk = bgk_ref[...]
    srcs = {"q": q_ref, "k": k_ref, "v": v_ref, "g": g_ref}
    widths = {"q": DK_C, "k": DK_C, "v": DV_C, "g": DV_C}

    def body(rows):
        _gla_chunk(_ref_getter(srcs, widths, rows), _ref_putter(o_ref, DV_C, rows),
                   low_ref[rows, :], wgk_ref[...], bgk, gn, st_ref)

    _chunk_loop(q_ref.shape[0] // CHUNK, body, 2)
    _state_epilogue(sn_ref, st_ref)


def _row_spec(tt, width, nt, block):
    return pl.BlockSpec((tt, width), lambda b, t: (b * nt + t, block))


def _recurrence_call(body, in_specs, args, s0, bsz, seq, heads, dk, dv, tt):
    nt = seq // tt
    if s0 is not None:
        in_specs = in_specs + [pl.BlockSpec((1, heads, dk, dv), lambda b, t: (b, 0, 0, 0))]
        args = args + [s0]
    o, sn = pl.pallas_call(
        body,
        grid=(bsz, nt),
        in_specs=in_specs,
        out_specs=[
            pl.BlockSpec((tt, heads * dv), lambda b, t: (b * nt + t, 0)),
            pl.BlockSpec((1, heads, dk, dv), lambda b, t: (b, 0, 0, 0)),
        ],
        out_shape=[
            jax.ShapeDtypeStruct((bsz * seq, heads * dv), BF16),
            jax.ShapeDtypeStruct((bsz, heads, dk, dv), F32),
        ],
        scratch_shapes=[pltpu.VMEM((heads, dv, dk), F32)],
        compiler_params=_cparams(2),
        name="recurrence",
    )(*args)
    return o, sn


def _retention(p, cos2, sin2, s0, bsz, seq, tt):
    nt = seq // tt
    qa, va = H_A * DK_A, H_A * DV_A
    in_specs = [
        _row_spec(tt, qa, nt, 0),
        _row_spec(tt, qa, nt, 1),
        _row_spec(tt, va, nt, 2 * qa // va),
        _row_spec(tt, va, nt, (2 * qa + va) // va),
        pl.BlockSpec((tt, DK_A), lambda b, t: (t, 0)),
        pl.BlockSpec((tt, DK_A), lambda b, t: (t, 0)),
    ]
    args = [p, p, p, p, cos2, sin2]
    body = functools.partial(_ret_kernel, has_s0=s0 is not None)
    return _recurrence_call(body, in_specs, args, s0, bsz, seq, H_A, DK_A, DV_A, tt)


def _hgrn(p, lbp, gnorm, s0, bsz, seq, tt, layer):
    nt = seq // tt
    base = 2 * H_A * DK_A + 2 * H_A * DV_A
    qb = H_B * DK_B
    assert base % qb == 0 and H_B * DV_B == qb
    in_specs = [_row_spec(tt, qb, nt, base // qb + n) for n in range(4)]
    in_specs += [
        pl.BlockSpec(lbp.shape, lambda b, t: (0, 0)),
        pl.BlockSpec((1, DV_B), lambda b, t: (0, 0)),
    ]
    args = [p, p, p, p, lbp, gnorm.reshape(1, DV_B)]
    body = functools.partial(_hgrn_kernel, has_s0=s0 is not None, layer=layer)
    return _recurrence_call(body, in_specs, args, s0, bsz, seq, H_B, DK_B, DV_B, tt)


def _gla(p, low, wgk, bgk, gnorm, s0, bsz, seq, tt):
    nt = seq // tt
    qc, vc = H_C * DK_C, H_C * DV_C
    in_specs = [
        _row_spec(tt, qc, nt, 0),
        _row_spec(tt, qc, nt, 1),
        _row_spec(tt, vc, nt, 2 * qc // vc),
        _row_spec(tt, vc, nt, (2 * qc + vc) // vc),
        _row_spec(tt, low.shape[1], nt, 0),
        pl.BlockSpec(wgk.shape, lambda b, t: (0, 0)),
        pl.BlockSpec((1, qc), lambda b, t: (0, 0)),
        pl.BlockSpec((1, DV_C), lambda b, t: (0, 0)),
    ]
    args = [p, p, p, p, low, wgk, bgk.reshape(1, qc), gnorm.reshape(1, DV_C)]
    body = functools.partial(_gla_kernel, has_s0=s0 is not None)
    return _recurrence_call(body, in_specs, args, s0, bsz, seq, H_C, DK_C, DV_C, tt)


def _ffn_kernel(*refs, seq_rows, tiles_per_seq, has_cache, final_norm, jobs):
    refs = list(refs)
    h_ref, nw_ref, up_ref, conv_ref, wd_ref = refs[:5]
    pos = 5
    cache_ref = None
    nf_ref = None
    if has_cache:
        cache_ref = refs[pos]
        pos += 1
    if final_norm:
        nf_ref = refs[pos]
        pos += 1
    nj = len(jobs)
    src_refs = refs[pos:pos + nj]
    out_ref, newc_ref = refs[pos + nj:pos + nj + 2]
    dst_refs = refs[pos + nj + 2:pos + 2 * nj + 2]
    hn_ref, carry_ref = refs[pos + 2 * nj + 2:]

    i = pl.program_id(0)
    j = pl.program_id(1)
    tm, tf = h_ref.shape[0], wd_ref.shape[0]

    @pl.when(j == 0)
    def _():
        _rmsnorm_rows_to(h_ref, nw_ref, hn_ref, 128)
        out_ref[...] = h_ref[...]

    if tiles_per_seq > 1:
        @pl.when(i % tiles_per_seq == 0)
        def _():
            carry_ref[j] = jnp.zeros((CONV_W - 1, tf), F32)

    hn = hn_ref[...]
    a = jnp.dot(hn, up_ref[0], preferred_element_type=F32)
    u = jnp.dot(hn, up_ref[1], preferred_element_type=F32)

    nseq = tm // seq_rows
    r8 = lax.broadcasted_iota(jnp.int32, (SUBLANES, 1), 0)
    sh1 = pltpu.roll(a, 1, 0)
    sh2 = pltpu.roll(a, 2, 0)
    a1_parts, a2_parts = [], []
    for s in range(nseq):
        if tiles_per_seq > 1:
            p0, p1 = carry_ref[j, 0:1, :], carry_ref[j, 1:2, :]
        elif has_cache:
            p0, p1 = cache_ref[s, 0:1, :], cache_ref[s, 1:2, :]
        else:
            p0 = p1 = jnp.zeros((1, tf), F32)
        lo, hi = s * seq_rows, (s + 1) * seq_rows
        head = slice(lo, lo + SUBLANES)
        a1_parts += [jnp.where(r8 == 0, p1, sh1[head]), sh1[lo + SUBLANES:hi]]
        a2_parts += [jnp.where(r8 == 0, p0, jnp.where(r8 == 1, p1, sh2[head])),
                     sh2[lo + SUBLANES:hi]]
    a1 = jnp.concatenate(a1_parts, axis=0)
    a2 = jnp.concatenate(a2_parts, axis=0)
    cw = conv_ref[...]
    ac = cw[CONV_W:CONV_W + 1, :] + a2 * cw[0:1, :] + a1 * cw[1:2, :] + a * cw[2:3, :]

    if tiles_per_seq > 1:
        carry_ref[j] = a[tm - (CONV_W - 1):, :]
    for s in range(nseq):
        newc_ref[s] = a[(s + 1) * seq_rows - (CONV_W - 1):(s + 1) * seq_rows, :]

    y = (_silu(ac) * u).astype(BF16)
    out_ref[...] += jnp.dot(y, wd_ref[...], preferred_element_type=F32)
    _run_cast_jobs(jobs, src_refs, dst_refs)

    if final_norm:
        @pl.when(j == pl.num_programs(1) - 1)
        def _():
            rows = min(128, tm)

            def body(c, carry):
                r = pl.multiple_of(c * rows, rows)
                x = out_ref[pl.ds(r, rows), :]
                ms = jnp.mean(x * x, axis=-1, keepdims=True)
                out_ref[pl.ds(r, rows), :] = (x * lax.rsqrt(ms + EPS)) * nf_ref[...]
                return carry

            lax.fori_loop(0, tm // rows, body, 0)


def _ffn(h, nw, up, conv, wd, cache, nf, jobs=(), *, bsz, seq, tm, tf):
    m, d = h.shape
    dffp = up.shape[2]
    nj = dffp // tf
    jobs = list(jobs)
    if seq >= tm:
        tiles_per_seq = seq // tm
        seq_rows = tm
        seqs_per_tile = 1
    else:
        tiles_per_seq = 1
        seq_rows = seq
        seqs_per_tile = tm // seq
    assert seq_rows & (seq_rows - 1) == 0
    grid = (m // tm, nj)
    in_specs = [
        pl.BlockSpec((tm, d), lambda i, j: (i, 0)),
        pl.BlockSpec((1, d), lambda i, j: (0, 0)),
        pl.BlockSpec((2, d, tf), lambda i, j: (0, 0, j)),
        pl.BlockSpec((CONV_W + 1, tf), lambda i, j: (0, j)),
        pl.BlockSpec((tf, d), lambda i, j: (j, 0)),
    ]
    args = [h, nw.reshape(1, d), up, conv, wd]
    if cache is not None:
        assert tiles_per_seq == 1
        in_specs.append(pl.BlockSpec((seqs_per_tile, CONV_W - 1, tf), lambda i, j: (i, 0, j)))
        args.append(cache)
    if nf is not None:
        in_specs.append(pl.BlockSpec((1, d), lambda i, j: (0, 0)))
        args.append(nf.reshape(1, d))
    newc_spec = pl.BlockSpec((seqs_per_tile, CONV_W - 1, tf), lambda i, j: (i, 0, j))
    job_io = [_cast_job_io(job, grid) for job in jobs]
    in_specs += [io[0] for io in job_io]
    args += [job["src"] for job in jobs]
    body = functools.partial(
        _ffn_kernel, seq_rows=seq_rows, tiles_per_seq=tiles_per_seq,
        has_cache=cache is not None, final_norm=nf is not None, jobs=jobs)
    outs = pl.pallas_call(
        body,
        grid=grid,
        in_specs=in_specs,
        out_specs=[pl.BlockSpec((tm, d), lambda i, j: (i, 0)), newc_spec] + [io[1] for io in job_io],
        out_shape=[
            jax.ShapeDtypeStruct((m, d), F32),
            jax.ShapeDtypeStruct((grid[0] * seqs_per_tile, CONV_W - 1, dffp), F32),
        ] + [io[2] for io in job_io],
        scratch_shapes=[
            pltpu.VMEM((tm, d), BF16),
            pltpu.VMEM((nj, CONV_W - 1, tf), F32),
        ],
        compiler_params=_cparams(2),
        name="ffn",
    )(*args)
    out, newc = outs[0], outs[1]
    if tiles_per_seq > 1:
        newc = newc.reshape(bsz, tiles_per_seq, CONV_W - 1, dffp)[:, -1]
    return out, newc, list(outs[2:])


def _pad_to(x, axis, size):
    pad = size - x.shape[axis]
    if pad == 0:
        return x
    widths = [(0, 0)] * x.ndim
    widths[axis] = (0, pad)
    return jnp.pad(x, widths)


def _rotary_tables(pos):
    half = DK_A // 2
    inv = ROPE_BASE ** (-jnp.arange(half, dtype=F32) / half)
    ang = pos.astype(F32)[:, None] * inv[None, :]
    cos, sin = jnp.cos(ang), jnp.sin(ang)
    return jnp.concatenate([cos, cos], axis=-1), jnp.concatenate([-sin, sin], axis=-1)


FFN_TILE = 512


def _in_proj_job(raw, l):
    if l % 2 == 0:
        src = raw["w_in_even"]
        return _cast_job(src, (l // 2,), src.shape[1], src.shape[2], 16)
    src = jnp.swapaxes(raw["w_in_odd"], 1, 2)
    return _transpose_cast_job(src, (l // 2,), src.shape[1], src.shape[2], LANES)


def _initial_weights(raw):
    w = {("in", 0): raw["w_in_even"][0].astype(BF16)}
    depth = raw["norm_mix"].shape[0]
    for l in range(1, depth, 2):
        w[("gk2", l)] = _pad_to(raw["w_gk2"][l // 2], 0, LANES).astype(BF16)
    return w


def _cast_now(job):
    src = job["src"][job["lead"]]
    if job["transpose"]:
        return _pad_to(src.T.astype(BF16), 1, job["n_dst"] * job["rb"])
    rows_out = job["n_dst"] * job["rb"]
    parts = [_pad_to(_pad_to(src[:, s * job["cin"]:(s + 1) * job["cin"]].astype(BF16), 1, job["cout"]),
                     0, rows_out) for s in range(job["halves"])]
    return jnp.stack(parts) if job["halves"] > 1 else parts[0]


def _missing_jobs(w, wanted, n_steps):
    keys, jobs = [], []
    for key, job in wanted:
        if key in w:
            continue
        if job["n_dst"] <= n_steps:
            keys.append(key)
            jobs.append(job)
        else:
            w[key] = _cast_now(job)
    return keys, jobs


def _row_block(rows, rows_out, n_steps):
    fits = [rb for rb in (16, 32, 64, 128) if rows % rb == 0 and rows_out % rb == 0]
    return next((rb for rb in fits if rows_out // rb <= n_steps), fits[-1])


def _layer_weight_jobs(raw, l, dffp, n_steps):
    w_out = raw["w_out_even"] if l % 2 == 0 else raw["w_out_odd"]
    up, down = raw["ffn_w_up"], raw["ffn_w_down"]
    dff = down.shape[1]
    d = up.shape[1]
    rb, rb_down = _row_block(d, d, n_steps), _row_block(dff, dffp, n_steps)
    return [
        (("out", l), _cast_job(w_out, (l // 2,), w_out.shape[1], w_out.shape[2], rb)),
        (("up", l), _cast_job(up, (l,), d, dff, rb, halves=2, cout=dffp)),
        (("down", l), _cast_job(down, (l,), dff, down.shape[2], rb_down, rows_out=dffp)),
    ]


def _tiles(m, seq):
    return min(1024, m), 1024, min(512, m), 2048, min(512, m), min(512, seq)


def _run_group(x, pos0, s_ret, s_hgrn, s_gla, s_conv, raw, w):
    bsz, seq, d = x.shape
    m = bsz * seq
    depth = raw["norm_mix"].shape[0]
    dff = raw["ffn_w_down"].shape[1]
    dffp = -(-dff // FFN_TILE) * FFN_TILE
    tm, tn, tm_out, tn_out, tm_ffn, tt = _tiles(m, seq)
    cos2, sin2 = _rotary_tables(pos0 + jnp.arange(seq, dtype=jnp.int32))
    conv = _pad_to(jnp.concatenate([raw["ffn_conv_w"], raw["ffn_conv_b"][:, None, :]], axis=1), 2, dffp)

    h = x.reshape(m, d)
    new_ret, new_hgrn, new_gla, new_conv = [], [], [], []
    for l in range(depth):
        i = l // 2
        w_in = w[("in", l)]
        n_low = 0 if l % 2 == 0 else LANES
        n = w_in.shape[1] - n_low
        n_steps = (m // tm) * (n // tn)
        keys, jobs = _missing_jobs(w, _layer_weight_jobs(raw, l, dffp, n_steps), n_steps)
        p, low, casts = _norm_matmul(h, raw["norm_mix"][l], w_in, n, n_low, jobs, tm=tm, tn=tn)
        w.update(zip(keys, casts))
        if l % 2 == 0:
            o_a, sr = _retention(p, cos2, sin2, None if s_ret is None else s_ret[i], bsz, seq, tt)
            o_b, sh = _hgrn(p, raw["hgrn_lb"], raw["hgrn_gnorm"][i],
                            None if s_hgrn is None else s_hgrn[i], bsz, seq, tt, l)
            o_list = [o_a, o_b]
            new_ret.append(sr)
            new_hgrn.append(sh)
        else:
            o_c, sg = _gla(p, low, w[("gk2", l)], raw["b_gk2"][i], raw["gla_gnorm"][i],
                           None if s_gla is None else s_gla[i], bsz, seq, tt)
            o_list = [o_c]
            new_gla.append(sg)
        h = _matmul_res(o_list, w[("out", l)], h, tm=tm_out, tn=tn_out)

        n_steps = (m // tm_ffn) * (dffp // FFN_TILE)
        wanted = []
        if l + 1 < depth:
            wanted = [(("in", l + 1), _in_proj_job(raw, l + 1))] + _layer_weight_jobs(raw, l + 1, dffp, n_steps)
        keys, jobs = _missing_jobs(w, wanted, n_steps)
        cache = None if s_conv is None else _pad_to(s_conv[l], 2, dffp)
        h, nc, casts = _ffn(h, raw["norm_ffn"][l], w[("up", l)], conv[l], w[("down", l)], cache,
                            raw["norm_final"] if l == depth - 1 else None, jobs,
                            bsz=bsz, seq=seq, tm=tm_ffn, tf=FFN_TILE)
        w.update(zip(keys, casts))
        new_conv.append(nc[:, :, :dff])
    return (h.reshape(bsz, seq, d), jnp.stack(new_ret), jnp.stack(new_hgrn),
            jnp.stack(new_gla), jnp.stack(new_conv))


def kernel(x_prompt, x_sample, state_ret, state_hgrn, state_gla, cache_ffn_conv, norm_mix, norm_ffn, norm_final, w_in_even, w_out_even, hgrn_lb, hgrn_gnorm, w_in_odd, w_gk2, b_gk2, gla_gnorm, w_out_odd, ffn_w_up, ffn_conv_w, ffn_conv_b, ffn_w_down):
    raw = dict(norm_mix=norm_mix, norm_ffn=norm_ffn, norm_final=norm_final, w_in_even=w_in_even,
               w_out_even=w_out_even, hgrn_lb=hgrn_lb, hgrn_gnorm=hgrn_gnorm, w_in_odd=w_in_odd,
               w_gk2=w_gk2, b_gk2=b_gk2, gla_gnorm=gla_gnorm, w_out_odd=w_out_odd, ffn_w_up=ffn_w_up,
               ffn_conv_w=ffn_conv_w, ffn_conv_b=ffn_conv_b, ffn_w_down=ffn_w_down)
    w = _initial_weights(raw)
    y_p, ret_p, hgrn_p, gla_p, conv_p = _run_group(x_prompt, 0, None, None, None, None, raw, w)
    y_s, ret_s, hgrn_s, gla_s, conv_s = _run_group(
        x_sample, PAST_LEN, state_ret, state_hgrn, state_gla, cache_ffn_conv, raw, w)
    return (y_p, y_s, ret_p, ret_s, hgrn_p, hgrn_s, gla_p, gla_s, conv_p, conv_s)
```

```python
import functools

import numpy as np
import jax
import jax.numpy as jnp
from jax import lax
from jax.experimental import pallas as pl
from jax.experimental.pallas import tpu as pltpu

F32 = jnp.float32
BF16 = jnp.bfloat16

EPS = 1e-6
ROPE_BASE = 10000.0
GATE_NORMALIZER = 16.0
PAST_LEN = 1024

H_A, DK_A, DV_A = 4, 128, 256
H_B, DK_B, DV_B = 8, 128, 128
H_C, DK_C, DV_C = 4, 256, 512
GK_RANK = 16
CONV_W = 3

LANES = 128
SUBLANES = 8
VMEM_LIMIT_BYTES = 56 * 1024 * 1024

CHUNK = 64
SUB = 16
SUB_SHIFT = 4
assert 1 << SUB_SHIFT == SUB and CHUNK % SUB == 0

NT_DIMS = (((1,), (1,)), ((), ()))
TN_DIMS = (((0,), (0,)), ((), ()))


def _cparams(n_axes):
    return pltpu.CompilerParams(
        dimension_semantics=("arbitrary",) * n_axes,
        vmem_limit_bytes=VMEM_LIMIT_BYTES,
    )


def _sigmoid(x):
    return 1.0 / (1.0 + jnp.exp(-x))


def _silu(x):
    return x * _sigmoid(x)


def _rmsnorm_rows_to(x_ref, nw_ref, dst_ref, rows, copy_ref=None):
    rows = min(rows, x_ref.shape[0])
    assert x_ref.shape[0] % rows == 0
    n = x_ref.shape[0] // rows

    def body(c, carry):
        r = pl.multiple_of(c * rows, rows)
        x = x_ref[pl.ds(r, rows), :]
        if copy_ref is not None:
            copy_ref[pl.ds(r, rows), :] = x
        ms = jnp.mean(x * x, axis=-1, keepdims=True)
        dst_ref[pl.ds(r, rows), :] = ((x * lax.rsqrt(ms + EPS)) * nw_ref[...]).astype(BF16)
        return carry

    lax.fori_loop(0, n, body, 0)


def _cast_job(src, lead, rows, cin, rb, *, halves=1, cout=None, rows_out=None):
    cout = cin if cout is None else cout
    rows_out = rows if rows_out is None else rows_out
    assert rows % rb == 0 and rows_out % rb == 0
    return dict(src=src, lead=tuple(lead), rb=rb, cin=cin, cout=cout, halves=halves,
                n_src=rows // rb, n_dst=rows_out // rb, transpose=False)


def _transpose_cast_job(src, lead, rows, cin, rb):
    n = -(-rows // rb)
    return dict(src=src, lead=tuple(lead), rb=rb, cin=cin, cout=cin, halves=1, rows=rows,
                n_src=n, n_dst=n, transpose=True)


def _cast_job_io(job, grid):
    assert job["n_dst"] <= grid[0] * grid[1]
    lead = job["lead"]
    rb, cin, cout, n_dst, halves = job["rb"], job["cin"], job["cout"], job["n_dst"], job["halves"]

    def src_index(i, j):
        return lead + (jnp.minimum(i * grid[1] + j, job["n_src"] - 1), 0)

    def dst_index(i, j):
        blk = jnp.minimum(i * grid[1] + j, n_dst - 1)
        if job["transpose"]:
            return (0, blk)
        return (0, blk, 0) if halves > 1 else (blk, 0)

    if job["transpose"]:
        dst_block, dst_shape = (cin, rb), (cin, n_dst * rb)
    elif halves > 1:
        dst_block, dst_shape = (halves, rb, cout), (halves, n_dst * rb, cout)
    else:
        dst_block, dst_shape = (rb, cout), (n_dst * rb, cout)
    return (pl.BlockSpec((None,) * len(lead) + (rb, halves * cin), src_index),
            pl.BlockSpec(dst_block, dst_index),
            jax.ShapeDtypeStruct(dst_shape, BF16))


def _run_cast_jobs(jobs, src_refs, dst_refs):
    step = pl.program_id(0) * pl.num_programs(1) + pl.program_id(1)
    for job, src_ref, dst_ref in zip(jobs, src_refs, dst_refs):
        if job["transpose"]:
            first_row = jnp.minimum(step, job["n_src"] - 1) * job["rb"]
            row = lax.broadcasted_iota(jnp.int32, (job["rb"], 1), 0)
            x = jnp.where(row < job["rows"] - first_row, src_ref[...], 0.0)
            dst_ref[...] = jnp.transpose(x).astype(BF16)
            continue
        cin, cout, halves = job["cin"], job["cout"], job["halves"]
        for s in range(halves):
            y = src_ref[:, s * cin:(s + 1) * cin].astype(BF16)
            if job["n_dst"] > job["n_src"]:
                y = jnp.where(step < job["n_src"], y, jnp.zeros_like(y))
            dst = dst_ref.at[s] if halves > 1 else dst_ref
            if cout > cin:
                dst[:, :cin] = y
                dst[:, cin:] = jnp.zeros((job["rb"], cout - cin), BF16)
            else:
                dst[...] = y


def _norm_matmul_kernel(*refs, has_low, jobs):
    n_in = 3 + has_low
    n_out = 1 + has_low
    nj = len(jobs)
    x_ref, nw_ref, w_ref = refs[:3]
    src_refs = refs[n_in:n_in + nj]
    o_ref = refs[n_in + nj]
    dst_refs = refs[n_in + nj + n_out:n_in + nj + n_out + nj]
    hn_ref = refs[-1]

    @pl.when(pl.program_id(1) == 0)
    def _():
        _rmsnorm_rows_to(x_ref, nw_ref, hn_ref, 128)
        if has_low:
            low_ref = refs[n_in + nj + 1]
            low_ref[...] = jnp.dot(hn_ref[...], refs[3][...], preferred_element_type=F32)

    o_ref[...] = jnp.dot(hn_ref[...], w_ref[...], preferred_element_type=F32)
    _run_cast_jobs(jobs, src_refs, dst_refs)


def _norm_matmul(x, nw, w, n, n_low=0, jobs=(), *, tm, tn):
    m, d = x.shape
    assert n % tn == 0 and n + n_low <= w.shape[1]
    w_low = w if n_low else None
    grid = (m // tm, n // tn)
    jobs = list(jobs)
    job_io = [_cast_job_io(job, grid) for job in jobs]
    in_specs = [
        pl.BlockSpec((tm, d), lambda i, j: (i, 0)),
        pl.BlockSpec((1, d), lambda i, j: (0, 0)),
        pl.BlockSpec((d, tn), lambda i, j: (0, j)),
    ]
    out_shape = [jax.ShapeDtypeStruct((m, n), F32)]
    out_specs = [pl.BlockSpec((tm, tn), lambda i, j: (i, j))]
    args = [x, nw.reshape(1, d), w]
    if w_low is not None:
        nl = n_low
        assert n % nl == 0
        in_specs.append(pl.BlockSpec((d, nl), lambda i, j: (0, n // nl)))
        out_shape.append(jax.ShapeDtypeStruct((m, nl), F32))
        out_specs.append(pl.BlockSpec((tm, nl), lambda i, j: (i, 0)))
        args.append(w_low)
    in_specs += [io[0] for io in job_io]
    out_specs += [io[1] for io in job_io]
    out_shape += [io[2] for io in job_io]
    args += [job["src"] for job in jobs]
    outs = pl.pallas_call(
        functools.partial(_norm_matmul_kernel, has_low=w_low is not None, jobs=jobs),
        grid=grid,
        in_specs=in_specs,
        out_specs=out_specs,
        out_shape=out_shape,
        scratch_shapes=[pltpu.VMEM((tm, d), BF16)],
        compiler_params=_cparams(2),
        name="norm_matmul",
    )(*args)
    n_main = 1 + (w_low is not None)
    return outs[0], (outs[1] if w_low is not None else None), list(outs[n_main:])


def _matmul_res_kernel(*refs, n_in):
    a_refs = refs[:n_in]
    w_refs = refs[n_in:2 * n_in]
    res_ref = refs[2 * n_in]
    o_ref = refs[2 * n_in + 1]
    acc = res_ref[...]
    for a_ref, w_ref in zip(a_refs, w_refs):
        acc = acc + jnp.dot(a_ref[...], w_ref[...], preferred_element_type=F32)
    o_ref[...] = acc


def _matmul_res(a_list, w, res, *, tm, tn):
    m, n = res.shape
    ka = a_list[0].shape[1]
    assert all(a.shape[1] == ka for a in a_list) and ka * len(a_list) == w.shape[0]
    grid = (m // tm, n // tn)
    in_specs = [pl.BlockSpec((tm, ka), lambda i, j: (i, 0)) for _ in a_list]
    for rb in range(len(a_list)):
        in_specs.append(pl.BlockSpec((ka, tn), lambda i, j, rb=rb: (rb, j)))
    in_specs.append(pl.BlockSpec((tm, tn), lambda i, j: (i, j)))
    return pl.pallas_call(
        functools.partial(_matmul_res_kernel, n_in=len(a_list)),
        grid=grid,
        in_specs=in_specs,
        out_specs=pl.BlockSpec((tm, tn), lambda i, j: (i, j)),
        out_shape=jax.ShapeDtypeStruct((m, n), F32),
        compiler_params=_cparams(2),
        name="matmul_res",
    )(*a_list, *([w] * len(a_list)), res)


def _local_cumsum(lg):
    c, k = lg.shape
    x = lg.reshape(c // SUBLANES, SUBLANES, k)
    pos = lax.broadcasted_iota(jnp.int32, x.shape, 1)
    s = 1
    while s < SUBLANES:
        x = x + jnp.where(pos >= s, pltpu.roll(x, s, 1), 0.0)
        s *= 2
    per = SUB // SUBLANES
    x = x.reshape(c // SUB, per, SUBLANES, k)
    groups = [x[:, 0]]
    for g in range(1, per):
        groups.append(x[:, g] + groups[-1][:, SUBLANES - 1:SUBLANES, :])
    return jnp.stack(groups, axis=1).reshape(c, k)


def _chunk_operands(q, k, bl, tot):
    ns = q.shape[0] // SUB
    a = [jnp.zeros_like(tot[0])]
    for m in range(ns):
        a.append(a[-1] + tot[m])
    b_last = a[ns]

    qt, kbar, khat, qin, kout = [], [], [], [], []
    qd = {d: [] for d in range(2, ns)}
    for m in range(ns):
        sl = slice(m * SUB, (m + 1) * SUB)
        blm = bl[sl]
        e_in = jnp.exp(blm)
        qtm = q[sl] * e_in
        kbm = k[sl] * (1.0 / e_in)
        khm = kbm * jnp.exp(tot[m])
        qt.append(qtm)
        kbar.append(kbm)
        khat.append(khm)
        qin.append(qtm * jnp.exp(a[m]))
        kout.append(khm * jnp.exp(b_last - a[m + 1]))
        for d in range(2, ns):
            if m >= d:
                qd[d].append(qtm * jnp.exp(a[m] - a[m - d + 1]))
            else:
                qd[d].append(jnp.zeros_like(qtm))

    def cat(xs):
        return jnp.concatenate(xs, axis=0).astype(BF16)

    return dict(qt=cat(qt), kbar=cat(kbar), khat=cat(khat), qin=cat(qin), kout=cat(kout),
                qd=[cat(qd[d]) for d in range(2, ns)], decay=jnp.exp(b_last))


def _chunk_steps(heads, st_ref):
    c = heads[0][0].shape[0]
    ns = c // SUB
    ops = [_chunk_operands(q, k, bl, tot) for (q, k, _, bl, tot) in heads]
    raw = []
    for op in ops:
        s = [lax.dot_general(op["qt"], op["kbar"], NT_DIMS, preferred_element_type=F32),
             lax.dot_general(op["qt"], op["khat"], NT_DIMS, preferred_element_type=F32)]
        s += [lax.dot_general(qd, op["khat"], NT_DIMS, preferred_element_type=F32) for qd in op["qd"]]
        raw.append(s)
    ri = lax.broadcasted_iota(jnp.int32, (c, c), 0)
    ci = lax.broadcasted_iota(jnp.int32, (c, c), 1)
    dist = (ri >> SUB_SHIFT) - (ci >> SUB_SHIFT)
    outs = []
    for h, (op, s) in enumerate(zip(ops, raw)):
        scores = jnp.where((dist == 0) & (ri >= ci), s[0], jnp.where(dist == 1, s[1], 0.0))
        for d in range(2, ns):
            scores = jnp.where(dist == d, s[d], scores)
        vb = heads[h][2]
        o = jnp.dot(scores.astype(BF16), vb, preferred_element_type=F32)
        o = o + lax.dot_general(op["qin"], st_ref[h].astype(BF16), NT_DIMS,
                                preferred_element_type=F32)
        outs.append(o)
    for h, op in enumerate(ops):
        st_ref[h] = st_ref[h] * op["decay"] + lax.dot_general(
            heads[h][2], op["kout"], TN_DIMS, preferred_element_type=F32)
    return outs


def _sub_totals(bl):
    return [bl[(m + 1) * SUB - 1:(m + 1) * SUB, :] for m in range(bl.shape[0] // SUB)]


def _state_prologue(s0_ref, st_ref):
    @pl.when(pl.program_id(1) == 0)
    def _():
        for h in range(st_ref.shape[0]):
            if s0_ref is None:
                st_ref[h] = jnp.zeros(st_ref.shape[1:], F32)
            else:
                st_ref[h] = jnp.transpose(s0_ref[0, h])


def _state_epilogue(sn_ref, st_ref):
    @pl.when(pl.program_id(1) == pl.num_programs(1) - 1)
    def _():
        for h in range(st_ref.shape[0]):
            sn_ref[0, h] = jnp.transpose(st_ref[h])


def _chunk_loop(n_chunks, body, unroll):
    def step(c, carry):
        body(pl.ds(pl.multiple_of(c * CHUNK, CHUNK), CHUNK))
        return carry

    lax.fori_loop(0, n_chunks, step, 0, unroll=min(unroll, n_chunks))


def _head(ref, rows, h, width):
    return ref[rows, h * width:(h + 1) * width]


def _ret_chunk(get, put, cos, sin, st_ref):
    pos1 = ((lax.broadcasted_iota(jnp.int32, (CHUNK, 1), 0) & (SUB - 1)) + 1).astype(F32)
    half = DK_A // 2
    heads = []
    for h in range(H_A):
        lgam = float(np.log1p(-np.exp2(-5.0 - h)))
        bl = pos1 * lgam
        tot = [jnp.full((1, 1), lgam * SUB, F32)] * (CHUNK // SUB)
        qr, kr = get("q", h), get("k", h)
        q = qr * cos + pltpu.roll(qr, half, 1) * sin
        k = (kr * cos + pltpu.roll(kr, half, 1) * sin) * (DK_A ** -0.5)
        heads.append((q, k, get("v", h).astype(BF16), bl, tot))
    for h, o in enumerate(_chunk_steps(heads, st_ref)):
        mu = jnp.mean(o, axis=-1, keepdims=True)
        oc = o - mu
        var = jnp.mean(oc * oc, axis=-1, keepdims=True)
        put(h, ((oc * lax.rsqrt(var + EPS)) * _silu(get("g", h))).astype(BF16))


def _hgrn_lower_bound(lbp_ref, layer):
    lbp = lbp_ref[...]
    e = jnp.exp(lbp - jnp.max(lbp, axis=0, keepdims=True))
    sm = e / jnp.sum(e, axis=0, keepdims=True)
    return jnp.sum(sm[:layer + 1], axis=0, keepdims=True)


def _hgrn_chunk(get, put, lb_all, gn, st_ref):
    heads = []
    for h in range(H_B):
        lb = lb_all[:, h * DK_B:(h + 1) * DK_B]
        q = _silu(get("q", h)) * (DK_B ** -0.5)
        f = lb + (1.0 - lb) * _sigmoid(get("f", h))
        bl = _local_cumsum(jnp.log(f))
        heads.append((q, 1.0 - f, get("i", h).astype(BF16), bl, _sub_totals(bl)))
    for h, o in enumerate(_chunk_steps(heads, st_ref)):
        ms = jnp.mean(o * o, axis=-1, keepdims=True)
        put(h, (((o * lax.rsqrt(ms + EPS)) * gn) * _silu(get("g", h))).astype(BF16))


def _gla_chunk(get, put, low, wgk, bgk, gn, st_ref):
    x = jnp.dot(low.astype(BF16), wgk, preferred_element_type=F32) + bgk
    lg_all = (jnp.minimum(x, 0.0) - jnp.log1p(jnp.exp(-jnp.abs(x)))) * (1.0 / GATE_NORMALIZER)
    heads = []
    for h in range(H_C):
        bl = _local_cumsum(lg_all[:, h * DK_C:(h + 1) * DK_C])
        q = get("q", h) * (DK_C ** -0.5)
        heads.append((q, get("k", h), get("v", h).astype(BF16), bl, _sub_totals(bl)))
    for h, o in enumerate(_chunk_steps(heads, st_ref)):
        ms = jnp.mean(o * o, axis=-1, keepdims=True)
        put(h, (((o * lax.rsqrt(ms + EPS)) * gn) * _silu(get("g", h))).astype(BF16))


def _ref_getter(refs, widths, rows):
    return lambda name, h: _head(refs[name], rows, h, widths[name])


def _ref_putter(o_ref, width, rows):
    def put(h, x):
        o_ref[rows, h * width:(h + 1) * width] = x
    return put


def _ret_kernel(*refs, has_s0):
    if has_s0:
        q_ref, k_ref, v_ref, g_ref, cos_ref, sin_ref, s0_ref, o_ref, sn_ref, st_ref = refs
    else:
        q_ref, k_ref, v_ref, g_ref, cos_ref, sin_ref, o_ref, sn_ref, st_ref = refs
        s0_ref = None
    _state_prologue(s0_ref, st_ref)
    srcs = {"q": q_ref, "k": k_ref, "v": v_ref, "g": g_ref}
    widths = {"q": DK_A, "k": DK_A, "v": DV_A, "g": DV_A}

    def body(rows):
        _ret_chunk(_ref_getter(srcs, widths, rows), _ref_putter(o_ref, DV_A, rows),
                   cos_ref[rows, :], sin_ref[rows, :], st_ref)

    _chunk_loop(q_ref.shape[0] // CHUNK, body, 2)
    _state_epilogue(sn_ref, st_ref)


def _hgrn_kernel(*refs, has_s0, layer):
    if has_s0:
        q_ref, f_ref, i_ref, g_ref, lbp_ref, gn_ref, s0_ref, o_ref, sn_ref, st_ref = refs
    else:
        q_ref, f_ref, i_ref, g_ref, lbp_ref, gn_ref, o_ref, sn_ref, st_ref = refs
        s0_ref = None
    _state_prologue(s0_ref, st_ref)
    lb_all = _hgrn_lower_bound(lbp_ref, layer)
    gn = gn_ref[...]
    srcs = {"q": q_ref, "f": f_ref, "i": i_ref, "g": g_ref}
    widths = {"q": DK_B, "f": DK_B, "i": DV_B, "g": DV_B}

    def body(rows):
        _hgrn_chunk(_ref_getter(srcs, widths, rows), _ref_putter(o_ref, DV_B, rows),
                    lb_all, gn, st_ref)

    _chunk_loop(q_ref.shape[0] // CHUNK, body, 2)
    _state_epilogue(sn_ref, st_ref)


def _gla_kernel(*refs, has_s0):
    if has_s0:
        q_ref, k_ref, v_ref, g_ref, low_ref, wgk_ref, bgk_ref, gn_ref, s0_ref, o_ref, sn_ref, st_ref = refs
    else:
        q_ref, k_ref, v_ref, g_ref, low_ref, wgk_ref, bgk_ref, gn_ref, o_ref, sn_ref, st_ref = refs
        s0_ref = None
    _state_prologue(s0_ref, st_ref)
    gn = gn_ref[...]
    bgk = bgk_ref[...]
    srcs = {"q": q_ref, "k": k_ref, "v": v_ref, "g": g_ref}
    widths = {"q": DK_C, "k": DK_C, "v": DV_C, "g": DV_C}

    def body(rows):
        _gla_chunk(_ref_getter(srcs, widths, rows), _ref_putter(o_ref, DV_C, rows),
                   low_ref[rows, :], wgk_ref[...], bgk, gn, st_ref)

    _chunk_loop(q_ref.shape[0] // CHUNK, body, 2)
    _state_epilogue(sn_ref, st_ref)


def _row_spec(tt, width, nt, block):
    return pl.BlockSpec((tt, width), lambda b, t: (b * nt + t, block))


def _recurrence_call(body, in_specs, args, s0, bsz, seq, heads, dk, dv, tt):
    nt = seq // tt
    if s0 is not None:
        in_specs = in_specs + [pl.BlockSpec((1, heads, dk, dv), lambda b, t: (b, 0, 0, 0))]
        args = args + [s0]
    o, sn = pl.pallas_call(
        body,
        grid=(bsz, nt),
        in_specs=in_specs,
        out_specs=[
            pl.BlockSpec((tt, heads * dv), lambda b, t: (b * nt + t, 0)),
            pl.BlockSpec((1, heads, dk, dv), lambda b, t: (b, 0, 0, 0)),
        ],
        out_shape=[
            jax.ShapeDtypeStruct((bsz * seq, heads * dv), BF16),
            jax.ShapeDtypeStruct((bsz, heads, dk, dv), F32),
        ],
        scratch_shapes=[pltpu.VMEM((heads, dv, dk), F32)],
        compiler_params=_cparams(2),
        name="recurrence",
    )(*args)
    return o, sn


def _retention(p, cos2, sin2, s0, bsz, seq, tt):
    nt = seq // tt
    qa, va = H_A * DK_A, H_A * DV_A
    in_specs = [
        _row_spec(tt, qa, nt, 0),
        _row_spec(tt, qa, nt, 1),
        _row_spec(tt, va, nt, 2 * qa // va),
        _row_spec(tt, va, nt, (2 * qa + va) // va),
        pl.BlockSpec((tt, DK_A), lambda b, t: (t, 0)),
        pl.BlockSpec((tt, DK_A), lambda b, t: (t, 0)),
    ]
    args = [p, p, p, p, cos2, sin2]
    body = functools.partial(_ret_kernel, has_s0=s0 is not None)
    return _recurrence_call(body, in_specs, args, s0, bsz, seq, H_A, DK_A, DV_A, tt)


def _hgrn(p, lbp, gnorm, s0, bsz, seq, tt, layer):
    nt = seq // tt
    base = 2 * H_A * DK_A + 2 * H_A * DV_A
    qb = H_B * DK_B
    assert base % qb == 0 and H_B * DV_B == qb
    in_specs = [_row_spec(tt, qb, nt, base // qb + n) for n in range(4)]
    in_specs += [
        pl.BlockSpec(lbp.shape, lambda b, t: (0, 0)),
        pl.BlockSpec((1, DV_B), lambda b, t: (0, 0)),
    ]
    args = [p, p, p, p, lbp, gnorm.reshape(1, DV_B)]
    body = functools.partial(_hgrn_kernel, has_s0=s0 is not None, layer=layer)
    return _recurrence_call(body, in_specs, args, s0, bsz, seq, H_B, DK_B, DV_B, tt)


def _gla(p, low, wgk, bgk, gnorm, s0, bsz, seq, tt):
    nt = seq // tt
    qc, vc = H_C * DK_C, H_C * DV_C
    in_specs = [
        _row_spec(tt, qc, nt, 0),
        _row_spec(tt, qc, nt, 1),
        _row_spec(tt, vc, nt, 2 * qc // vc),
        _row_spec(tt, vc, nt, (2 * qc + vc) // vc),
        _row_spec(tt, low.shape[1], nt, 0),
        pl.BlockSpec(wgk.shape, lambda b, t: (0, 0)),
        pl.BlockSpec((1, qc), lambda b, t: (0, 0)),
        pl.BlockSpec((1, DV_C), lambda b, t: (0, 0)),
    ]
    args = [p, p, p, p, low, wgk, bgk.reshape(1, qc), gnorm.reshape(1, DV_C)]
    body = functools.partial(_gla_kernel, has_s0=s0 is not None)
    return _recurrence_call(body, in_specs, args, s0, bsz, seq, H_C, DK_C, DV_C, tt)


def _ffn_kernel(*refs, seq_rows, tiles_per_seq, has_cache, final_norm, jobs):
    refs = list(refs)
    h_ref, nw_ref, up_ref, conv_ref, wd_ref = refs[:5]
    pos = 5
    cache_ref = None
    nf_ref = None
    if has_cache:
        cache_ref = refs[pos]
        pos += 1
    if final_norm:
        nf_ref = refs[pos]
        pos += 1
    nj = len(jobs)
    src_refs = refs[pos:pos + nj]
    out_ref, newc_ref = refs[pos + nj:pos + nj + 2]
    dst_refs = refs[pos + nj + 2:pos + 2 * nj + 2]
    hn_ref, carry_ref = refs[pos + 2 * nj + 2:]

    i = pl.program_id(0)
    j = pl.program_id(1)
    tm, tf = h_ref.shape[0], wd_ref.shape[0]

    @pl.when(j == 0)
    def _():
        _rmsnorm_rows_to(h_ref, nw_ref, hn_ref, 128, copy_ref=out_ref)

    if tiles_per_seq > 1:
        @pl.when(i % tiles_per_seq == 0)
        def _():
            carry_ref[j] = jnp.zeros((CONV_W - 1, tf), F32)

    hn = hn_ref[...]
    a = jnp.dot(hn, up_ref[0], preferred_element_type=F32)
    u = jnp.dot(hn, up_ref[1], preferred_element_type=F32)

    nseq = tm // seq_rows
    r8 = lax.broadcasted_iota(jnp.int32, (SUBLANES, 1), 0)
    sh1 = pltpu.roll(a, 1, 0)
    sh2 = pltpu.roll(a, 2, 0)
    a1_parts, a2_parts = [], []
    for s in range(nseq):
        if tiles_per_seq > 1:
            p0, p1 = carry_ref[j, 0:1, :], carry_ref[j, 1:2, :]
        elif has_cache:
            p0, p1 = cache_ref[s, 0:1, :], cache_ref[s, 1:2, :]
        else:
            p0 = p1 = jnp.zeros((1, tf), F32)
        lo, hi = s * seq_rows, (s + 1) * seq_rows
        head = slice(lo, lo + SUBLANES)
        a1_parts += [jnp.where(r8 == 0, p1, sh1[head]), sh1[lo + SUBLANES:hi]]
        a2_parts += [jnp.where(r8 == 0, p0, jnp.where(r8 == 1, p1, sh2[head])),
                     sh2[lo + SUBLANES:hi]]
    a1 = jnp.concatenate(a1_parts, axis=0)
    a2 = jnp.concatenate(a2_parts, axis=0)
    cw = conv_ref[...]
    ac = cw[CONV_W:CONV_W + 1, :] + a2 * cw[0:1, :] + a1 * cw[1:2, :] + a * cw[2:3, :]

    if tiles_per_seq > 1:
        carry_ref[j] = a[tm - (CONV_W - 1):, :]
    for s in range(nseq):
        newc_ref[s] = a[(s + 1) * seq_rows - (CONV_W - 1):(s + 1) * seq_rows, :]

    y = (_silu(ac) * u).astype(BF16)
    out_ref[...] += jnp.dot(y, wd_ref[...], preferred_element_type=F32)
    _run_cast_jobs(jobs, src_refs, dst_refs)

    if final_norm:
        @pl.when(j == pl.num_programs(1) - 1)
        def _():
            rows = min(128, tm)

            def body(c, carry):
                r = pl.multiple_of(c * rows, rows)
                x = out_ref[pl.ds(r, rows), :]
                ms = jnp.mean(x * x, axis=-1, keepdims=True)
                out_ref[pl.ds(r, rows), :] = (x * lax.rsqrt(ms + EPS)) * nf_ref[...]
                return carry

            lax.fori_loop(0, tm // rows, body, 0)


def _ffn(h, nw, up, conv, wd, cache, nf, jobs=(), *, bsz, seq, tm, tf):
    m, d = h.shape
    dffp = up.shape[2]
    nj = dffp // tf
    jobs = list(jobs)
    if seq >= tm:
        tiles_per_seq = seq // tm
        seq_rows = tm
        seqs_per_tile = 1
    else:
        tiles_per_seq = 1
        seq_rows = seq
        seqs_per_tile = tm // seq
    assert seq_rows & (seq_rows - 1) == 0
    grid = (m // tm, nj)
    in_specs = [
        pl.BlockSpec((tm, d), lambda i, j: (i, 0)),
        pl.BlockSpec((1, d), lambda i, j: (0, 0)),
        pl.BlockSpec((2, d, tf), lambda i, j: (0, 0, j)),
        pl.BlockSpec((CONV_W + 1, tf), lambda i, j: (0, j)),
        pl.BlockSpec((tf, d), lambda i, j: (j, 0)),
    ]
    args = [h, nw.reshape(1, d), up, conv, wd]
    if cache is not None:
        assert tiles_per_seq == 1
        in_specs.append(pl.BlockSpec((seqs_per_tile, CONV_W - 1, tf), lambda i, j: (i, 0, j)))
        args.append(cache)
    if nf is not None:
        in_specs.append(pl.BlockSpec((1, d), lambda i, j: (0, 0)))
        args.append(nf.reshape(1, d))
    newc_spec = pl.BlockSpec((seqs_per_tile, CONV_W - 1, tf), lambda i, j: (i, 0, j))
    job_io = [_cast_job_io(job, grid) for job in jobs]
    in_specs += [io[0] for io in job_io]
    args += [job["src"] for job in jobs]
    body = functools.partial(
        _ffn_kernel, seq_rows=seq_rows, tiles_per_seq=tiles_per_seq,
        has_cache=cache is not None, final_norm=nf is not None, jobs=jobs)
    outs = pl.pallas_call(
        body,
        grid=grid,
        in_specs=in_specs,
        out_specs=[pl.BlockSpec((tm, d), lambda i, j: (i, 0)), newc_spec] + [io[1] for io in job_io],
        out_shape=[
            jax.ShapeDtypeStruct((m, d), F32),
            jax.ShapeDtypeStruct((grid[0] * seqs_per_tile, CONV_W - 1, dffp), F32),
        ] + [io[2] for io in job_io],
        scratch_shapes=[
            pltpu.VMEM((tm, d), BF16),
            pltpu.VMEM((nj, CONV_W - 1, tf), F32),
        ],
        compiler_params=_cparams(2),
        name="ffn",
    )(*args)
    out, newc = outs[0], outs[1]
    if tiles_per_seq > 1:
        newc = newc.reshape(bsz, tiles_per_seq, CONV_W - 1, dffp)[:, -1]
    return out, newc, list(outs[2:])


def _pad_to(x, axis, size):
    pad = size - x.shape[axis]
    if pad == 0:
        return x
    widths = [(0, 0)] * x.ndim
    widths[axis] = (0, pad)
    return jnp.pad(x, widths)


def _rotary_tables(pos):
    half = DK_A // 2
    inv = ROPE_BASE ** (-jnp.arange(half, dtype=F32) / half)
    ang = pos.astype(F32)[:, None] * inv[None, :]
    cos, sin = jnp.cos(ang), jnp.sin(ang)
    return jnp.concatenate([cos, cos], axis=-1), jnp.concatenate([-sin, sin], axis=-1)


FFN_TILE = 512


def _in_proj_job(raw, l):
    if l % 2 == 0:
        src = raw["w_in_even"]
        return _cast_job(src, (l // 2,), src.shape[1], src.shape[2], 16)
    src = jnp.swapaxes(raw["w_in_odd"], 1, 2)
    return _transpose_cast_job(src, (l // 2,), src.shape[1], src.shape[2], LANES)


def _initial_weights(raw):
    w = {("in", 0): raw["w_in_even"][0].astype(BF16)}
    depth = raw["norm_mix"].shape[0]
    for l in range(1, depth, 2):
        w[("gk2", l)] = _pad_to(raw["w_gk2"][l // 2], 0, LANES).astype(BF16)
    return w


def _cast_now(job):
    src = job["src"][job["lead"]]
    if job["transpose"]:
        return _pad_to(src.T.astype(BF16), 1, job["n_dst"] * job["rb"])
    rows_out = job["n_dst"] * job["rb"]
    parts = [_pad_to(_pad_to(src[:, s * job["cin"]:(s + 1) * job["cin"]].astype(BF16), 1, job["cout"]),
                     0, rows_out) for s in range(job["halves"])]
    return jnp.stack(parts) if job["halves"] > 1 else parts[0]


def _missing_jobs(w, wanted, n_steps):
    keys, jobs = [], []
    for key, job in wanted:
        if key in w:
            continue
        if job["n_dst"] <= n_steps:
            keys.append(key)
            jobs.append(job)
        else:
            w[key] = _cast_now(job)
    return keys, jobs


def _row_block(rows, rows_out, n_steps):
    fits = [rb for rb in (16, 32, 64, 128) if rows % rb == 0 and rows_out % rb == 0]
    return next((rb for rb in fits if rows_out // rb <= n_steps), fits[-1])


def _layer_weight_jobs(raw, l, dffp, n_steps):
    w_out = raw["w_out_even"] if l % 2 == 0 else raw["w_out_odd"]
    up, down = raw["ffn_w_up"], raw["ffn_w_down"]
    dff = down.shape[1]
    d = up.shape[1]
    rb, rb_down = _row_block(d, d, n_steps), _row_block(dff, dffp, n_steps)
    return [
        (("out", l), _cast_job(w_out, (l // 2,), w_out.shape[1], w_out.shape[2], rb)),
        (("up", l), _cast_job(up, (l,), d, dff, rb, halves=2, cout=dffp)),
        (("down", l), _cast_job(down, (l,), dff, down.shape[2], rb_down, rows_out=dffp)),
    ]


def _tiles(m, seq):
    return min(1024, m), 1024, min(512, m), 2048, min(512, m), min(512, seq)


def _run_group(x, pos0, s_ret, s_hgrn, s_gla, s_conv, raw, w):
    bsz, seq, d = x.shape
    m = bsz * seq
    depth = raw["norm_mix"].shape[0]
    dff = raw["ffn_w_down"].shape[1]
    dffp = -(-dff // FFN_TILE) * FFN_TILE
    tm, tn, tm_out, tn_out, tm_ffn, tt = _tiles(m, seq)
    cos2, sin2 = _rotary_tables(pos0 + jnp.arange(seq, dtype=jnp.int32))
    conv = _pad_to(jnp.concatenate([raw["ffn_conv_w"], raw["ffn_conv_b"][:, None, :]], axis=1), 2, dffp)

    h = x.reshape(m, d)
    new_ret, new_hgrn, new_gla, new_conv = [], [], [], []
    for l in range(depth):
        i = l // 2
        w_in = w[("in", l)]
        n_low = 0 if l % 2 == 0 else LANES
        n = w_in.shape[1] - n_low
        n_steps = (m // tm) * (n // tn)
        keys, jobs = _missing_jobs(w, _layer_weight_jobs(raw, l, dffp, n_steps), n_steps)
        p, low, casts = _norm_matmul(h, raw["norm_mix"][l], w_in, n, n_low, jobs, tm=tm, tn=tn)
        w.update(zip(keys, casts))
        if l % 2 == 0:
            o_a, sr = _retention(p, cos2, sin2, None if s_ret is None else s_ret[i], bsz, seq, tt)
            o_b, sh = _hgrn(p, raw["hgrn_lb"], raw["hgrn_gnorm"][i],
                            None if s_hgrn is None else s_hgrn[i], bsz, seq, tt, l)
            o_list = [o_a, o_b]
            new_ret.append(sr)
            new_hgrn.append(sh)
        else:
            o_c, sg = _gla(p, low, w[("gk2", l)], raw["b_gk2"][i], raw["gla_gnorm"][i],
                           None if s_gla is None else s_gla[i], bsz, seq, tt)
            o_list = [o_c]
            new_gla.append(sg)
        h = _matmul_res(o_list, w[("out", l)], h, tm=tm_out, tn=tn_out)

        n_steps = (m // tm_ffn) * (dffp // FFN_TILE)
        wanted = []
        if l + 1 < depth:
            wanted = [(("in", l + 1), _in_proj_job(raw, l + 1))] + _layer_weight_jobs(raw, l + 1, dffp, n_steps)
        keys, jobs = _missing_jobs(w, wanted, n_steps)
        cache = None if s_conv is None else _pad_to(s_conv[l], 2, dffp)
        h, nc, casts = _ffn(h, raw["norm_ffn"][l], w[("up", l)], conv[l], w[("down", l)], cache,
                            raw["norm_final"] if l == depth - 1 else None, jobs,
                            bsz=bsz, seq=seq, tm=tm_ffn, tf=FFN_TILE)
        w.update(zip(keys, casts))
        new_conv.append(nc[:, :, :dff])
    return (h.reshape(bsz, seq, d), jnp.stack(new_ret), jnp.stack(new_hgrn),
            jnp.stack(new_gla), jnp.stack(new_conv))


def kernel(x_prompt, x_sample, state_ret, state_hgrn, state_gla, cache_ffn_conv, norm_mix, norm_ffn, norm_final, w_in_even, w_out_even, hgrn_lb, hgrn_gnorm, w_in_odd, w_gk2, b_gk2, gla_gnorm, w_out_odd, ffn_w_up, ffn_conv_w, ffn_conv_b, ffn_w_down):
    raw = dict(norm_mix=norm_mix, norm_ffn=norm_ffn, norm_final=norm_final, w_in_even=w_in_even,
               w_out_even=w_out_even, hgrn_lb=hgrn_lb, hgrn_gnorm=hgrn_gnorm, w_in_odd=w_in_odd,
               w_gk2=w_gk2, b_gk2=b_gk2, gla_gnorm=gla_gnorm, w_out_odd=w_out_odd, ffn_w_up=ffn_w_up,
               ffn_conv_w=ffn_conv_w, ffn_conv_b=ffn_conv_b, ffn_w_down=ffn_w_down)
    w = _initial_weights(raw)
    y_p, ret_p, hgrn_p, gla_p, conv_p = _run_group(x_prompt, 0, None, None, None, None, raw, w)
    y_s, ret_s, hgrn_s, gla_s, conv_s = _run_group(
        x_sample, PAST_LEN, state_ret, state_hgrn, state_gla, cache_ffn_conv, raw, w)
    return (y_p, y_s, ret_p, ret_s, hgrn_p, hgrn_s, gla_p, gla_s, conv_p, conv_s)
```

```python
import functools

import numpy as np
import jax
import jax.numpy as jnp
from jax import lax
from jax.experimental import pallas as pl
from jax.experimental.pallas import tpu as pltpu

F32 = jnp.float32
BF16 = jnp.bfloat16

EPS = 1e-6
ROPE_BASE = 10000.0
GATE_NORMALIZER = 16.0
PAST_LEN = 1024

H_A, DK_A, DV_A = 4, 128, 256
H_B, DK_B, DV_B = 8, 128, 128
H_C, DK_C, DV_C = 4, 256, 512
GK_RANK = 16
CONV_W = 3

LANES = 128
SUBLANES = 8
VMEM_LIMIT_BYTES = 56 * 1024 * 1024

CHUNK = 64
SUB = 16
SUB_SHIFT = 4
assert 1 << SUB_SHIFT == SUB and CHUNK % SUB == 0

NT_DIMS = (((1,), (1,)), ((), ()))
TN_DIMS = (((0,), (0,)), ((), ()))


def _cparams(n_axes):
    return pltpu.CompilerParams(
        dimension_semantics=("arbitrary",) * n_axes,
        vmem_limit_bytes=VMEM_LIMIT_BYTES,
    )


def _sigmoid(x):
    return 1.0 / (1.0 + jnp.exp(-x))


def _silu(x):
    return x * _sigmoid(x)


def _rmsnorm_rows_to(x_ref, nw_ref, dst_ref, rows, copy_ref=None):
    rows = min(rows, x_ref.shape[0])
    assert x_ref.shape[0] % rows == 0
    n = x_ref.shape[0] // rows

    def body(c, carry):
        r = pl.multiple_of(c * rows, rows)
        x = x_ref[pl.ds(r, rows), :]
        if copy_ref is not None:
            copy_ref[pl.ds(r, rows), :] = x
        ms = jnp.mean(x * x, axis=-1, keepdims=True)
        dst_ref[pl.ds(r, rows), :] = ((x * lax.rsqrt(ms + EPS)) * nw_ref[...]).astype(BF16)
        return carry

    lax.fori_loop(0, n, body, 0)


def _cast_job(src, lead, rows, cin, rb, *, halves=1, cout=None, rows_out=None):
    cout = cin if cout is None else cout
    rows_out = rows if rows_out is None else rows_out
    assert rows % rb == 0 and rows_out % rb == 0
    return dict(src=src, lead=tuple(lead), rb=rb, cin=cin, cout=cout, halves=halves,
                n_src=rows // rb, n_dst=rows_out // rb, transpose=False)


def _transpose_cast_job(src, lead, rows, cin, rb):
    n = -(-rows // rb)
    return dict(src=src, lead=tuple(lead), rb=rb, cin=cin, cout=cin, halves=1, rows=rows,
                n_src=n, n_dst=n, transpose=True)


def _cast_job_io(job, grid):
    assert job["n_dst"] <= grid[0] * grid[1]
    lead = job["lead"]
    rb, cin, cout, n_dst, halves = job["rb"], job["cin"], job["cout"], job["n_dst"], job["halves"]

    def src_index(i, j):
        return lead + (jnp.minimum(i * grid[1] + j, job["n_src"] - 1), 0)

    def dst_index(i, j):
        blk = jnp.minimum(i * grid[1] + j, n_dst - 1)
        if job["transpose"]:
            return (0, blk)
        return (0, blk, 0) if halves > 1 else (blk, 0)

    if job["transpose"]:
        dst_block, dst_shape = (cin, rb), (cin, n_dst * rb)
    elif halves > 1:
        dst_block, dst_shape = (halves, rb, cout), (halves, n_dst * rb, cout)
    else:
        dst_block, dst_shape = (rb, cout), (n_dst * rb, cout)
    return (pl.BlockSpec((None,) * len(lead) + (rb, halves * cin), src_index),
            pl.BlockSpec(dst_block, dst_index),
            jax.ShapeDtypeStruct(dst_shape, BF16))


def _run_cast_jobs(jobs, src_refs, dst_refs):
    step = pl.program_id(0) * pl.num_programs(1) + pl.program_id(1)
    for job, src_ref, dst_ref in zip(jobs, src_refs, dst_refs):
        if job["transpose"]:
            first_row = jnp.minimum(step, job["n_src"] - 1) * job["rb"]
            row = lax.broadcasted_iota(jnp.int32, (job["rb"], 1), 0)
            x = jnp.where(row < job["rows"] - first_row, src_ref[...], 0.0)
            dst_ref[...] = jnp.transpose(x).astype(BF16)
            continue
        cin, cout, halves = job["cin"], job["cout"], job["halves"]
        for s in range(halves):
            y = src_ref[:, s * cin:(s + 1) * cin].astype(BF16)
            if job["n_dst"] > job["n_src"]:
                y = jnp.where(step < job["n_src"], y, jnp.zeros_like(y))
            dst = dst_ref.at[s] if halves > 1 else dst_ref
            if cout > cin:
                dst[:, :cin] = y
                dst[:, cin:] = jnp.zeros((job["rb"], cout - cin), BF16)
            else:
                dst[...] = y


def _norm_matmul_kernel(*refs, has_low, jobs):
    n_in = 3 + has_low
    n_out = 1 + has_low
    nj = len(jobs)
    x_ref, nw_ref, w_ref = refs[:3]
    src_refs = refs[n_in:n_in + nj]
    o_ref = refs[n_in + nj]
    dst_refs = refs[n_in + nj + n_out:n_in + nj + n_out + nj]
    hn_ref = refs[-1]

    @pl.when(pl.program_id(1) == 0)
    def _():
        _rmsnorm_rows_to(x_ref, nw_ref, hn_ref, 128)
        if has_low:
            low_ref = refs[n_in + nj + 1]
            low_ref[...] = jnp.dot(hn_ref[...], refs[3][...], preferred_element_type=F32)

    o_ref[...] = jnp.dot(hn_ref[...], w_ref[...], preferred_element_type=F32)
    _run_cast_jobs(jobs, src_refs, dst_refs)


def _norm_matmul(x, nw, w, n, n_low=0, jobs=(), *, tm, tn):
    m, d = x.shape
    assert n % tn == 0 and n + n_low <= w.shape[1]
    w_low = w if n_low else None
    grid = (m // tm, n // tn)
    jobs = list(jobs)
    job_io = [_cast_job_io(job, grid) for job in jobs]
    in_specs = [
        pl.BlockSpec((tm, d), lambda i, j: (i, 0)),
        pl.BlockSpec((1, d), lambda i, j: (0, 0)),
        pl.BlockSpec((d, tn), lambda i, j: (0, j)),
    ]
    out_shape = [jax.ShapeDtypeStruct((m, n), F32)]
    out_specs = [pl.BlockSpec((tm, tn), lambda i, j: (i, j))]
    args = [x, nw.reshape(1, d), w]
    if w_low is not None:
        nl = n_low
        assert n % nl == 0
        in_specs.append(pl.BlockSpec((d, nl), lambda i, j: (0, n // nl)))
        out_shape.append(jax.ShapeDtypeStruct((m, nl), F32))
        out_specs.append(pl.BlockSpec((tm, nl), lambda i, j: (i, 0)))
        args.append(w_low)
    in_specs += [io[0] for io in job_io]
    out_specs += [io[1] for io in job_io]
    out_shape += [io[2] for io in job_io]
    args += [job["src"] for job in jobs]
    outs = pl.pallas_call(
        functools.partial(_norm_matmul_kernel, has_low=w_low is not None, jobs=jobs),
        grid=grid,
        in_specs=in_specs,
        out_specs=out_specs,
        out_shape=out_shape,
        scratch_shapes=[pltpu.VMEM((tm, d), BF16)],
        compiler_params=_cparams(2),
        name="norm_matmul",
    )(*args)
    n_main = 1 + (w_low is not None)
    return outs[0], (outs[1] if w_low is not None else None), list(outs[n_main:])


def _matmul_res_kernel(*refs, n_in):
    a_refs = refs[:n_in]
    w_refs = refs[n_in:2 * n_in]
    res_ref = refs[2 * n_in]
    o_ref = refs[2 * n_in + 1]
    acc = res_ref[...]
    for a_ref, w_ref in zip(a_refs, w_refs):
        acc = acc + jnp.dot(a_ref[...], w_ref[...], preferred_element_type=F32)
    o_ref[...] = acc


def _matmul_res(a_list, w, res, *, tm, tn):
    m, n = res.shape
    ka = a_list[0].shape[1]
    assert all(a.shape[1] == ka for a in a_list) and ka * len(a_list) == w.shape[0]
    grid = (m // tm, n // tn)
    in_specs = [pl.BlockSpec((tm, ka), lambda i, j: (i, 0)) for _ in a_list]
    for rb in range(len(a_list)):
        in_specs.append(pl.BlockSpec((ka, tn), lambda i, j, rb=rb: (rb, j)))
    in_specs.append(pl.BlockSpec((tm, tn), lambda i, j: (i, j)))
    return pl.pallas_call(
        functools.partial(_matmul_res_kernel, n_in=len(a_list)),
        grid=grid,
        in_specs=in_specs,
        out_specs=pl.BlockSpec((tm, tn), lambda i, j: (i, j)),
        out_shape=jax.ShapeDtypeStruct((m, n), F32),
        compiler_params=_cparams(2),
        name="matmul_res",
    )(*a_list, *([w] * len(a_list)), res)


def _local_cumsum(lg):
    c, k = lg.shape
    x = lg.reshape(c // SUBLANES, SUBLANES, k)
    pos = lax.broadcasted_iota(jnp.int32, x.shape, 1)
    s = 1
    while s < SUBLANES:
        x = x + jnp.where(pos >= s, pltpu.roll(x, s, 1), 0.0)
        s *= 2
    per = SUB // SUBLANES
    x = x.reshape(c // SUB, per, SUBLANES, k)
    groups = [x[:, 0]]
    for g in range(1, per):
        groups.append(x[:, g] + groups[-1][:, SUBLANES - 1:SUBLANES, :])
    return jnp.stack(groups, axis=1).reshape(c, k)


def _chunk_operands(q, k, bl, tot):
    ns = q.shape[0] // SUB
    a = [jnp.zeros_like(tot[0])]
    for m in range(ns):
        a.append(a[-1] + tot[m])
    b_last = a[ns]

    qt, kbar, khat, qin, kout = [], [], [], [], []
    qd = {d: [] for d in range(2, ns)}
    for m in range(ns):
        sl = slice(m * SUB, (m + 1) * SUB)
        blm = bl[sl]
        e_in = jnp.exp(blm)
        qtm = q[sl] * e_in
        kbm = k[sl] * (1.0 / e_in)
        khm = kbm * jnp.exp(tot[m])
        qt.append(qtm)
        kbar.append(kbm)
        khat.append(khm)
        qin.append(qtm * jnp.exp(a[m]))
        kout.append(khm * jnp.exp(b_last - a[m + 1]))
        for d in range(2, ns):
            if m >= d:
                qd[d].append(qtm * jnp.exp(a[m] - a[m - d + 1]))
            else:
                qd[d].append(jnp.zeros_like(qtm))

    def cat(xs):
        return jnp.concatenate(xs, axis=0).astype(BF16)

    return dict(qt=cat(qt), kbar=cat(kbar), khat=cat(khat), qin=cat(qin), kout=cat(kout),
                qd=[cat(qd[d]) for d in range(2, ns)], decay=jnp.exp(b_last))


def _chunk_steps(chunks, st_ref):
    c = chunks[0][0][0].shape[0]
    ns = c // SUB
    ops = [[_chunk_operands(q, k, bl, tot) for (q, k, _, bl, tot) in heads] for heads in chunks]
    raw = []
    for chunk_ops in ops:
        for op in chunk_ops:
            s = [lax.dot_general(op["qt"], op["kbar"], NT_DIMS, preferred_element_type=F32),
                 lax.dot_general(op["qt"], op["khat"], NT_DIMS, preferred_element_type=F32)]
            s += [lax.dot_general(qd, op["khat"], NT_DIMS, preferred_element_type=F32)
                  for qd in op["qd"]]
            raw.append(s)
    ri = lax.broadcasted_iota(jnp.int32, (c, c), 0)
    ci = lax.broadcasted_iota(jnp.int32, (c, c), 1)
    dist = (ri >> SUB_SHIFT) - (ci >> SUB_SHIFT)
    intra = []
    for s, (_, _, vb, _, _) in zip(raw, [head for heads in chunks for head in heads]):
        scores = jnp.where((dist == 0) & (ri >= ci), s[0], jnp.where(dist == 1, s[1], 0.0))
        for d in range(2, ns):
            scores = jnp.where(dist == d, s[d], scores)
        intra.append(jnp.dot(scores.astype(BF16), vb, preferred_element_type=F32))
    outs = []
    for g, (heads, chunk_ops) in enumerate(zip(chunks, ops)):
        outs.append([
            intra[g * len(heads) + h] + lax.dot_general(
                op["qin"], st_ref[h].astype(BF16), NT_DIMS, preferred_element_type=F32)
            for h, op in enumerate(chunk_ops)])
        for h, op in enumerate(chunk_ops):
            st_ref[h] = st_ref[h] * op["decay"] + lax.dot_general(
                heads[h][2], op["kout"], TN_DIMS, preferred_element_type=F32)
    return outs


def _sub_totals(bl):
    return [bl[(m + 1) * SUB - 1:(m + 1) * SUB, :] for m in range(bl.shape[0] // SUB)]


def _state_prologue(s0_ref, st_ref):
    @pl.when(pl.program_id(1) == 0)
    def _():
        for h in range(st_ref.shape[0]):
            if s0_ref is None:
                st_ref[h] = jnp.zeros(st_ref.shape[1:], F32)
            else:
                st_ref[h] = jnp.transpose(s0_ref[0, h])


def _state_epilogue(sn_ref, st_ref):
    @pl.when(pl.program_id(1) == pl.num_programs(1) - 1)
    def _():
        for h in range(st_ref.shape[0]):
            sn_ref[0, h] = jnp.transpose(st_ref[h])


def _chunk_loop(n_chunks, body, together, unroll):
    together = min(together, n_chunks)
    assert n_chunks % together == 0
    trips = n_chunks // together

    def step(c, carry):
        body([pl.ds(pl.multiple_of((c * together + p) * CHUNK, CHUNK), CHUNK) for p in range(together)])
        return carry

    lax.fori_loop(0, trips, step, 0, unroll=min(unroll, trips))


def _head(ref, rows, h, width):
    return ref[rows, h * width:(h + 1) * width]


def _ret_chunks(chunks, st_ref):
    pos1 = ((lax.broadcasted_iota(jnp.int32, (CHUNK, 1), 0) & (SUB - 1)) + 1).astype(F32)
    half = DK_A // 2
    inputs = []
    for get, _, cos, sin in chunks:
        heads = []
        for h in range(H_A):
            lgam = float(np.log1p(-np.exp2(-5.0 - h)))
            bl = pos1 * lgam
            tot = [jnp.full((1, 1), lgam * SUB, F32)] * (CHUNK // SUB)
            qr, kr = get("q", h), get("k", h)
            q = qr * cos + pltpu.roll(qr, half, 1) * sin
            k = (kr * cos + pltpu.roll(kr, half, 1) * sin) * (DK_A ** -0.5)
            heads.append((q, k, get("v", h).astype(BF16), bl, tot))
        inputs.append(heads)
    for (get, put, _, _), outs in zip(chunks, _chunk_steps(inputs, st_ref)):
        for h, o in enumerate(outs):
            mu = jnp.mean(o, axis=-1, keepdims=True)
            oc = o - mu
            var = jnp.mean(oc * oc, axis=-1, keepdims=True)
            put(h, ((oc * lax.rsqrt(var + EPS)) * _silu(get("g", h))).astype(BF16))


def _hgrn_lower_bound(lbp_ref, layer):
    lbp = lbp_ref[...]
    e = jnp.exp(lbp - jnp.max(lbp, axis=0, keepdims=True))
    sm = e / jnp.sum(e, axis=0, keepdims=True)
    return jnp.sum(sm[:layer + 1], axis=0, keepdims=True)


def _gated_rmsnorm_out(chunks, outs_per_chunk, gn):
    for (get, put, *_), outs in zip(chunks, outs_per_chunk):
        for h, o in enumerate(outs):
            ms = jnp.mean(o * o, axis=-1, keepdims=True)
            put(h, (((o * lax.rsqrt(ms + EPS)) * gn) * _silu(get("g", h))).astype(BF16))


def _hgrn_chunks(chunks, lb_all, gn, st_ref):
    inputs = []
    for get, _ in chunks:
        heads = []
        for h in range(H_B):
            lb = lb_all[:, h * DK_B:(h + 1) * DK_B]
            q = _silu(get("q", h)) * (DK_B ** -0.5)
            f = lb + (1.0 - lb) * _sigmoid(get("f", h))
            bl = _local_cumsum(jnp.log(f))
            heads.append((q, 1.0 - f, get("i", h).astype(BF16), bl, _sub_totals(bl)))
        inputs.append(heads)
    _gated_rmsnorm_out(chunks, _chunk_steps(inputs, st_ref), gn)


def _gla_chunks(chunks, wgk, bgk, gn, st_ref):
    inputs = []
    for get, _, low in chunks:
        x = jnp.dot(low.astype(BF16), wgk, preferred_element_type=F32) + bgk
        lg_all = (jnp.minimum(x, 0.0) - jnp.log1p(jnp.exp(-jnp.abs(x)))) * (1.0 / GATE_NORMALIZER)
        heads = []
        for h in range(H_C):
            bl = _local_cumsum(lg_all[:, h * DK_C:(h + 1) * DK_C])
            q = get("q", h) * (DK_C ** -0.5)
            heads.append((q, get("k", h), get("v", h).astype(BF16), bl, _sub_totals(bl)))
        inputs.append(heads)
    _gated_rmsnorm_out(chunks, _chunk_steps(inputs, st_ref), gn)


def _ref_getter(refs, widths, rows):
    return lambda name, h: _head(refs[name], rows, h, widths[name])


def _ref_putter(o_ref, width, rows):
    def put(h, x):
        o_ref[rows, h * width:(h + 1) * width] = x
    return put


def _ret_kernel(*refs, has_s0):
    if has_s0:
        q_ref, k_ref, v_ref, g_ref, cos_ref, sin_ref, s0_ref, o_ref, sn_ref, st_ref = refs
    else:
        q_ref, k_ref, v_ref, g_ref, cos_ref, sin_ref, o_ref, sn_ref, st_ref = refs
        s0_ref = None
    _state_prologue(s0_ref, st_ref)
    srcs = {"q": q_ref, "k": k_ref, "v": v_ref, "g": g_ref}
    widths = {"q": DK_A, "k": DK_A, "v": DV_A, "g": DV_A}

    def body(rows_list):
        _ret_chunks([(_ref_getter(srcs, widths, rows), _ref_putter(o_ref, DV_A, rows),
                      cos_ref[rows, :], sin_ref[rows, :]) for rows in rows_list], st_ref)

    _chunk_loop(q_ref.shape[0] // CHUNK, body, together=1, unroll=2)
    _state_epilogue(sn_ref, st_ref)


def _hgrn_kernel(*refs, has_s0, layer):
    if has_s0:
        q_ref, f_ref, i_ref, g_ref, lbp_ref, gn_ref, s0_ref, o_ref, sn_ref, st_ref = refs
    else:
        q_ref, f_ref, i_ref, g_ref, lbp_ref, gn_ref, o_ref, sn_ref, st_ref = refs
        s0_ref = None
    _state_prologue(s0_ref, st_ref)
    lb_all = _hgrn_lower_bound(lbp_ref, layer)
    gn = gn_ref[...]
    srcs = {"q": q_ref, "f": f_ref, "i": i_ref, "g": g_ref}
    widths = {"q": DK_B, "f": DK_B, "i": DV_B, "g": DV_B}

    def body(rows_list):
        _hgrn_chunks([(_ref_getter(srcs, widths, rows), _ref_putter(o_ref, DV_B, rows))
                      for rows in rows_list], lb_all, gn, st_ref)

    _chunk_loop(q_ref.shape[0] // CHUNK, body, together=1, unroll=2)
    _state_epilogue(sn_ref, st_ref)


def _gla_kernel(*refs, has_s0):
    if has_s0:
        q_ref, k_ref, v_ref, g_ref, low_ref, wgk_ref, bgk_ref, gn_ref, s0_ref, o_ref, sn_ref, st_ref = refs
    else:
        q_ref, k_ref, v_ref, g_ref, low_ref, wgk_ref, bgk_ref, gn_ref, o_ref, sn_ref, st_ref = refs
        s0_ref = None
    _state_prologue(s0_ref, st_ref)
    gn = gn_ref[...]
    bgk = bgk_ref[...]
    srcs = {"q": q_ref, "k": k_ref, "v": v_ref, "g": g_ref}
    widths = {"q": DK_C, "k": DK_C, "v": DV_C, "g": DV_C}

    def body(rows_list):
        _gla_chunks([(_ref_getter(srcs, widths, rows), _ref_putter(o_ref, DV_C, rows), low_ref[rows, :])
                     for rows in rows_list], wgk_ref[...], bgk, gn, st_ref)

    _chunk_loop(q_ref.shape[0] // CHUNK, body, together=4, unroll=1)
    _state_epilogue(sn_ref, st_ref)


def _row_spec(tt, width, nt, block):
    return pl.BlockSpec((tt, width), lambda b, t: (b * nt + t, block))


def _recurrence_call(body, in_specs, args, s0, bsz, seq, heads, dk, dv, tt):
    nt = seq // tt
    if s0 is not None:
        in_specs = in_specs + [pl.BlockSpec((1, heads, dk, dv), lambda b, t: (b, 0, 0, 0))]
        args = args + [s0]
    o, sn = pl.pallas_call(
        body,
        grid=(bsz, nt),
        in_specs=in_specs,
        out_specs=[
            pl.BlockSpec((tt, heads * dv), lambda b, t: (b * nt + t, 0)),
            pl.BlockSpec((1, heads, dk, dv), lambda b, t: (b, 0, 0, 0)),
        ],
        out_shape=[
            jax.ShapeDtypeStruct((bsz * seq, heads * dv), BF16),
            jax.ShapeDtypeStruct((bsz, heads, dk, dv), F32),
        ],
        scratch_shapes=[pltpu.VMEM((heads, dv, dk), F32)],
        compiler_params=_cparams(2),
        name="recurrence",
    )(*args)
    return o, sn


def _retention(p, cos2, sin2, s0, bsz, seq, tt):
    nt = seq // tt
    qa, va = H_A * DK_A, H_A * DV_A
    in_specs = [
        _row_spec(tt, qa, nt, 0),
        _row_spec(tt, qa, nt, 1),
        _row_spec(tt, va, nt, 2 * qa // va),
        _row_spec(tt, va, nt, (2 * qa + va) // va),
        pl.BlockSpec((tt, DK_A), lambda b, t: (t, 0)),
        pl.BlockSpec((tt, DK_A), lambda b, t: (t, 0)),
    ]
    args = [p, p, p, p, cos2, sin2]
    body = functools.partial(_ret_kernel, has_s0=s0 is not None)
    return _recurrence_call(body, in_specs, args, s0, bsz, seq, H_A, DK_A, DV_A, tt)


def _hgrn(p, lbp, gnorm, s0, bsz, seq, tt, layer):
    nt = seq // tt
    base = 2 * H_A * DK_A + 2 * H_A * DV_A
    qb = H_B * DK_B
    assert base % qb == 0 and H_B * DV_B == qb
    in_specs = [_row_spec(tt, qb, nt, base // qb + n) for n in range(4)]
    in_specs += [
        pl.BlockSpec(lbp.shape, lambda b, t: (0, 0)),
        pl.BlockSpec((1, DV_B), lambda b, t: (0, 0)),
    ]
    args = [p, p, p, p, lbp, gnorm.reshape(1, DV_B)]
    body = functools.partial(_hgrn_kernel, has_s0=s0 is not None, layer=layer)
    return _recurrence_call(body, in_specs, args, s0, bsz, seq, H_B, DK_B, DV_B, tt)


def _gla(p, low, wgk, bgk, gnorm, s0, bsz, seq, tt):
    nt = seq // tt
    qc, vc = H_C * DK_C, H_C * DV_C
    in_specs = [
        _row_spec(tt, qc, nt, 0),
        _row_spec(tt, qc, nt, 1),
        _row_spec(tt, vc, nt, 2 * qc // vc),
        _row_spec(tt, vc, nt, (2 * qc + vc) // vc),
        _row_spec(tt, low.shape[1], nt, 0),
        pl.BlockSpec(wgk.shape, lambda b, t: (0, 0)),
        pl.BlockSpec((1, qc), lambda b, t: (0, 0)),
        pl.BlockSpec((1, DV_C), lambda b, t: (0, 0)),
    ]
    args = [p, p, p, p, low, wgk, bgk.reshape(1, qc), gnorm.reshape(1, DV_C)]
    body = functools.partial(_gla_kernel, has_s0=s0 is not None)
    return _recurrence_call(body, in_specs, args, s0, bsz, seq, H_C, DK_C, DV_C, tt)


def _ffn_kernel(*refs, seq_rows, tiles_per_seq, has_cache, final_norm, jobs):
    refs = list(refs)
    h_ref, nw_ref, up_ref, conv_ref, wd_ref = refs[:5]
    pos = 5
    cache_ref = None
    nf_ref = None
    if has_cache:
        cache_ref = refs[pos]
        pos += 1
    if final_norm:
        nf_ref = refs[pos]
        pos += 1
    nj = len(jobs)
    src_refs = refs[pos:pos + nj]
    out_ref, newc_ref = refs[pos + nj:pos + nj + 2]
    dst_refs = refs[pos + nj + 2:pos + 2 * nj + 2]
    hn_ref, carry_ref = refs[pos + 2 * nj + 2:]

    i = pl.program_id(0)
    j = pl.program_id(1)
    tm, tf = h_ref.shape[0], wd_ref.shape[0]

    @pl.when(j == 0)
    def _():
        _rmsnorm_rows_to(h_ref, nw_ref, hn_ref, 128, copy_ref=out_ref)

    if tiles_per_seq > 1:
        @pl.when(i % tiles_per_seq == 0)
        def _():
            carry_ref[j] = jnp.zeros((CONV_W - 1, tf), F32)

    hn = hn_ref[...]
    a = jnp.dot(hn, up_ref[0], preferred_element_type=F32)
    u = jnp.dot(hn, up_ref[1], preferred_element_type=F32)

    nseq = tm // seq_rows
    r8 = lax.broadcasted_iota(jnp.int32, (SUBLANES, 1), 0)
    sh1 = pltpu.roll(a, 1, 0)
    sh2 = pltpu.roll(a, 2, 0)
    a1_parts, a2_parts = [], []
    for s in range(nseq):
        if tiles_per_seq > 1:
            p0, p1 = carry_ref[j, 0:1, :], carry_ref[j, 1:2, :]
        elif has_cache:
            p0, p1 = cache_ref[s, 0:1, :], cache_ref[s, 1:2, :]
        else:
            p0 = p1 = jnp.zeros((1, tf), F32)
        lo, hi = s * seq_rows, (s + 1) * seq_rows
        head = slice(lo, lo + SUBLANES)
        a1_parts += [jnp.where(r8 == 0, p1, sh1[head]), sh1[lo + SUBLANES:hi]]
        a2_parts += [jnp.where(r8 == 0, p0, jnp.where(r8 == 1, p1, sh2[head])),
                     sh2[lo + SUBLANES:hi]]
    a1 = jnp.concatenate(a1_parts, axis=0)
    a2 = jnp.concatenate(a2_parts, axis=0)
    cw = conv_ref[...]
    ac = cw[CONV_W:CONV_W + 1, :] + a2 * cw[0:1, :] + a1 * cw[1:2, :] + a * cw[2:3, :]

    if tiles_per_seq > 1:
        carry_ref[j] = a[tm - (CONV_W - 1):, :]
    for s in range(nseq):
        newc_ref[s] = a[(s + 1) * seq_rows - (CONV_W - 1):(s + 1) * seq_rows, :]

    y = (_silu(ac) * u).astype(BF16)
    out_ref[...] += jnp.dot(y, wd_ref[...], preferred_element_type=F32)
    _run_cast_jobs(jobs, src_refs, dst_refs)

    if final_norm:
        @pl.when(j == pl.num_programs(1) - 1)
        def _():
            rows = min(128, tm)

            def body(c, carry):
                r = pl.multiple_of(c * rows, rows)
                x = out_ref[pl.ds(r, rows), :]
                ms = jnp.mean(x * x, axis=-1, keepdims=True)
                out_ref[pl.ds(r, rows), :] = (x * lax.rsqrt(ms + EPS)) * nf_ref[...]
                return carry

            lax.fori_loop(0, tm // rows, body, 0)


def _ffn(h, nw, up, conv, wd, cache, nf, jobs=(), *, bsz, seq, tm, tf):
    m, d = h.shape
    dffp = up.shape[2]
    nj = dffp // tf
    jobs = list(jobs)
    if seq >= tm:
        tiles_per_seq = seq // tm
        seq_rows = tm
        seqs_per_tile = 1
    else:
        tiles_per_seq = 1
        seq_rows = seq
        seqs_per_tile = tm // seq
    assert seq_rows & (seq_rows - 1) == 0
    grid = (m // tm, nj)
    in_specs = [
        pl.BlockSpec((tm, d), lambda i, j: (i, 0)),
        pl.BlockSpec((1, d), lambda i, j: (0, 0)),
        pl.BlockSpec((2, d, tf), lambda i, j: (0, 0, j)),
        pl.BlockSpec((CONV_W + 1, tf), lambda i, j: (0, j)),
        pl.BlockSpec((tf, d), lambda i, j: (j, 0)),
    ]
    args = [h, nw.reshape(1, d), up, conv, wd]
    if cache is not None:
        assert tiles_per_seq == 1
        in_specs.append(pl.BlockSpec((seqs_per_tile, CONV_W - 1, tf), lambda i, j: (i, 0, j)))
        args.append(cache)
    if nf is not None:
        in_specs.append(pl.BlockSpec((1, d), lambda i, j: (0, 0)))
        args.append(nf.reshape(1, d))
    newc_spec = pl.BlockSpec((seqs_per_tile, CONV_W - 1, tf), lambda i, j: (i, 0, j))
    job_io = [_cast_job_io(job, grid) for job in jobs]
    in_specs += [io[0] for io in job_io]
    args += [job["src"] for job in jobs]
    body = functools.partial(
        _ffn_kernel, seq_rows=seq_rows, tiles_per_seq=tiles_per_seq,
        has_cache=cache is not None, final_norm=nf is not None, jobs=jobs)
    outs = pl.pallas_call(
        body,
        grid=grid,
        in_specs=in_specs,
        out_specs=[pl.BlockSpec((tm, d), lambda i, j: (i, 0)), newc_spec] + [io[1] for io in job_io],
        out_shape=[
            jax.ShapeDtypeStruct((m, d), F32),
            jax.ShapeDtypeStruct((grid[0] * seqs_per_tile, CONV_W - 1, dffp), F32),
        ] + [io[2] for io in job_io],
        scratch_shapes=[
            pltpu.VMEM((tm, d), BF16),
            pltpu.VMEM((nj, CONV_W - 1, tf), F32),
        ],
        compiler_params=_cparams(2),
        name="ffn",
    )(*args)
    out, newc = outs[0], outs[1]
    if tiles_per_seq > 1:
        newc = newc.reshape(bsz, tiles_per_seq, CONV_W - 1, dffp)[:, -1]
    return out, newc, list(outs[2:])


def _pad_to(x, axis, size):
    pad = size - x.shape[axis]
    if pad == 0:
        return x
    widths = [(0, 0)] * x.ndim
    widths[axis] = (0, pad)
    return jnp.pad(x, widths)


def _rotary_tables(pos):
    half = DK_A // 2
    inv = ROPE_BASE ** (-jnp.arange(half, dtype=F32) / half)
    ang = pos.astype(F32)[:, None] * inv[None, :]
    cos, sin = jnp.cos(ang), jnp.sin(ang)
    return jnp.concatenate([cos, cos], axis=-1), jnp.concatenate([-sin, sin], axis=-1)


FFN_TILE = 512


def _in_proj_job(raw, l):
    if l % 2 == 0:
        src = raw["w_in_even"]
        return _cast_job(src, (l // 2,), src.shape[1], src.shape[2], 16)
    src = jnp.swapaxes(raw["w_in_odd"], 1, 2)
    return _transpose_cast_job(src, (l // 2,), src.shape[1], src.shape[2], LANES)


def _initial_weights(raw):
    w = {("in", 0): raw["w_in_even"][0].astype(BF16)}
    depth = raw["norm_mix"].shape[0]
    for l in range(1, depth, 2):
        w[("gk2", l)] = _pad_to(raw["w_gk2"][l // 2], 0, LANES).astype(BF16)
    return w


def _cast_now(job):
    src = job["src"][job["lead"]]
    if job["transpose"]:
        return _pad_to(src.T.astype(BF16), 1, job["n_dst"] * job["rb"])
    rows_out = job["n_dst"] * job["rb"]
    parts = [_pad_to(_pad_to(src[:, s * job["cin"]:(s + 1) * job["cin"]].astype(BF16), 1, job["cout"]),
                     0, rows_out) for s in range(job["halves"])]
    return jnp.stack(parts) if job["halves"] > 1 else parts[0]


def _missing_jobs(w, wanted, n_steps):
    keys, jobs = [], []
    for key, job in wanted:
        if key in w:
            continue
        if job["n_dst"] <= n_steps:
            keys.append(key)
            jobs.append(job)
        else:
            w[key] = _cast_now(job)
    return keys, jobs


def _row_block(rows, rows_out, n_steps):
    fits = [rb for rb in (16, 32, 64, 128) if rows % rb == 0 and rows_out % rb == 0]
    return next((rb for rb in fits if rows_out // rb <= n_steps), fits[-1])


def _layer_weight_jobs(raw, l, dffp, n_steps):
    w_out = raw["w_out_even"] if l % 2 == 0 else raw["w_out_odd"]
    up, down = raw["ffn_w_up"], raw["ffn_w_down"]
    dff = down.shape[1]
    d = up.shape[1]
    rb, rb_down = _row_block(d, d, n_steps), _row_block(dff, dffp, n_steps)
    return [
        (("out", l), _cast_job(w_out, (l // 2,), w_out.shape[1], w_out.shape[2], rb)),
        (("up", l), _cast_job(up, (l,), d, dff, rb, halves=2, cout=dffp)),
        (("down", l), _cast_job(down, (l,), dff, down.shape[2], rb_down, rows_out=dffp)),
    ]


def _tiles(m, seq):
    return min(1024, m), 1024, min(512, m), 2048, min(512, m), min(512, seq)


def _run_group(x, pos0, s_ret, s_hgrn, s_gla, s_conv, raw, w):
    bsz, seq, d = x.shape
    m = bsz * seq
    depth = raw["norm_mix"].shape[0]
    dff = raw["ffn_w_down"].shape[1]
    dffp = -(-dff // FFN_TILE) * FFN_TILE
    tm, tn, tm_out, tn_out, tm_ffn, tt = _tiles(m, seq)
    cos2, sin2 = _rotary_tables(pos0 + jnp.arange(seq, dtype=jnp.int32))
    conv = _pad_to(jnp.concatenate([raw["ffn_conv_w"], raw["ffn_conv_b"][:, None, :]], axis=1), 2, dffp)

    h = x.reshape(m, d)
    new_ret, new_hgrn, new_gla, new_conv = [], [], [], []
    for l in range(depth):
        i = l // 2
        w_in = w[("in", l)]
        n_low = 0 if l % 2 == 0 else LANES
        n = w_in.shape[1] - n_low
        n_steps = (m // tm) * (n // tn)
        keys, jobs = _missing_jobs(w, _layer_weight_jobs(raw, l, dffp, n_steps), n_steps)
        p, low, casts = _norm_matmul(h, raw["norm_mix"][l], w_in, n, n_low, jobs, tm=tm, tn=tn)
        w.update(zip(keys, casts))
        if l % 2 == 0:
            o_a, sr = _retention(p, cos2, sin2, None if s_ret is None else s_ret[i], bsz, seq, tt)
            o_b, sh = _hgrn(p, raw["hgrn_lb"], raw["hgrn_gnorm"][i],
                            None if s_hgrn is None else s_hgrn[i], bsz, seq, tt, l)
            o_list = [o_a, o_b]
            new_ret.append(sr)
            new_hgrn.append(sh)
        else:
            o_c, sg = _gla(p, low, w[("gk2", l)], raw["b_gk2"][i], raw["gla_gnorm"][i],
                           None if s_gla is None else s_gla[i], bsz, seq, tt)
            o_list = [o_c]
            new_gla.append(sg)
        h = _matmul_res(o_list, w[("out", l)], h, tm=tm_out, tn=tn_out)

        n_steps = (m // tm_ffn) * (dffp // FFN_TILE)
        wanted = []
        if l + 1 < depth:
            wanted = [(("in", l + 1), _in_proj_job(raw, l + 1))] + _layer_weight_jobs(raw, l + 1, dffp, n_steps)
        keys, jobs = _missing_jobs(w, wanted, n_steps)
        cache = None if s_conv is None else _pad_to(s_conv[l], 2, dffp)
        h, nc, casts = _ffn(h, raw["norm_ffn"][l], w[("up", l)], conv[l], w[("down", l)], cache,
                            raw["norm_final"] if l == depth - 1 else None, jobs,
                            bsz=bsz, seq=seq, tm=tm_ffn, tf=FFN_TILE)
        w.update(zip(keys, casts))
        new_conv.append(nc[:, :, :dff])
    return (h.reshape(bsz, seq, d), jnp.stack(new_ret), jnp.stack(new_hgrn),
            jnp.stack(new_gla), jnp.stack(new_conv))


def kernel(x_prompt, x_sample, state_ret, state_hgrn, state_gla, cache_ffn_conv, norm_mix, norm_ffn, norm_final, w_in_even, w_out_even, hgrn_lb, hgrn_gnorm, w_in_odd, w_gk2, b_gk2, gla_gnorm, w_out_odd, ffn_w_up, ffn_conv_w, ffn_conv_b, ffn_w_down):
    raw = dict(norm_mix=norm_mix, norm_ffn=norm_ffn, norm_final=norm_final, w_in_even=w_in_even,
               w_out_even=w_out_even, hgrn_lb=hgrn_lb, hgrn_gnorm=hgrn_gnorm, w_in_odd=w_in_odd,
               w_gk2=w_gk2, b_gk2=b_gk2, gla_gnorm=gla_gnorm, w_out_odd=w_out_odd, ffn_w_up=ffn_w_up,
               ffn_conv_w=ffn_conv_w, ffn_conv_b=ffn_conv_b, ffn_w_down=ffn_w_down)
    w = _initial_weights(raw)
    y_p, ret_p, hgrn_p, gla_p, conv_p = _run_group(x_prompt, 0, None, None, None, None, raw, w)
    y_s, ret_s, hgrn_s, gla_s, conv_s = _run_group(
        x_sample, PAST_LEN, state_ret, state_hgrn, state_gla, cache_ffn_conv, raw, w)
    return (y_p, y_s, ret_p, ret_s, hgrn_p, hgrn_s, gla_p, gla_s, conv_p, conv_s)
```

```python
import functools

import numpy as np
import jax
import jax.numpy as jnp
from jax import lax
from jax.experimental import pallas as pl
from jax.experimental.pallas import tpu as pltpu

F32 = jnp.float32
BF16 = jnp.bfloat16

EPS = 1e-6
ROPE_BASE = 10000.0
GATE_NORMALIZER = 16.0
PAST_LEN = 1024

H_A, DK_A, DV_A = 4, 128, 256
H_B, DK_B, DV_B = 8, 128, 128
H_C, DK_C, DV_C = 4, 256, 512
GK_RANK = 16
CONV_W = 3

LANES = 128
SUBLANES = 8
VMEM_LIMIT_BYTES = 56 * 1024 * 1024

CHUNK = 64
SUB = 16
SUB_SHIFT = 4
assert 1 << SUB_SHIFT == SUB and CHUNK % SUB == 0

NT_DIMS = (((1,), (1,)), ((), ()))
TN_DIMS = (((0,), (0,)), ((), ()))


def _cparams(n_axes):
    return pltpu.CompilerParams(
        dimension_semantics=("arbitrary",) * n_axes,
        vmem_limit_bytes=VMEM_LIMIT_BYTES,
    )


def _sigmoid(x):
    return 1.0 / (1.0 + jnp.exp(-x))


def _silu(x):
    return x * _sigmoid(x)


def _rmsnorm_rows_to(x_ref, nw_ref, dst_ref, rows, copy_ref=None):
    rows = min(rows, x_ref.shape[0])
    assert x_ref.shape[0] % rows == 0
    n = x_ref.shape[0] // rows

    def body(c, carry):
        r = pl.multiple_of(c * rows, rows)
        x = x_ref[pl.ds(r, rows), :]
        if copy_ref is not None:
            copy_ref[pl.ds(r, rows), :] = x
        ms = jnp.mean(x * x, axis=-1, keepdims=True)
        dst_ref[pl.ds(r, rows), :] = ((x * lax.rsqrt(ms + EPS)) * nw_ref[...]).astype(BF16)
        return carry

    lax.fori_loop(0, n, body, 0)


def _cast_job(src, lead, rows, cin, rb, *, halves=1, cout=None, rows_out=None):
    cout = cin if cout is None else cout
    rows_out = rows if rows_out is None else rows_out
    assert rows % rb == 0 and rows_out % rb == 0
    return dict(src=src, lead=tuple(lead), rb=rb, cin=cin, cout=cout, halves=halves,
                n_src=rows // rb, n_dst=rows_out // rb, transpose=False)


def _transpose_cast_job(src, lead, rows, cin, rb):
    n = -(-rows // rb)
    return dict(src=src, lead=tuple(lead), rb=rb, cin=cin, cout=cin, halves=1, rows=rows,
                n_src=n, n_dst=n, transpose=True)


def _cast_job_io(job, grid):
    assert job["n_dst"] <= grid[0] * grid[1]
    lead = job["lead"]
    rb, cin, cout, n_dst, halves = job["rb"], job["cin"], job["cout"], job["n_dst"], job["halves"]

    def src_index(i, j):
        return lead + (jnp.minimum(i * grid[1] + j, job["n_src"] - 1), 0)

    def dst_index(i, j):
        blk = jnp.minimum(i * grid[1] + j, n_dst - 1)
        if job["transpose"]:
            return (0, blk)
        return (0, blk, 0) if halves > 1 else (blk, 0)

    if job["transpose"]:
        dst_block, dst_shape = (cin, rb), (cin, n_dst * rb)
    elif halves > 1:
        dst_block, dst_shape = (halves, rb, cout), (halves, n_dst * rb, cout)
    else:
        dst_block, dst_shape = (rb, cout), (n_dst * rb, cout)
    return (pl.BlockSpec((None,) * len(lead) + (rb, halves * cin), src_index),
            pl.BlockSpec(dst_block, dst_index),
            jax.ShapeDtypeStruct(dst_shape, BF16))


def _run_cast_jobs(jobs, src_refs, dst_refs):
    step = pl.program_id(0) * pl.num_programs(1) + pl.program_id(1)
    for job, src_ref, dst_ref in zip(jobs, src_refs, dst_refs):
        if job["transpose"]:
            first_row = jnp.minimum(step, job["n_src"] - 1) * job["rb"]
            row = lax.broadcasted_iota(jnp.int32, (job["rb"], 1), 0)
            x = jnp.where(row < job["rows"] - first_row, src_ref[...], 0.0)
            dst_ref[...] = jnp.transpose(x).astype(BF16)
            continue
        cin, cout, halves = job["cin"], job["cout"], job["halves"]
        for s in range(halves):
            y = src_ref[:, s * cin:(s + 1) * cin].astype(BF16)
            if job["n_dst"] > job["n_src"]:
                y = jnp.where(step < job["n_src"], y, jnp.zeros_like(y))
            dst = dst_ref.at[s] if halves > 1 else dst_ref
            if cout > cin:
                dst[:, :cin] = y
                dst[:, cin:] = jnp.zeros((job["rb"], cout - cin), BF16)
            else:
                dst[...] = y


def _norm_matmul_kernel(*refs, has_low, jobs):
    n_in = 3 + has_low
    n_out = 1 + has_low
    nj = len(jobs)
    x_ref, nw_ref, w_ref = refs[:3]
    src_refs = refs[n_in:n_in + nj]
    o_ref = refs[n_in + nj]
    dst_refs = refs[n_in + nj + n_out:n_in + nj + n_out + nj]
    hn_ref = refs[-1]

    @pl.when(pl.program_id(1) == 0)
    def _():
        _rmsnorm_rows_to(x_ref, nw_ref, hn_ref, 128)
        if has_low:
            low_ref = refs[n_in + nj + 1]
            low_ref[...] = jnp.dot(hn_ref[...], refs[3][...], preferred_element_type=F32)

    o_ref[...] = jnp.dot(hn_ref[...], w_ref[...], preferred_element_type=F32)
    _run_cast_jobs(jobs, src_refs, dst_refs)


def _norm_matmul(x, nw, w, n, n_low=0, jobs=(), *, tm, tn):
    m, d = x.shape
    assert n % tn == 0 and n + n_low <= w.shape[1]
    w_low = w if n_low else None
    grid = (m // tm, n // tn)
    jobs = list(jobs)
    job_io = [_cast_job_io(job, grid) for job in jobs]
    in_specs = [
        pl.BlockSpec((tm, d), lambda i, j: (i, 0)),
        pl.BlockSpec((1, d), lambda i, j: (0, 0)),
        pl.BlockSpec((d, tn), lambda i, j: (0, j)),
    ]
    out_shape = [jax.ShapeDtypeStruct((m, n), F32)]
    out_specs = [pl.BlockSpec((tm, tn), lambda i, j: (i, j))]
    args = [x, nw.reshape(1, d), w]
    if w_low is not None:
        nl = n_low
        assert n % nl == 0
        in_specs.append(pl.BlockSpec((d, nl), lambda i, j: (0, n // nl)))
        out_shape.append(jax.ShapeDtypeStruct((m, nl), F32))
        out_specs.append(pl.BlockSpec((tm, nl), lambda i, j: (i, 0)))
        args.append(w_low)
    in_specs += [io[0] for io in job_io]
    out_specs += [io[1] for io in job_io]
    out_shape += [io[2] for io in job_io]
    args += [job["src"] for job in jobs]
    outs = pl.pallas_call(
        functools.partial(_norm_matmul_kernel, has_low=w_low is not None, jobs=jobs),
        grid=grid,
        in_specs=in_specs,
        out_specs=out_specs,
        out_shape=out_shape,
        scratch_shapes=[pltpu.VMEM((tm, d), BF16)],
        compiler_params=_cparams(2),
        name="norm_matmul",
    )(*args)
    n_main = 1 + (w_low is not None)
    return outs[0], (outs[1] if w_low is not None else None), list(outs[n_main:])


def _matmul_res_kernel(*refs, n_in):
    a_refs = refs[:n_in]
    w_refs = refs[n_in:2 * n_in]
    res_ref = refs[2 * n_in]
    o_ref = refs[2 * n_in + 1]
    acc = res_ref[...]
    for a_ref, w_ref in zip(a_refs, w_refs):
        acc = acc + jnp.dot(a_ref[...], w_ref[...], preferred_element_type=F32)
    o_ref[...] = acc


def _matmul_res(a_list, w, res, *, tm, tn):
    m, n = res.shape
    ka = a_list[0].shape[1]
    assert all(a.shape[1] == ka for a in a_list) and ka * len(a_list) == w.shape[0]
    grid = (m // tm, n // tn)
    in_specs = [pl.BlockSpec((tm, ka), lambda i, j: (i, 0)) for _ in a_list]
    for rb in range(len(a_list)):
        in_specs.append(pl.BlockSpec((ka, tn), lambda i, j, rb=rb: (rb, j)))
    in_specs.append(pl.BlockSpec((tm, tn), lambda i, j: (i, j)))
    return pl.pallas_call(
        functools.partial(_matmul_res_kernel, n_in=len(a_list)),
        grid=grid,
        in_specs=in_specs,
        out_specs=pl.BlockSpec((tm, tn), lambda i, j: (i, j)),
        out_shape=jax.ShapeDtypeStruct((m, n), F32),
        compiler_params=_cparams(2),
        name="matmul_res",
    )(*a_list, *([w] * len(a_list)), res)


def _local_cumsum(lg):
    c, k = lg.shape
    x = lg.reshape(c // SUBLANES, SUBLANES, k)
    pos = lax.broadcasted_iota(jnp.int32, x.shape, 1)
    s = 1
    while s < SUBLANES:
        x = x + jnp.where(pos >= s, pltpu.roll(x, s, 1), 0.0)
        s *= 2
    per = SUB // SUBLANES
    x = x.reshape(c // SUB, per, SUBLANES, k)
    groups = [x[:, 0]]
    for g in range(1, per):
        groups.append(x[:, g] + groups[-1][:, SUBLANES - 1:SUBLANES, :])
    return jnp.stack(groups, axis=1).reshape(c, k)


def _chunk_operands(q, k, bl, tot):
    ns = q.shape[0] // SUB
    a = [jnp.zeros_like(tot[0])]
    for m in range(ns):
        a.append(a[-1] + tot[m])
    b_last = a[ns]

    qt, kbar, khat, qin, kout = [], [], [], [], []
    qd = {d: [] for d in range(2, ns)}
    for m in range(ns):
        sl = slice(m * SUB, (m + 1) * SUB)
        blm = bl[sl]
        e_in = jnp.exp(blm)
        qtm = q[sl] * e_in
        kbm = k[sl] * (1.0 / e_in)
        khm = kbm * jnp.exp(tot[m])
        qt.append(qtm)
        kbar.append(kbm)
        khat.append(khm)
        qin.append(qtm * jnp.exp(a[m]))
        kout.append(khm * jnp.exp(b_last - a[m + 1]))
        for d in range(2, ns):
            if m >= d:
                qd[d].append(qtm * jnp.exp(a[m] - a[m - d + 1]))
            else:
                qd[d].append(jnp.zeros_like(qtm))

    def cat(xs):
        return jnp.concatenate(xs, axis=0).astype(BF16)

    return dict(qt=cat(qt), kbar=cat(kbar), khat=cat(khat), qin=cat(qin), kout=cat(kout),
                qd=[cat(qd[d]) for d in range(2, ns)], decay=jnp.exp(b_last))


def _chunk_steps(chunks, st_ref):
    c = chunks[0][0][0].shape[0]
    ns = c // SUB
    ops = [[_chunk_operands(q, k, bl, tot) for (q, k, _, bl, tot) in heads] for heads in chunks]
    raw = []
    for chunk_ops in ops:
        for op in chunk_ops:
            s = [lax.dot_general(op["qt"], op["kbar"], NT_DIMS, preferred_element_type=F32),
                 lax.dot_general(op["qt"], op["khat"], NT_DIMS, preferred_element_type=F32)]
            s += [lax.dot_general(qd, op["khat"], NT_DIMS, preferred_element_type=F32)
                  for qd in op["qd"]]
            raw.append(s)
    ri = lax.broadcasted_iota(jnp.int32, (c, c), 0)
    ci = lax.broadcasted_iota(jnp.int32, (c, c), 1)
    dist = (ri >> SUB_SHIFT) - (ci >> SUB_SHIFT)
    intra = []
    for s, (_, _, vb, _, _) in zip(raw, [head for heads in chunks for head in heads]):
        scores = jnp.where((dist == 0) & (ri >= ci), s[0], jnp.where(dist == 1, s[1], 0.0))
        for d in range(2, ns):
            scores = jnp.where(dist == d, s[d], scores)
        intra.append(jnp.dot(scores.astype(BF16), vb, preferred_element_type=F32))
    outs = []
    for g, (heads, chunk_ops) in enumerate(zip(chunks, ops)):
        outs.append([
            intra[g * len(heads) + h] + lax.dot_general(
                op["qin"], st_ref[h].astype(BF16), NT_DIMS, preferred_element_type=F32)
            for h, op in enumerate(chunk_ops)])
        for h, op in enumerate(chunk_ops):
            st_ref[h] = st_ref[h] * op["decay"] + lax.dot_general(
                heads[h][2], op["kout"], TN_DIMS, preferred_element_type=F32)
    return outs


def _sub_totals(bl):
    return [bl[(m + 1) * SUB - 1:(m + 1) * SUB, :] for m in range(bl.shape[0] // SUB)]


def _state_prologue(s0_ref, st_ref):
    @pl.when(pl.program_id(1) == 0)
    def _():
        for h in range(st_ref.shape[0]):
            if s0_ref is None:
                st_ref[h] = jnp.zeros(st_ref.shape[1:], F32)
            else:
                st_ref[h] = jnp.transpose(s0_ref[0, h])


def _state_epilogue(sn_ref, st_ref):
    @pl.when(pl.program_id(1) == pl.num_programs(1) - 1)
    def _():
        for h in range(st_ref.shape[0]):
            sn_ref[0, h] = jnp.transpose(st_ref[h])


def _chunk_loop(n_chunks, body, together, unroll):
    together = min(together, n_chunks)
    assert n_chunks % together == 0
    trips = n_chunks // together

    def step(c, carry):
        body([pl.ds(pl.multiple_of((c * together + p) * CHUNK, CHUNK), CHUNK) for p in range(together)])
        return carry

    lax.fori_loop(0, trips, step, 0, unroll=min(unroll, trips))


def _head(ref, rows, h, width):
    return ref[rows, h * width:(h + 1) * width]


def _ret_chunks(chunks, st_ref):
    pos1 = ((lax.broadcasted_iota(jnp.int32, (CHUNK, 1), 0) & (SUB - 1)) + 1).astype(F32)
    half = DK_A // 2
    inputs = []
    for get, _, cos, sin in chunks:
        heads = []
        for h in range(H_A):
            lgam = float(np.log1p(-np.exp2(-5.0 - h)))
            bl = pos1 * lgam
            tot = [jnp.full((1, 1), lgam * SUB, F32)] * (CHUNK // SUB)
            qr, kr = get("q", h), get("k", h)
            q = qr * cos + pltpu.roll(qr, half, 1) * sin
            k = (kr * cos + pltpu.roll(kr, half, 1) * sin) * (DK_A ** -0.5)
            heads.append((q, k, get("v", h).astype(BF16), bl, tot))
        inputs.append(heads)
    for (get, put, _, _), outs in zip(chunks, _chunk_steps(inputs, st_ref)):
        for h, o in enumerate(outs):
            mu = jnp.mean(o, axis=-1, keepdims=True)
            oc = o - mu
            var = jnp.mean(oc * oc, axis=-1, keepdims=True)
            put(h, ((oc * lax.rsqrt(var + EPS)) * _silu(get("g", h))).astype(BF16))


def _hgrn_lower_bound(lbp_ref, layer):
    lbp = lbp_ref[...]
    e = jnp.exp(lbp - jnp.max(lbp, axis=0, keepdims=True))
    sm = e / jnp.sum(e, axis=0, keepdims=True)
    return jnp.sum(sm[:layer + 1], axis=0, keepdims=True)


def _gated_rmsnorm_out(chunks, outs_per_chunk, gn):
    for (get, put, *_), outs in zip(chunks, outs_per_chunk):
        for h, o in enumerate(outs):
            ms = jnp.mean(o * o, axis=-1, keepdims=True)
            put(h, (((o * lax.rsqrt(ms + EPS)) * gn) * _silu(get("g", h))).astype(BF16))


def _hgrn_chunks(chunks, lb_all, gn, st_ref):
    inputs = []
    for get, _ in chunks:
        heads = []
        for h in range(H_B):
            lb = lb_all[:, h * DK_B:(h + 1) * DK_B]
            q = _silu(get("q", h)) * (DK_B ** -0.5)
            f = lb + (1.0 - lb) * _sigmoid(get("f", h))
            bl = _local_cumsum(jnp.log(f))
            heads.append((q, 1.0 - f, get("i", h).astype(BF16), bl, _sub_totals(bl)))
        inputs.append(heads)
    _gated_rmsnorm_out(chunks, _chunk_steps(inputs, st_ref), gn)


def _gla_chunks(chunks, wgk, bgk, gn, st_ref):
    inputs = []
    for get, _, low in chunks:
        x = jnp.dot(low.astype(BF16), wgk, preferred_element_type=F32) + bgk
        lg_all = (jnp.minimum(x, 0.0) - jnp.log1p(jnp.exp(-jnp.abs(x)))) * (1.0 / GATE_NORMALIZER)
        heads = []
        for h in range(H_C):
            bl = _local_cumsum(lg_all[:, h * DK_C:(h + 1) * DK_C])
            q = get("q", h) * (DK_C ** -0.5)
            heads.append((q, get("k", h), get("v", h).astype(BF16), bl, _sub_totals(bl)))
        inputs.append(heads)
    _gated_rmsnorm_out(chunks, _chunk_steps(inputs, st_ref), gn)


def _ref_getter(refs, widths, rows):
    return lambda name, h: _head(refs[name], rows, h, widths[name])


def _ref_putter(o_ref, width, rows):
    def put(h, x):
        o_ref[rows, h * width:(h + 1) * width] = x
    return put


def _ret_kernel(*refs, has_s0):
    if has_s0:
        q_ref, k_ref, v_ref, g_ref, cos_ref, sin_ref, s0_ref, o_ref, sn_ref, st_ref = refs
    else:
        q_ref, k_ref, v_ref, g_ref, cos_ref, sin_ref, o_ref, sn_ref, st_ref = refs
        s0_ref = None
    _state_prologue(s0_ref, st_ref)
    srcs = {"q": q_ref, "k": k_ref, "v": v_ref, "g": g_ref}
    widths = {"q": DK_A, "k": DK_A, "v": DV_A, "g": DV_A}

    def body(rows_list):
        _ret_chunks([(_ref_getter(srcs, widths, rows), _ref_putter(o_ref, DV_A, rows),
                      cos_ref[rows, :], sin_ref[rows, :]) for rows in rows_list], st_ref)

    _chunk_loop(q_ref.shape[0] // CHUNK, body, together=1, unroll=2)
    _state_epilogue(sn_ref, st_ref)


def _hgrn_kernel(*refs, has_s0, layer, jobs):
    q_ref, f_ref, i_ref, g_ref, lbp_ref, gn_ref = refs[:6]
    pos = 6
    s0_ref = refs[pos] if has_s0 else None
    pos += has_s0
    src_refs = refs[pos:pos + len(jobs)]
    o_ref, sn_ref = refs[pos + len(jobs):pos + len(jobs) + 2]
    dst_refs = refs[pos + len(jobs) + 2:pos + 2 * len(jobs) + 2]
    st_ref = refs[-1]
    _run_cast_jobs(jobs, src_refs, dst_refs)
    _state_prologue(s0_ref, st_ref)
    lb_all = _hgrn_lower_bound(lbp_ref, layer)
    gn = gn_ref[...]
    srcs = {"q": q_ref, "f": f_ref, "i": i_ref, "g": g_ref}
    widths = {"q": DK_B, "f": DK_B, "i": DV_B, "g": DV_B}

    def body(rows_list):
        _hgrn_chunks([(_ref_getter(srcs, widths, rows), _ref_putter(o_ref, DV_B, rows))
                      for rows in rows_list], lb_all, gn, st_ref)

    _chunk_loop(q_ref.shape[0] // CHUNK, body, together=1, unroll=2)
    _state_epilogue(sn_ref, st_ref)


def _gla_kernel(*refs, has_s0):
    if has_s0:
        q_ref, k_ref, v_ref, g_ref, low_ref, wgk_ref, bgk_ref, gn_ref, s0_ref, o_ref, sn_ref, st_ref = refs
    else:
        q_ref, k_ref, v_ref, g_ref, low_ref, wgk_ref, bgk_ref, gn_ref, o_ref, sn_ref, st_ref = refs
        s0_ref = None
    _state_prologue(s0_ref, st_ref)
    gn = gn_ref[...]
    bgk = bgk_ref[...]
    srcs = {"q": q_ref, "k": k_ref, "v": v_ref, "g": g_ref}
    widths = {"q": DK_C, "k": DK_C, "v": DV_C, "g": DV_C}

    def body(rows_list):
        _gla_chunks([(_ref_getter(srcs, widths, rows), _ref_putter(o_ref, DV_C, rows), low_ref[rows, :])
                     for rows in rows_list], wgk_ref[...], bgk, gn, st_ref)

    _chunk_loop(q_ref.shape[0] // CHUNK, body, together=4, unroll=1)
    _state_epilogue(sn_ref, st_ref)


def _row_spec(tt, width, nt, block):
    return pl.BlockSpec((tt, width), lambda b, t: (b * nt + t, block))


def _recurrence_call(body, in_specs, args, s0, bsz, seq, heads, dk, dv, tt, jobs=()):
    nt = seq // tt
    if s0 is not None:
        in_specs = in_specs + [pl.BlockSpec((1, heads, dk, dv), lambda b, t: (b, 0, 0, 0))]
        args = args + [s0]
    job_io = [_cast_job_io(job, (bsz, nt)) for job in jobs]
    outs = pl.pallas_call(
        body,
        grid=(bsz, nt),
        in_specs=in_specs + [io[0] for io in job_io],
        out_specs=[
            pl.BlockSpec((tt, heads * dv), lambda b, t: (b * nt + t, 0)),
            pl.BlockSpec((1, heads, dk, dv), lambda b, t: (b, 0, 0, 0)),
        ] + [io[1] for io in job_io],
        out_shape=[
            jax.ShapeDtypeStruct((bsz * seq, heads * dv), BF16),
            jax.ShapeDtypeStruct((bsz, heads, dk, dv), F32),
        ] + [io[2] for io in job_io],
        scratch_shapes=[pltpu.VMEM((heads, dv, dk), F32)],
        compiler_params=_cparams(2),
        name="recurrence",
    )(*args, *[job["src"] for job in jobs])
    return outs[0], outs[1], list(outs[2:])


def _retention(p, cos2, sin2, s0, bsz, seq, tt):
    nt = seq // tt
    qa, va = H_A * DK_A, H_A * DV_A
    in_specs = [
        _row_spec(tt, qa, nt, 0),
        _row_spec(tt, qa, nt, 1),
        _row_spec(tt, va, nt, 2 * qa // va),
        _row_spec(tt, va, nt, (2 * qa + va) // va),
        pl.BlockSpec((tt, DK_A), lambda b, t: (t, 0)),
        pl.BlockSpec((tt, DK_A), lambda b, t: (t, 0)),
    ]
    args = [p, p, p, p, cos2, sin2]
    body = functools.partial(_ret_kernel, has_s0=s0 is not None)
    return _recurrence_call(body, in_specs, args, s0, bsz, seq, H_A, DK_A, DV_A, tt)[:2]


def _hgrn(p, lbp, gnorm, s0, bsz, seq, tt, layer, jobs=()):
    nt = seq // tt
    base = 2 * H_A * DK_A + 2 * H_A * DV_A
    qb = H_B * DK_B
    assert base % qb == 0 and H_B * DV_B == qb
    in_specs = [_row_spec(tt, qb, nt, base // qb + n) for n in range(4)]
    in_specs += [
        pl.BlockSpec(lbp.shape, lambda b, t: (0, 0)),
        pl.BlockSpec((1, DV_B), lambda b, t: (0, 0)),
    ]
    args = [p, p, p, p, lbp, gnorm.reshape(1, DV_B)]
    jobs = list(jobs)
    body = functools.partial(_hgrn_kernel, has_s0=s0 is not None, layer=layer, jobs=jobs)
    return _recurrence_call(body, in_specs, args, s0, bsz, seq, H_B, DK_B, DV_B, tt, jobs)


def _gla(p, low, wgk, bgk, gnorm, s0, bsz, seq, tt):
    nt = seq // tt
    qc, vc = H_C * DK_C, H_C * DV_C
    in_specs = [
        _row_spec(tt, qc, nt, 0),
        _row_spec(tt, qc, nt, 1),
        _row_spec(tt, vc, nt, 2 * qc // vc),
        _row_spec(tt, vc, nt, (2 * qc + vc) // vc),
        _row_spec(tt, low.shape[1], nt, 0),
        pl.BlockSpec(wgk.shape, lambda b, t: (0, 0)),
        pl.BlockSpec((1, qc), lambda b, t: (0, 0)),
        pl.BlockSpec((1, DV_C), lambda b, t: (0, 0)),
    ]
    args = [p, p, p, p, low, wgk, bgk.reshape(1, qc), gnorm.reshape(1, DV_C)]
    body = functools.partial(_gla_kernel, has_s0=s0 is not None)
    return _recurrence_call(body, in_specs, args, s0, bsz, seq, H_C, DK_C, DV_C, tt)[:2]


def _ffn_kernel(*refs, seq_rows, tiles_per_seq, has_cache, final_norm, jobs):
    refs = list(refs)
    h_ref, nw_ref, up_ref, conv_ref, wd_ref = refs[:5]
    pos = 5
    cache_ref = None
    nf_ref = None
    if has_cache:
        cache_ref = refs[pos]
        pos += 1
    if final_norm:
        nf_ref = refs[pos]
        pos += 1
    nj = len(jobs)
    src_refs = refs[pos:pos + nj]
    out_ref, newc_ref = refs[pos + nj:pos + nj + 2]
    dst_refs = refs[pos + nj + 2:pos + 2 * nj + 2]
    hn_ref, carry_ref = refs[pos + 2 * nj + 2:]

    i = pl.program_id(0)
    j = pl.program_id(1)
    tm, tf = h_ref.shape[0], wd_ref.shape[0]

    @pl.when(j == 0)
    def _():
        _rmsnorm_rows_to(h_ref, nw_ref, hn_ref, 128, copy_ref=out_ref)

    if tiles_per_seq > 1:
        @pl.when(i % tiles_per_seq == 0)
        def _():
            carry_ref[j] = jnp.zeros((CONV_W - 1, tf), F32)

    hn = hn_ref[...]
    a = jnp.dot(hn, up_ref[0], preferred_element_type=F32)
    u = jnp.dot(hn, up_ref[1], preferred_element_type=F32)

    nseq = tm // seq_rows
    r8 = lax.broadcasted_iota(jnp.int32, (SUBLANES, 1), 0)
    sh1 = pltpu.roll(a, 1, 0)
    sh2 = pltpu.roll(a, 2, 0)
    a1_parts, a2_parts = [], []
    for s in range(nseq):
        if tiles_per_seq > 1:
            p0, p1 = carry_ref[j, 0:1, :], carry_ref[j, 1:2, :]
        elif has_cache:
            p0, p1 = cache_ref[s, 0:1, :], cache_ref[s, 1:2, :]
        else:
            p0 = p1 = jnp.zeros((1, tf), F32)
        lo, hi = s * seq_rows, (s + 1) * seq_rows
        head = slice(lo, lo + SUBLANES)
        a1_parts += [jnp.where(r8 == 0, p1, sh1[head]), sh1[lo + SUBLANES:hi]]
        a2_parts += [jnp.where(r8 == 0, p0, jnp.where(r8 == 1, p1, sh2[head])),
                     sh2[lo + SUBLANES:hi]]
    a1 = jnp.concatenate(a1_parts, axis=0)
    a2 = jnp.concatenate(a2_parts, axis=0)
    cw = conv_ref[...]
    ac = cw[CONV_W:CONV_W + 1, :] + a2 * cw[0:1, :] + a1 * cw[1:2, :] + a * cw[2:3, :]

    if tiles_per_seq > 1:
        carry_ref[j] = a[tm - (CONV_W - 1):, :]
    for s in range(nseq):
        newc_ref[s] = a[(s + 1) * seq_rows - (CONV_W - 1):(s + 1) * seq_rows, :]

    y = (_silu(ac) * u).astype(BF16)
    out_ref[...] += jnp.dot(y, wd_ref[...], preferred_element_type=F32)
    _run_cast_jobs(jobs, src_refs, dst_refs)

    if final_norm:
        @pl.when(j == pl.num_programs(1) - 1)
        def _():
            rows = min(128, tm)

            def body(c, carry):
                r = pl.multiple_of(c * rows, rows)
                x = out_ref[pl.ds(r, rows), :]
                ms = jnp.mean(x * x, axis=-1, keepdims=True)
                out_ref[pl.ds(r, rows), :] = (x * lax.rsqrt(ms + EPS)) * nf_ref[...]
                return carry

            lax.fori_loop(0, tm // rows, body, 0)


def _ffn(h, nw, up, conv, wd, cache, nf, jobs=(), *, bsz, seq, tm, tf):
    m, d = h.shape
    dffp = up.shape[2]
    nj = dffp // tf
    jobs = list(jobs)
    if seq >= tm:
        tiles_per_seq = seq // tm
        seq_rows = tm
        seqs_per_tile = 1
    else:
        tiles_per_seq = 1
        seq_rows = seq
        seqs_per_tile = tm // seq
    assert seq_rows & (seq_rows - 1) == 0
    grid = (m // tm, nj)
    in_specs = [
        pl.BlockSpec((tm, d), lambda i, j: (i, 0)),
        pl.BlockSpec((1, d), lambda i, j: (0, 0)),
        pl.BlockSpec((2, d, tf), lambda i, j: (0, 0, j)),
        pl.BlockSpec((CONV_W + 1, tf), lambda i, j: (0, j)),
        pl.BlockSpec((tf, d), lambda i, j: (j, 0)),
    ]
    args = [h, nw.reshape(1, d), up, conv, wd]
    if cache is not None:
        assert tiles_per_seq == 1
        in_specs.append(pl.BlockSpec((seqs_per_tile, CONV_W - 1, tf), lambda i, j: (i, 0, j)))
        args.append(cache)
    if nf is not None:
        in_specs.append(pl.BlockSpec((1, d), lambda i, j: (0, 0)))
        args.append(nf.reshape(1, d))
    newc_spec = pl.BlockSpec((seqs_per_tile, CONV_W - 1, tf), lambda i, j: (i, 0, j))
    job_io = [_cast_job_io(job, grid) for job in jobs]
    in_specs += [io[0] for io in job_io]
    args += [job["src"] for job in jobs]
    body = functools.partial(
        _ffn_kernel, seq_rows=seq_rows, tiles_per_seq=tiles_per_seq,
        has_cache=cache is not None, final_norm=nf is not None, jobs=jobs)
    outs = pl.pallas_call(
        body,
        grid=grid,
        in_specs=in_specs,
        out_specs=[pl.BlockSpec((tm, d), lambda i, j: (i, 0)), newc_spec] + [io[1] for io in job_io],
        out_shape=[
            jax.ShapeDtypeStruct((m, d), F32),
            jax.ShapeDtypeStruct((grid[0] * seqs_per_tile, CONV_W - 1, dffp), F32),
        ] + [io[2] for io in job_io],
        scratch_shapes=[
            pltpu.VMEM((tm, d), BF16),
            pltpu.VMEM((nj, CONV_W - 1, tf), F32),
        ],
        compiler_params=_cparams(2),
        name="ffn",
    )(*args)
    out, newc = outs[0], outs[1]
    if tiles_per_seq > 1:
        newc = newc.reshape(bsz, tiles_per_seq, CONV_W - 1, dffp)[:, -1]
    return out, newc, list(outs[2:])


def _pad_to(x, axis, size):
    pad = size - x.shape[axis]
    if pad == 0:
        return x
    widths = [(0, 0)] * x.ndim
    widths[axis] = (0, pad)
    return jnp.pad(x, widths)


def _rotary_tables(pos):
    half = DK_A // 2
    inv = ROPE_BASE ** (-jnp.arange(half, dtype=F32) / half)
    ang = pos.astype(F32)[:, None] * inv[None, :]
    cos, sin = jnp.cos(ang), jnp.sin(ang)
    return jnp.concatenate([cos, cos], axis=-1), jnp.concatenate([-sin, sin], axis=-1)


FFN_TILE = 512


def _in_proj_job(raw, l):
    if l % 2 == 0:
        src = raw["w_in_even"]
        return _cast_job(src, (l // 2,), src.shape[1], src.shape[2], 16)
    src = jnp.swapaxes(raw["w_in_odd"], 1, 2)
    return _transpose_cast_job(src, (l // 2,), src.shape[1], src.shape[2], LANES)


def _initial_weights(raw):
    w = {("in", 0): raw["w_in_even"][0].astype(BF16)}
    depth = raw["norm_mix"].shape[0]
    for l in range(1, depth, 2):
        w[("gk2", l)] = _pad_to(raw["w_gk2"][l // 2], 0, LANES).astype(BF16)
    return w


def _cast_now(job):
    src = job["src"][job["lead"]]
    if job["transpose"]:
        return _pad_to(src.T.astype(BF16), 1, job["n_dst"] * job["rb"])
    rows_out = job["n_dst"] * job["rb"]
    parts = [_pad_to(_pad_to(src[:, s * job["cin"]:(s + 1) * job["cin"]].astype(BF16), 1, job["cout"]),
                     0, rows_out) for s in range(job["halves"])]
    return jnp.stack(parts) if job["halves"] > 1 else parts[0]


def _missing_jobs(w, wanted, n_steps):
    keys, jobs = [], []
    for key, job in wanted:
        if key in w:
            continue
        if job["n_dst"] <= n_steps:
            keys.append(key)
            jobs.append(job)
        else:
            w[key] = _cast_now(job)
    return keys, jobs


def _row_block(rows, rows_out, n_steps):
    fits = [rb for rb in (16, 32, 64, 128) if rows % rb == 0 and rows_out % rb == 0]
    return next((rb for rb in fits if rows_out // rb <= n_steps), fits[-1])


def _layer_weight_jobs(raw, l, dffp, n_steps):
    w_out = raw["w_out_even"] if l % 2 == 0 else raw["w_out_odd"]
    up, down = raw["ffn_w_up"], raw["ffn_w_down"]
    dff = down.shape[1]
    d = up.shape[1]
    rb, rb_down = _row_block(d, d, n_steps), _row_block(dff, dffp, n_steps)
    return [
        (("out", l), _cast_job(w_out, (l // 2,), w_out.shape[1], w_out.shape[2], rb)),
        (("up", l), _cast_job(up, (l,), d, dff, rb, halves=2, cout=dffp)),
        (("down", l), _cast_job(down, (l,), dff, down.shape[2], rb_down, rows_out=dffp)),
    ]


def _tiles(m, seq):
    return min(1024, m), 1024, min(512, m), 2048, min(512, m), min(512, seq)


def _run_group(x, pos0, s_ret, s_hgrn, s_gla, s_conv, raw, w):
    bsz, seq, d = x.shape
    m = bsz * seq
    depth = raw["norm_mix"].shape[0]
    dff = raw["ffn_w_down"].shape[1]
    dffp = -(-dff // FFN_TILE) * FFN_TILE
    tm, tn, tm_out, tn_out, tm_ffn, tt = _tiles(m, seq)
    cos2, sin2 = _rotary_tables(pos0 + jnp.arange(seq, dtype=jnp.int32))
    conv = _pad_to(jnp.concatenate([raw["ffn_conv_w"], raw["ffn_conv_b"][:, None, :]], axis=1), 2, dffp)

    h = x.reshape(m, d)
    new_ret, new_hgrn, new_gla, new_conv = [], [], [], []
    for l in range(depth):
        i = l // 2
        w_in = w[("in", l)]
        n_low = 0 if l % 2 == 0 else LANES
        n = w_in.shape[1] - n_low
        n_steps = (m // tm) * (n // tn)
        rec_steps = bsz * (seq // tt)
        wanted = dict(_layer_weight_jobs(raw, l, dffp, n_steps))
        wanted_rec = dict(_layer_weight_jobs(raw, l, dffp, rec_steps))
        in_rec = [("up", l)] if l % 2 == 0 and wanted_rec[("up", l)]["n_dst"] <= rec_steps else []
        keys, jobs = _missing_jobs(w, [kv for kv in wanted.items() if kv[0] not in in_rec], n_steps)
        p, low, casts = _norm_matmul(h, raw["norm_mix"][l], w_in, n, n_low, jobs, tm=tm, tn=tn)
        w.update(zip(keys, casts))
        if l % 2 == 0:
            o_a, sr = _retention(p, cos2, sin2, None if s_ret is None else s_ret[i], bsz, seq, tt)
            keys, jobs = _missing_jobs(w, [(key, wanted_rec[key]) for key in in_rec], rec_steps)
            o_b, sh, casts = _hgrn(p, raw["hgrn_lb"], raw["hgrn_gnorm"][i],
                                   None if s_hgrn is None else s_hgrn[i], bsz, seq, tt, l, jobs)
            w.update(zip(keys, casts))
            o_list = [o_a, o_b]
            new_ret.append(sr)
            new_hgrn.append(sh)
        else:
            o_c, sg = _gla(p, low, w[("gk2", l)], raw["b_gk2"][i], raw["gla_gnorm"][i],
                           None if s_gla is None else s_gla[i], bsz, seq, tt)
            o_list = [o_c]
            new_gla.append(sg)
        h = _matmul_res(o_list, w[("out", l)], h, tm=tm_out, tn=tn_out)

        n_steps = (m // tm_ffn) * (dffp // FFN_TILE)
        wanted = []
        if l + 1 < depth:
            wanted = [(("in", l + 1), _in_proj_job(raw, l + 1))] + _layer_weight_jobs(raw, l + 1, dffp, n_steps)
        keys, jobs = _missing_jobs(w, wanted, n_steps)
        cache = None if s_conv is None else _pad_to(s_conv[l], 2, dffp)
        h, nc, casts = _ffn(h, raw["norm_ffn"][l], w[("up", l)], conv[l], w[("down", l)], cache,
                            raw["norm_final"] if l == depth - 1 else None, jobs,
                            bsz=bsz, seq=seq, tm=tm_ffn, tf=FFN_TILE)
        w.update(zip(keys, casts))
        new_conv.append(nc[:, :, :dff])
    return (h.reshape(bsz, seq, d), jnp.stack(new_ret), jnp.stack(new_hgrn),
            jnp.stack(new_gla), jnp.stack(new_conv))


def kernel(x_prompt, x_sample, state_ret, state_hgrn, state_gla, cache_ffn_conv, norm_mix, norm_ffn, norm_final, w_in_even, w_out_even, hgrn_lb, hgrn_gnorm, w_in_odd, w_gk2, b_gk2, gla_gnorm, w_out_odd, ffn_w_up, ffn_conv_w, ffn_conv_b, ffn_w_down):
    raw = dict(norm_mix=norm_mix, norm_ffn=norm_ffn, norm_final=norm_final, w_in_even=w_in_even,
               w_out_even=w_out_even, hgrn_lb=hgrn_lb, hgrn_gnorm=hgrn_gnorm, w_in_odd=w_in_odd,
               w_gk2=w_gk2, b_gk2=b_gk2, gla_gnorm=gla_gnorm, w_out_odd=w_out_odd, ffn_w_up=ffn_w_up,
               ffn_conv_w=ffn_conv_w, ffn_conv_b=ffn_conv_b, ffn_w_down=ffn_w_down)
    w = _initial_weights(raw)
    y_p, ret_p, hgrn_p, gla_p, conv_p = _run_group(x_prompt, 0, None, None, None, None, raw, w)
    y_s, ret_s, hgrn_s, gla_s, conv_s = _run_group(
        x_sample, PAST_LEN, state_ret, state_hgrn, state_gla, cache_ffn_conv, raw, w)
    return (y_p, y_s, ret_p, ret_s, hgrn_p, hgrn_s, gla_p, gla_s, conv_p, conv_s)
```

```python
import functools

import numpy as np
import jax
import jax.numpy as jnp
from jax import lax
from jax.experimental import pallas as pl
from jax.experimental.pallas import tpu as pltpu

F32 = jnp.float32
BF16 = jnp.bfloat16

EPS = 1e-6
ROPE_BASE = 10000.0
GATE_NORMALIZER = 16.0
PAST_LEN = 1024

H_A, DK_A, DV_A = 4, 128, 256
H_B, DK_B, DV_B = 8, 128, 128
H_C, DK_C, DV_C = 4, 256, 512
GK_RANK = 16
CONV_W = 3

LANES = 128
SUBLANES = 8
BF16_TILE_ROWS = 16
VMEM_LIMIT_BYTES = 56 * 1024 * 1024
NORM_ROWS = 128

CHUNK = 64
SUB = 16
SUB_SHIFT = SUB.bit_length() - 1
assert 1 << SUB_SHIFT == SUB and CHUNK % SUB == 0

NT_DIMS = (((1,), (1,)), ((), ()))
TN_DIMS = (((0,), (0,)), ((), ()))


def _cparams(n_axes):
    return pltpu.CompilerParams(
        dimension_semantics=("arbitrary",) * n_axes,
        vmem_limit_bytes=VMEM_LIMIT_BYTES,
    )


def _sigmoid(x):
    return 1.0 / (1.0 + jnp.exp(-x))


def _silu(x):
    return x * _sigmoid(x)


def _rmsnorm_rows_to(x_ref, nw_ref, dst_ref, rows, copy_ref=None):
    rows = min(rows, x_ref.shape[0])
    assert x_ref.shape[0] % rows == 0
    n = x_ref.shape[0] // rows

    def body(c, carry):
        r = pl.multiple_of(c * rows, rows)
        x = x_ref[pl.ds(r, rows), :]
        if copy_ref is not None:
            copy_ref[pl.ds(r, rows), :] = x
        ms = jnp.mean(x * x, axis=-1, keepdims=True)
        dst_ref[pl.ds(r, rows), :] = ((x * lax.rsqrt(ms + EPS)) * nw_ref[...]).astype(BF16)
        return carry

    lax.fori_loop(0, n, body, 0)


def _cast_job(src, lead, rows, cin, rb, *, halves=1, cout=None, rows_out=None):
    cout = cin if cout is None else cout
    rows_out = rows if rows_out is None else rows_out
    assert rows % rb == 0 and rows_out % rb == 0
    return dict(src=src, lead=tuple(lead), rb=rb, cin=cin, cout=cout, halves=halves,
                n_src=rows // rb, n_dst=rows_out // rb, transpose=False)


def _transpose_cast_job(src, lead, rows, cin, rb):
    n = -(-rows // rb)
    return dict(src=src, lead=tuple(lead), rb=rb, cin=cin, cout=cin, halves=1, rows=rows,
                n_src=n, n_dst=n, transpose=True)


def _cast_job_io(job, grid):
    assert job["n_dst"] <= grid[0] * grid[1]
    lead = job["lead"]
    rb, cin, cout, n_dst, halves = job["rb"], job["cin"], job["cout"], job["n_dst"], job["halves"]

    def src_index(i, j):
        return lead + (jnp.minimum(i * grid[1] + j, job["n_src"] - 1), 0)

    def dst_index(i, j):
        blk = jnp.minimum(i * grid[1] + j, n_dst - 1)
        if job["transpose"]:
            return (0, blk)
        return (0, blk, 0) if halves > 1 else (blk, 0)

    if job["transpose"]:
        dst_block, dst_shape = (cin, rb), (cin, n_dst * rb)
    elif halves > 1:
        dst_block, dst_shape = (halves, rb, cout), (halves, n_dst * rb, cout)
    else:
        dst_block, dst_shape = (rb, cout), (n_dst * rb, cout)
    return (pl.BlockSpec((None,) * len(lead) + (rb, halves * cin), src_index),
            pl.BlockSpec(dst_block, dst_index),
            jax.ShapeDtypeStruct(dst_shape, BF16))


def _run_cast_jobs(jobs, src_refs, dst_refs):
    step = pl.program_id(0) * pl.num_programs(1) + pl.program_id(1)
    for job, src_ref, dst_ref in zip(jobs, src_refs, dst_refs):
        if job["transpose"]:
            first_row = jnp.minimum(step, job["n_src"] - 1) * job["rb"]
            row = lax.broadcasted_iota(jnp.int32, (job["rb"], 1), 0)
            x = jnp.where(row < job["rows"] - first_row, src_ref[...], 0.0)
            dst_ref[...] = jnp.transpose(x).astype(BF16)
            continue
        cin, cout, halves = job["cin"], job["cout"], job["halves"]
        for s in range(halves):
            y = src_ref[:, s * cin:(s + 1) * cin].astype(BF16)
            if job["n_dst"] > job["n_src"]:
                y = jnp.where(step < job["n_src"], y, jnp.zeros_like(y))
            dst = dst_ref.at[s] if halves > 1 else dst_ref
            if cout > cin:
                dst[:, :cin] = y
                dst[:, cin:] = jnp.zeros((job["rb"], cout - cin), BF16)
            else:
                dst[...] = y


def _norm_matmul_kernel(*refs, has_low, jobs):
    n_in = 3 + has_low
    n_out = 1 + has_low
    nj = len(jobs)
    x_ref, nw_ref, w_ref = refs[:3]
    src_refs = refs[n_in:n_in + nj]
    o_ref = refs[n_in + nj]
    dst_refs = refs[n_in + nj + n_out:n_in + nj + n_out + nj]
    hn_ref = refs[-1]

    @pl.when(pl.program_id(1) == 0)
    def _():
        _rmsnorm_rows_to(x_ref, nw_ref, hn_ref, NORM_ROWS)
        if has_low:
            low_ref = refs[n_in + nj + 1]
            low_ref[...] = jnp.dot(hn_ref[...], refs[3][...], preferred_element_type=F32)

    o_ref[...] = jnp.dot(hn_ref[...], w_ref[...], preferred_element_type=F32)
    _run_cast_jobs(jobs, src_refs, dst_refs)


def _norm_matmul(x, nw, w, n, n_low=0, jobs=(), *, tm, tn):
    m, d = x.shape
    assert n % tn == 0 and n + n_low <= w.shape[1]
    w_low = w if n_low else None
    grid = (m // tm, n // tn)
    jobs = list(jobs)
    job_io = [_cast_job_io(job, grid) for job in jobs]
    in_specs = [
        pl.BlockSpec((tm, d), lambda i, j: (i, 0)),
        pl.BlockSpec((1, d), lambda i, j: (0, 0)),
        pl.BlockSpec((d, tn), lambda i, j: (0, j)),
    ]
    out_shape = [jax.ShapeDtypeStruct((m, n), F32)]
    out_specs = [pl.BlockSpec((tm, tn), lambda i, j: (i, j))]
    args = [x, nw.reshape(1, d), w]
    if w_low is not None:
        nl = n_low
        assert n % nl == 0
        in_specs.append(pl.BlockSpec((d, nl), lambda i, j: (0, n // nl)))
        out_shape.append(jax.ShapeDtypeStruct((m, nl), F32))
        out_specs.append(pl.BlockSpec((tm, nl), lambda i, j: (i, 0)))
        args.append(w_low)
    in_specs += [io[0] for io in job_io]
    out_specs += [io[1] for io in job_io]
    out_shape += [io[2] for io in job_io]
    args += [job["src"] for job in jobs]
    outs = pl.pallas_call(
        functools.partial(_norm_matmul_kernel, has_low=w_low is not None, jobs=jobs),
        grid=grid,
        in_specs=in_specs,
        out_specs=out_specs,
        out_shape=out_shape,
        scratch_shapes=[pltpu.VMEM((tm, d), BF16)],
        compiler_params=_cparams(2),
        name="norm_matmul",
    )(*args)
    n_main = 1 + (w_low is not None)
    return outs[0], (outs[1] if w_low is not None else None), list(outs[n_main:])


def _matmul_res_kernel(*refs, n_in):
    a_refs = refs[:n_in]
    w_refs = refs[n_in:2 * n_in]
    res_ref = refs[2 * n_in]
    o_ref = refs[2 * n_in + 1]
    acc = res_ref[...]
    for a_ref, w_ref in zip(a_refs, w_refs):
        acc = acc + jnp.dot(a_ref[...], w_ref[...], preferred_element_type=F32)
    o_ref[...] = acc


def _matmul_res(a_list, w, res, *, tm, tn):
    m, n = res.shape
    ka = a_list[0].shape[1]
    assert all(a.shape[1] == ka for a in a_list) and ka * len(a_list) == w.shape[0]
    grid = (m // tm, n // tn)
    in_specs = [pl.BlockSpec((tm, ka), lambda i, j: (i, 0)) for _ in a_list]
    for rb in range(len(a_list)):
        in_specs.append(pl.BlockSpec((ka, tn), lambda i, j, rb=rb: (rb, j)))
    in_specs.append(pl.BlockSpec((tm, tn), lambda i, j: (i, j)))
    return pl.pallas_call(
        functools.partial(_matmul_res_kernel, n_in=len(a_list)),
        grid=grid,
        in_specs=in_specs,
        out_specs=pl.BlockSpec((tm, tn), lambda i, j: (i, j)),
        out_shape=jax.ShapeDtypeStruct((m, n), F32),
        compiler_params=_cparams(2),
        name="matmul_res",
    )(*a_list, *([w] * len(a_list)), res)


def _local_cumsum(lg):
    c, k = lg.shape
    x = lg.reshape(c // SUBLANES, SUBLANES, k)
    pos = lax.broadcasted_iota(jnp.int32, x.shape, 1)
    s = 1
    while s < SUBLANES:
        x = x + jnp.where(pos >= s, pltpu.roll(x, s, 1), 0.0)
        s *= 2
    per = SUB // SUBLANES
    x = x.reshape(c // SUB, per, SUBLANES, k)
    groups = [x[:, 0]]
    for g in range(1, per):
        groups.append(x[:, g] + groups[-1][:, SUBLANES - 1:SUBLANES, :])
    return jnp.stack(groups, axis=1).reshape(c, k)


def _chunk_operands(q, k, bl, tot):
    ns = q.shape[0] // SUB
    a = [jnp.zeros_like(tot[0])]
    for m in range(ns):
        a.append(a[-1] + tot[m])
    b_last = a[ns]

    qt, kbar, khat, qin, kout = [], [], [], [], []
    qd = {d: [] for d in range(2, ns)}
    for m in range(ns):
        sl = slice(m * SUB, (m + 1) * SUB)
        blm = bl[sl]
        e_in = jnp.exp(blm)
        qtm = q[sl] * e_in
        kbm = k[sl] * (1.0 / e_in)
        khm = kbm * jnp.exp(tot[m])
        qt.append(qtm)
        kbar.append(kbm)
        khat.append(khm)
        qin.append(qtm * jnp.exp(a[m]))
        kout.append(khm * jnp.exp(b_last - a[m + 1]))
        for d in range(2, ns):
            if m >= d:
                qd[d].append(qtm * jnp.exp(a[m] - a[m - d + 1]))
            else:
                qd[d].append(jnp.zeros_like(qtm))

    def cat(xs):
        return jnp.concatenate(xs, axis=0).astype(BF16)

    return dict(qt=cat(qt), kbar=cat(kbar), khat=cat(khat), qin=cat(qin), kout=cat(kout),
                qd=[cat(qd[d]) for d in range(2, ns)], decay=jnp.exp(b_last))


def _chunk_steps(chunks, st_ref):
    c = chunks[0][0][0].shape[0]
    ns = c // SUB
    ops = [[_chunk_operands(q, k, bl, tot) for (q, k, _, bl, tot) in heads] for heads in chunks]
    raw = []
    for chunk_ops in ops:
        for op in chunk_ops:
            s = [lax.dot_general(op["qt"], op["kbar"], NT_DIMS, preferred_element_type=F32),
                 lax.dot_general(op["qt"], op["khat"], NT_DIMS, preferred_element_type=F32)]
            s += [lax.dot_general(qd, op["khat"], NT_DIMS, preferred_element_type=F32)
                  for qd in op["qd"]]
            raw.append(s)
    ri = lax.broadcasted_iota(jnp.int32, (c, c), 0)
    ci = lax.broadcasted_iota(jnp.int32, (c, c), 1)
    dist = (ri >> SUB_SHIFT) - (ci >> SUB_SHIFT)
    intra = []
    for s, (_, _, vb, _, _) in zip(raw, [head for heads in chunks for head in heads]):
        scores = jnp.where((dist == 0) & (ri >= ci), s[0], jnp.where(dist == 1, s[1], 0.0))
        for d in range(2, ns):
            scores = jnp.where(dist == d, s[d], scores)
        intra.append(jnp.dot(scores.astype(BF16), vb, preferred_element_type=F32))
    outs = []
    for g, (heads, chunk_ops) in enumerate(zip(chunks, ops)):
        outs.append([
            intra[g * len(heads) + h] + lax.dot_general(
                op["qin"], st_ref[h].astype(BF16), NT_DIMS, preferred_element_type=F32)
            for h, op in enumerate(chunk_ops)])
        for h, op in enumerate(chunk_ops):
            st_ref[h] = st_ref[h] * op["decay"] + lax.dot_general(
                heads[h][2], op["kout"], TN_DIMS, preferred_element_type=F32)
    return outs


def _sub_totals(bl):
    return [bl[(m + 1) * SUB - 1:(m + 1) * SUB, :] for m in range(bl.shape[0] // SUB)]


def _state_prologue(s0_ref, st_ref):
    @pl.when(pl.program_id(1) == 0)
    def _():
        for h in range(st_ref.shape[0]):
            if s0_ref is None:
                st_ref[h] = jnp.zeros(st_ref.shape[1:], F32)
            else:
                st_ref[h] = jnp.transpose(s0_ref[0, h])


def _state_epilogue(sn_ref, st_ref):
    @pl.when(pl.program_id(1) == pl.num_programs(1) - 1)
    def _():
        for h in range(st_ref.shape[0]):
            sn_ref[0, h] = jnp.transpose(st_ref[h])


def _chunk_loop(n_chunks, body, together, unroll):
    together = min(together, n_chunks)
    assert n_chunks % together == 0
    trips = n_chunks // together

    def step(c, carry):
        body([pl.ds(pl.multiple_of((c * together + p) * CHUNK, CHUNK), CHUNK) for p in range(together)])
        return carry

    lax.fori_loop(0, trips, step, 0, unroll=min(unroll, trips))


def _head(ref, rows, h, width):
    return ref[rows, h * width:(h + 1) * width]


def _ret_chunks(chunks, st_ref):
    pos1 = ((lax.broadcasted_iota(jnp.int32, (CHUNK, 1), 0) & (SUB - 1)) + 1).astype(F32)
    half = DK_A // 2
    inputs = []
    for get, _, cos, sin in chunks:
        heads = []
        for h in range(H_A):
            lgam = float(np.log1p(-np.exp2(-5.0 - h)))
            bl = pos1 * lgam
            tot = [jnp.full((1, 1), lgam * SUB, F32)] * (CHUNK // SUB)
            qr, kr = get("q", h), get("k", h)
            q = qr * cos + pltpu.roll(qr, half, 1) * sin
            k = (kr * cos + pltpu.roll(kr, half, 1) * sin) * (DK_A ** -0.5)
            heads.append((q, k, get("v", h).astype(BF16), bl, tot))
        inputs.append(heads)
    for (get, put, _, _), outs in zip(chunks, _chunk_steps(inputs, st_ref)):
        for h, o in enumerate(outs):
            mu = jnp.mean(o, axis=-1, keepdims=True)
            oc = o - mu
            var = jnp.mean(oc * oc, axis=-1, keepdims=True)
            put(h, ((oc * lax.rsqrt(var + EPS)) * _silu(get("g", h))).astype(BF16))


def _hgrn_lower_bound(lbp_ref, layer):
    lbp = lbp_ref[...]
    e = jnp.exp(lbp - jnp.max(lbp, axis=0, keepdims=True))
    sm = e / jnp.sum(e, axis=0, keepdims=True)
    return jnp.sum(sm[:layer + 1], axis=0, keepdims=True)


def _gated_rmsnorm_out(chunks, outs_per_chunk, gn):
    for (get, put, *_), outs in zip(chunks, outs_per_chunk):
        for h, o in enumerate(outs):
            ms = jnp.mean(o * o, axis=-1, keepdims=True)
            put(h, (((o * lax.rsqrt(ms + EPS)) * gn) * _silu(get("g", h))).astype(BF16))


def _hgrn_chunks(chunks, lb_all, gn, st_ref):
    inputs = []
    for get, _ in chunks:
        heads = []
        for h in range(H_B):
            lb = lb_all[:, h * DK_B:(h + 1) * DK_B]
            q = _silu(get("q", h)) * (DK_B ** -0.5)
            f = lb + (1.0 - lb) * _sigmoid(get("f", h))
            bl = _local_cumsum(jnp.log(f))
            heads.append((q, 1.0 - f, get("i", h).astype(BF16), bl, _sub_totals(bl)))
        inputs.append(heads)
    _gated_rmsnorm_out(chunks, _chunk_steps(inputs, st_ref), gn)


def _gla_chunks(chunks, wgk, bgk, gn, st_ref):
    inputs = []
    for get, _, low in chunks:
        x = jnp.dot(low.astype(BF16), wgk, preferred_element_type=F32) + bgk
        lg_all = (jnp.minimum(x, 0.0) - jnp.log1p(jnp.exp(-jnp.abs(x)))) * (1.0 / GATE_NORMALIZER)
        heads = []
        for h in range(H_C):
            bl = _local_cumsum(lg_all[:, h * DK_C:(h + 1) * DK_C])
            q = get("q", h) * (DK_C ** -0.5)
            heads.append((q, get("k", h), get("v", h).astype(BF16), bl, _sub_totals(bl)))
        inputs.append(heads)
    _gated_rmsnorm_out(chunks, _chunk_steps(inputs, st_ref), gn)


def _ref_getter(refs, widths, rows):
    return lambda name, h: _head(refs[name], rows, h, widths[name])


def _ref_putter(o_ref, width, rows):
    def put(h, x):
        o_ref[rows, h * width:(h + 1) * width] = x
    return put


def _ret_kernel(*refs, has_s0):
    if has_s0:
        q_ref, k_ref, v_ref, g_ref, cos_ref, sin_ref, s0_ref, o_ref, sn_ref, st_ref = refs
    else:
        q_ref, k_ref, v_ref, g_ref, cos_ref, sin_ref, o_ref, sn_ref, st_ref = refs
        s0_ref = None
    _state_prologue(s0_ref, st_ref)
    srcs = {"q": q_ref, "k": k_ref, "v": v_ref, "g": g_ref}
    widths = {"q": DK_A, "k": DK_A, "v": DV_A, "g": DV_A}

    def body(rows_list):
        _ret_chunks([(_ref_getter(srcs, widths, rows), _ref_putter(o_ref, DV_A, rows),
                      cos_ref[rows, :], sin_ref[rows, :]) for rows in rows_list], st_ref)

    _chunk_loop(q_ref.shape[0] // CHUNK, body, together=1, unroll=2)
    _state_epilogue(sn_ref, st_ref)


def _hgrn_kernel(*refs, has_s0, layer, jobs):
    q_ref, f_ref, i_ref, g_ref, lbp_ref, gn_ref = refs[:6]
    pos = 6
    s0_ref = refs[pos] if has_s0 else None
    pos += has_s0
    src_refs = refs[pos:pos + len(jobs)]
    o_ref, sn_ref = refs[pos + len(jobs):pos + len(jobs) + 2]
    dst_refs = refs[pos + len(jobs) + 2:pos + 2 * len(jobs) + 2]
    st_ref = refs[-1]
    _run_cast_jobs(jobs, src_refs, dst_refs)
    _state_prologue(s0_ref, st_ref)
    lb_all = _hgrn_lower_bound(lbp_ref, layer)
    gn = gn_ref[...]
    srcs = {"q": q_ref, "f": f_ref, "i": i_ref, "g": g_ref}
    widths = {"q": DK_B, "f": DK_B, "i": DV_B, "g": DV_B}

    def body(rows_list):
        _hgrn_chunks([(_ref_getter(srcs, widths, rows), _ref_putter(o_ref, DV_B, rows))
                      for rows in rows_list], lb_all, gn, st_ref)

    _chunk_loop(q_ref.shape[0] // CHUNK, body, together=1, unroll=2)
    _state_epilogue(sn_ref, st_ref)


def _gla_kernel(*refs, has_s0):
    if has_s0:
        q_ref, k_ref, v_ref, g_ref, low_ref, wgk_ref, bgk_ref, gn_ref, s0_ref, o_ref, sn_ref, st_ref = refs
    else:
        q_ref, k_ref, v_ref, g_ref, low_ref, wgk_ref, bgk_ref, gn_ref, o_ref, sn_ref, st_ref = refs
        s0_ref = None
    _state_prologue(s0_ref, st_ref)
    gn = gn_ref[...]
    bgk = bgk_ref[...]
    srcs = {"q": q_ref, "k": k_ref, "v": v_ref, "g": g_ref}
    widths = {"q": DK_C, "k": DK_C, "v": DV_C, "g": DV_C}

    def body(rows_list):
        _gla_chunks([(_ref_getter(srcs, widths, rows), _ref_putter(o_ref, DV_C, rows), low_ref[rows, :])
                     for rows in rows_list], wgk_ref[...], bgk, gn, st_ref)

    _chunk_loop(q_ref.shape[0] // CHUNK, body, together=4, unroll=1)
    _state_epilogue(sn_ref, st_ref)


def _row_spec(tt, width, nt, block):
    return pl.BlockSpec((tt, width), lambda b, t: (b * nt + t, block))


def _recurrence_call(body, in_specs, args, s0, bsz, seq, heads, dk, dv, tt, jobs=()):
    nt = seq // tt
    if s0 is not None:
        in_specs = in_specs + [pl.BlockSpec((1, heads, dk, dv), lambda b, t: (b, 0, 0, 0))]
        args = args + [s0]
    job_io = [_cast_job_io(job, (bsz, nt)) for job in jobs]
    outs = pl.pallas_call(
        body,
        grid=(bsz, nt),
        in_specs=in_specs + [io[0] for io in job_io],
        out_specs=[
            pl.BlockSpec((tt, heads * dv), lambda b, t: (b * nt + t, 0)),
            pl.BlockSpec((1, heads, dk, dv), lambda b, t: (b, 0, 0, 0)),
        ] + [io[1] for io in job_io],
        out_shape=[
            jax.ShapeDtypeStruct((bsz * seq, heads * dv), BF16),
            jax.ShapeDtypeStruct((bsz, heads, dk, dv), F32),
        ] + [io[2] for io in job_io],
        scratch_shapes=[pltpu.VMEM((heads, dv, dk), F32)],
        compiler_params=_cparams(2),
        name="recurrence",
    )(*args, *[job["src"] for job in jobs])
    return outs[0], outs[1], list(outs[2:])


def _retention(p, cos2, sin2, s0, bsz, seq, tt):
    nt = seq // tt
    qa, va = H_A * DK_A, H_A * DV_A
    in_specs = [
        _row_spec(tt, qa, nt, 0),
        _row_spec(tt, qa, nt, 1),
        _row_spec(tt, va, nt, 2 * qa // va),
        _row_spec(tt, va, nt, (2 * qa + va) // va),
        pl.BlockSpec((tt, DK_A), lambda b, t: (t, 0)),
        pl.BlockSpec((tt, DK_A), lambda b, t: (t, 0)),
    ]
    args = [p, p, p, p, cos2, sin2]
    body = functools.partial(_ret_kernel, has_s0=s0 is not None)
    return _recurrence_call(body, in_specs, args, s0, bsz, seq, H_A, DK_A, DV_A, tt)[:2]


def _hgrn(p, lbp, gnorm, s0, bsz, seq, tt, layer, jobs=()):
    nt = seq // tt
    base = 2 * H_A * DK_A + 2 * H_A * DV_A
    qb = H_B * DK_B
    assert base % qb == 0 and H_B * DV_B == qb
    in_specs = [_row_spec(tt, qb, nt, base // qb + n) for n in range(4)]
    in_specs += [
        pl.BlockSpec(lbp.shape, lambda b, t: (0, 0)),
        pl.BlockSpec((1, DV_B), lambda b, t: (0, 0)),
    ]
    args = [p, p, p, p, lbp, gnorm.reshape(1, DV_B)]
    jobs = list(jobs)
    body = functools.partial(_hgrn_kernel, has_s0=s0 is not None, layer=layer, jobs=jobs)
    return _recurrence_call(body, in_specs, args, s0, bsz, seq, H_B, DK_B, DV_B, tt, jobs)


def _gla(p, low, wgk, bgk, gnorm, s0, bsz, seq, tt):
    nt = seq // tt
    qc, vc = H_C * DK_C, H_C * DV_C
    in_specs = [
        _row_spec(tt, qc, nt, 0),
        _row_spec(tt, qc, nt, 1),
        _row_spec(tt, vc, nt, 2 * qc // vc),
        _row_spec(tt, vc, nt, (2 * qc + vc) // vc),
        _row_spec(tt, low.shape[1], nt, 0),
        pl.BlockSpec(wgk.shape, lambda b, t: (0, 0)),
        pl.BlockSpec((1, qc), lambda b, t: (0, 0)),
        pl.BlockSpec((1, DV_C), lambda b, t: (0, 0)),
    ]
    args = [p, p, p, p, low, wgk, bgk.reshape(1, qc), gnorm.reshape(1, DV_C)]
    body = functools.partial(_gla_kernel, has_s0=s0 is not None)
    return _recurrence_call(body, in_specs, args, s0, bsz, seq, H_C, DK_C, DV_C, tt)[:2]


def _ffn_kernel(*refs, seq_rows, tiles_per_seq, has_cache, final_norm, jobs):
    refs = list(refs)
    h_ref, nw_ref, up_ref, conv_ref, wd_ref = refs[:5]
    pos = 5
    cache_ref = None
    nf_ref = None
    if has_cache:
        cache_ref = refs[pos]
        pos += 1
    if final_norm:
        nf_ref = refs[pos]
        pos += 1
    nj = len(jobs)
    src_refs = refs[pos:pos + nj]
    out_ref, newc_ref = refs[pos + nj:pos + nj + 2]
    dst_refs = refs[pos + nj + 2:pos + 2 * nj + 2]
    hn_ref, carry_ref = refs[pos + 2 * nj + 2:]

    i = pl.program_id(0)
    j = pl.program_id(1)
    tm, tf = h_ref.shape[0], wd_ref.shape[0]

    @pl.when(j == 0)
    def _():
        _rmsnorm_rows_to(h_ref, nw_ref, hn_ref, NORM_ROWS, copy_ref=out_ref)

    if tiles_per_seq > 1:
        @pl.when(i % tiles_per_seq == 0)
        def _():
            carry_ref[j] = jnp.zeros((CONV_W - 1, tf), F32)

    hn = hn_ref[...]
    a = jnp.dot(hn, up_ref[0], preferred_element_type=F32)
    u = jnp.dot(hn, up_ref[1], preferred_element_type=F32)

    nseq = tm // seq_rows
    r8 = lax.broadcasted_iota(jnp.int32, (SUBLANES, 1), 0)
    sh1 = pltpu.roll(a, 1, 0)
    sh2 = pltpu.roll(a, 2, 0)
    a1_parts, a2_parts = [], []
    for s in range(nseq):
        if tiles_per_seq > 1:
            p0, p1 = carry_ref[j, 0:1, :], carry_ref[j, 1:2, :]
        elif has_cache:
            p0, p1 = cache_ref[s, 0:1, :], cache_ref[s, 1:2, :]
        else:
            p0 = p1 = jnp.zeros((1, tf), F32)
        lo, hi = s * seq_rows, (s + 1) * seq_rows
        head = slice(lo, lo + SUBLANES)
        a1_parts += [jnp.where(r8 == 0, p1, sh1[head]), sh1[lo + SUBLANES:hi]]
        a2_parts += [jnp.where(r8 == 0, p0, jnp.where(r8 == 1, p1, sh2[head])),
                     sh2[lo + SUBLANES:hi]]
    a1 = jnp.concatenate(a1_parts, axis=0)
    a2 = jnp.concatenate(a2_parts, axis=0)
    cw = conv_ref[...]
    ac = cw[CONV_W:CONV_W + 1, :] + a2 * cw[0:1, :] + a1 * cw[1:2, :] + a * cw[2:3, :]

    if tiles_per_seq > 1:
        carry_ref[j] = a[tm - (CONV_W - 1):, :]
    for s in range(nseq):
        newc_ref[s] = a[(s + 1) * seq_rows - (CONV_W - 1):(s + 1) * seq_rows, :]

    y = (_silu(ac) * u).astype(BF16)
    out_ref[...] += jnp.dot(y, wd_ref[...], preferred_element_type=F32)
    _run_cast_jobs(jobs, src_refs, dst_refs)

    if final_norm:
        @pl.when(j == pl.num_programs(1) - 1)
        def _():
            rows = min(NORM_ROWS, tm)

            def body(c, carry):
                r = pl.multiple_of(c * rows, rows)
                x = out_ref[pl.ds(r, rows), :]
                ms = jnp.mean(x * x, axis=-1, keepdims=True)
                out_ref[pl.ds(r, rows), :] = (x * lax.rsqrt(ms + EPS)) * nf_ref[...]
                return carry

            lax.fori_loop(0, tm // rows, body, 0)


def _ffn(h, nw, up, conv, wd, cache, nf, jobs=(), *, bsz, seq, tm, tf):
    m, d = h.shape
    dffp = up.shape[2]
    nj = dffp // tf
    jobs = list(jobs)
    if seq >= tm:
        tiles_per_seq = seq // tm
        seq_rows = tm
        seqs_per_tile = 1
    else:
        tiles_per_seq = 1
        seq_rows = seq
        seqs_per_tile = tm // seq
    assert seq_rows & (seq_rows - 1) == 0
    grid = (m // tm, nj)
    in_specs = [
        pl.BlockSpec((tm, d), lambda i, j: (i, 0)),
        pl.BlockSpec((1, d), lambda i, j: (0, 0)),
        pl.BlockSpec((2, d, tf), lambda i, j: (0, 0, j)),
        pl.BlockSpec((CONV_W + 1, tf), lambda i, j: (0, j)),
        pl.BlockSpec((tf, d), lambda i, j: (j, 0)),
    ]
    args = [h, nw.reshape(1, d), up, conv, wd]
    if cache is not None:
        assert tiles_per_seq == 1
        in_specs.append(pl.BlockSpec((seqs_per_tile, CONV_W - 1, tf), lambda i, j: (i, 0, j)))
        args.append(cache)
    if nf is not None:
        in_specs.append(pl.BlockSpec((1, d), lambda i, j: (0, 0)))
        args.append(nf.reshape(1, d))
    newc_spec = pl.BlockSpec((seqs_per_tile, CONV_W - 1, tf), lambda i, j: (i, 0, j))
    job_io = [_cast_job_io(job, grid) for job in jobs]
    in_specs += [io[0] for io in job_io]
    args += [job["src"] for job in jobs]
    body = functools.partial(
        _ffn_kernel, seq_rows=seq_rows, tiles_per_seq=tiles_per_seq,
        has_cache=cache is not None, final_norm=nf is not None, jobs=jobs)
    outs = pl.pallas_call(
        body,
        grid=grid,
        in_specs=in_specs,
        out_specs=[pl.BlockSpec((tm, d), lambda i, j: (i, 0)), newc_spec] + [io[1] for io in job_io],
        out_shape=[
            jax.ShapeDtypeStruct((m, d), F32),
            jax.ShapeDtypeStruct((grid[0] * seqs_per_tile, CONV_W - 1, dffp), F32),
        ] + [io[2] for io in job_io],
        scratch_shapes=[
            pltpu.VMEM((tm, d), BF16),
            pltpu.VMEM((nj, CONV_W - 1, tf), F32),
        ],
        compiler_params=_cparams(2),
        name="ffn",
    )(*args)
    out, newc = outs[0], outs[1]
    if tiles_per_seq > 1:
        newc = newc.reshape(bsz, tiles_per_seq, CONV_W - 1, dffp)[:, -1]
    return out, newc, list(outs[2:])


def _pad_to(x, axis, size):
    pad = size - x.shape[axis]
    if pad == 0:
        return x
    widths = [(0, 0)] * x.ndim
    widths[axis] = (0, pad)
    return jnp.pad(x, widths)


def _rotary_tables(pos):
    half = DK_A // 2
    inv = ROPE_BASE ** (-jnp.arange(half, dtype=F32) / half)
    ang = pos.astype(F32)[:, None] * inv[None, :]
    cos, sin = jnp.cos(ang), jnp.sin(ang)
    return jnp.concatenate([cos, cos], axis=-1), jnp.concatenate([-sin, sin], axis=-1)


FFN_TILE = 512


def _in_proj_job(raw, l, n_steps):
    if l % 2 == 0:
        src = raw["w_in_even"]
        rb = _row_block(src.shape[1], src.shape[1], n_steps)
        return _cast_job(src, (l // 2,), src.shape[1], src.shape[2], rb)
    src = jnp.swapaxes(raw["w_in_odd"], 1, 2)
    return _transpose_cast_job(src, (l // 2,), src.shape[1], src.shape[2], LANES)


def _initial_weights(raw):
    w = {("in", 0): raw["w_in_even"][0].astype(BF16)}
    depth = raw["norm_mix"].shape[0]
    for l in range(1, depth, 2):
        w[("gk2", l)] = _pad_to(raw["w_gk2"][l // 2], 0, LANES).astype(BF16)
    return w


def _cast_now(job):
    src = job["src"][job["lead"]]
    if job["transpose"]:
        return _pad_to(src.T.astype(BF16), 1, job["n_dst"] * job["rb"])
    rows_out = job["n_dst"] * job["rb"]
    parts = [_pad_to(_pad_to(src[:, s * job["cin"]:(s + 1) * job["cin"]].astype(BF16), 1, job["cout"]),
                     0, rows_out) for s in range(job["halves"])]
    return jnp.stack(parts) if job["halves"] > 1 else parts[0]


def _missing_jobs(w, wanted, n_steps):
    keys, jobs = [], []
    for key, job in wanted:
        if key in w:
            continue
        if job["n_dst"] <= n_steps:
            keys.append(key)
            jobs.append(job)
        else:
            w[key] = _cast_now(job)
    return keys, jobs


def _row_block(rows, rows_out, n_steps):
    candidates = [BF16_TILE_ROWS << s for s in range(4)]
    fits = [rb for rb in candidates if rows % rb == 0 and rows_out % rb == 0]
    return next((rb for rb in fits if rows_out // rb <= n_steps), fits[-1])


def _layer_weight_jobs(raw, l, dffp, n_steps):
    w_out = raw["w_out_even"] if l % 2 == 0 else raw["w_out_odd"]
    up, down = raw["ffn_w_up"], raw["ffn_w_down"]
    dff = down.shape[1]
    d = up.shape[1]
    rb, rb_down = _row_block(d, d, n_steps), _row_block(dff, dffp, n_steps)
    return [
        (("out", l), _cast_job(w_out, (l // 2,), w_out.shape[1], w_out.shape[2], rb)),
        (("up", l), _cast_job(up, (l,), d, dff, rb, halves=2, cout=dffp)),
        (("down", l), _cast_job(down, (l,), dff, down.shape[2], rb_down, rows_out=dffp)),
    ]


def _tiles(m, seq):
    return min(1024, m), 1024, min(512, m), 2048, min(512, m), min(512, seq)


def _run_group(x, pos0, s_ret, s_hgrn, s_gla, s_conv, raw, w):
    bsz, seq, d = x.shape
    m = bsz * seq
    depth = raw["norm_mix"].shape[0]
    dff = raw["ffn_w_down"].shape[1]
    dffp = -(-dff // FFN_TILE) * FFN_TILE
    tm, tn, tm_out, tn_out, tm_ffn, tt = _tiles(m, seq)
    cos2, sin2 = _rotary_tables(pos0 + jnp.arange(seq, dtype=jnp.int32))
    conv = _pad_to(jnp.concatenate([raw["ffn_conv_w"], raw["ffn_conv_b"][:, None, :]], axis=1), 2, dffp)

    h = x.reshape(m, d)
    new_ret, new_hgrn, new_gla, new_conv = [], [], [], []
    for l in range(depth):
        i = l // 2
        w_in = w[("in", l)]
        n_low = 0 if l % 2 == 0 else LANES
        n = w_in.shape[1] - n_low
        n_steps = (m // tm) * (n // tn)
        rec_steps = bsz * (seq // tt)
        wanted = dict(_layer_weight_jobs(raw, l, dffp, n_steps))
        wanted_rec = dict(_layer_weight_jobs(raw, l, dffp, rec_steps))
        in_rec = [("up", l)] if l % 2 == 0 and wanted_rec[("up", l)]["n_dst"] <= rec_steps else []
        keys, jobs = _missing_jobs(w, [kv for kv in wanted.items() if kv[0] not in in_rec], n_steps)
        p, low, casts = _norm_matmul(h, raw["norm_mix"][l], w_in, n, n_low, jobs, tm=tm, tn=tn)
        w.update(zip(keys, casts))
        if l % 2 == 0:
            o_a, sr = _retention(p, cos2, sin2, None if s_ret is None else s_ret[i], bsz, seq, tt)
            keys, jobs = _missing_jobs(w, [(key, wanted_rec[key]) for key in in_rec], rec_steps)
            o_b, sh, casts = _hgrn(p, raw["hgrn_lb"], raw["hgrn_gnorm"][i],
                                   None if s_hgrn is None else s_hgrn[i], bsz, seq, tt, l, jobs)
            w.update(zip(keys, casts))
            o_list = [o_a, o_b]
            new_ret.append(sr)
            new_hgrn.append(sh)
        else:
            o_c, sg = _gla(p, low, w[("gk2", l)], raw["b_gk2"][i], raw["gla_gnorm"][i],
                           None if s_gla is None else s_gla[i], bsz, seq, tt)
            o_list = [o_c]
            new_gla.append(sg)
        h = _matmul_res(o_list, w[("out", l)], h, tm=tm_out, tn=tn_out)

        n_steps = (m // tm_ffn) * (dffp // FFN_TILE)
        wanted = []
        if l + 1 < depth:
            wanted = ([(("in", l + 1), _in_proj_job(raw, l + 1, n_steps))]
                      + _layer_weight_jobs(raw, l + 1, dffp, n_steps))
        keys, jobs = _missing_jobs(w, wanted, n_steps)
        cache = None if s_conv is None else _pad_to(s_conv[l], 2, dffp)
        h, nc, casts = _ffn(h, raw["norm_ffn"][l], w[("up", l)], conv[l], w[("down", l)], cache,
                            raw["norm_final"] if l == depth - 1 else None, jobs,
                            bsz=bsz, seq=seq, tm=tm_ffn, tf=FFN_TILE)
        w.update(zip(keys, casts))
        new_conv.append(nc[:, :, :dff])
    return (h.reshape(bsz, seq, d), jnp.stack(new_ret), jnp.stack(new_hgrn),
            jnp.stack(new_gla), jnp.stack(new_conv))


def kernel(x_prompt, x_sample, state_ret, state_hgrn, state_gla, cache_ffn_conv, norm_mix, norm_ffn, norm_final, w_in_even, w_out_even, hgrn_lb, hgrn_gnorm, w_in_odd, w_gk2, b_gk2, gla_gnorm, w_out_odd, ffn_w_up, ffn_conv_w, ffn_conv_b, ffn_w_down):
    raw = dict(norm_mix=norm_mix, norm_ffn=norm_ffn, norm_final=norm_final, w_in_even=w_in_even,
               w_out_even=w_out_even, hgrn_lb=hgrn_lb, hgrn_gnorm=hgrn_gnorm, w_in_odd=w_in_odd,
               w_gk2=w_gk2, b_gk2=b_gk2, gla_gnorm=gla_gnorm, w_out_odd=w_out_odd, ffn_w_up=ffn_w_up,
               ffn_conv_w=ffn_conv_w, ffn_conv_b=ffn_conv_b, ffn_w_down=ffn_w_down)
    w = _initial_weights(raw)
    y_p, ret_p, hgrn_p, gla_p, conv_p = _run_group(x_prompt, 0, None, None, None, None, raw, w)
    y_s, ret_s, hgrn_s, gla_s, conv_s = _run_group(
        x_sample, PAST_LEN, state_ret, state_hgrn, state_gla, cache_ffn_conv, raw, w)
    return (y_p, y_s, ret_p, ret_s, hgrn_p, hgrn_s, gla_p, gla_s, conv_p, conv_s)
```

```python
import functools

import numpy as np
import jax
import jax.numpy as jnp
from jax import lax
from jax.experimental import pallas as pl
from jax.experimental.pallas import tpu as pltpu

F32 = jnp.float32
BF16 = jnp.bfloat16

EPS = 1e-6
ROPE_BASE = 10000.0
GATE_NORMALIZER = 16.0
PAST_LEN = 1024

H_A, DK_A, DV_A = 4, 128, 256
H_B, DK_B, DV_B = 8, 128, 128
H_C, DK_C, DV_C = 4, 256, 512
GK_RANK = 16
CONV_W = 3

LANES = 128
SUBLANES = 8
BF16_TILE_ROWS = 16
VMEM_LIMIT_BYTES = 56 * 1024 * 1024
NORM_ROWS = 128

CHUNK = 64
SUB = 16
RET_CHUNK = 128
assert CHUNK % SUB == 0

NT_DIMS = (((1,), (1,)), ((), ()))
TN_DIMS = (((0,), (0,)), ((), ()))


def _cparams(n_axes):
    return pltpu.CompilerParams(
        dimension_semantics=("arbitrary",) * n_axes,
        vmem_limit_bytes=VMEM_LIMIT_BYTES,
    )


def _sigmoid(x):
    return 1.0 / (1.0 + jnp.exp(-x))


def _silu(x):
    return x * _sigmoid(x)


def _rmsnorm_rows_to(x_ref, nw_ref, dst_ref, rows, copy_ref=None):
    rows = min(rows, x_ref.shape[0])
    assert x_ref.shape[0] % rows == 0
    n = x_ref.shape[0] // rows

    def body(c, carry):
        r = pl.multiple_of(c * rows, rows)
        x = x_ref[pl.ds(r, rows), :]
        if copy_ref is not None:
            copy_ref[pl.ds(r, rows), :] = x
        ms = jnp.mean(x * x, axis=-1, keepdims=True)
        dst_ref[pl.ds(r, rows), :] = ((x * lax.rsqrt(ms + EPS)) * nw_ref[...]).astype(BF16)
        return carry

    lax.fori_loop(0, n, body, 0)


def _cast_job(src, lead, rows, cin, rb, *, halves=1, cout=None, rows_out=None):
    cout = cin if cout is None else cout
    rows_out = rows if rows_out is None else rows_out
    assert rows % rb == 0 and rows_out % rb == 0
    return dict(src=src, lead=tuple(lead), rb=rb, cin=cin, cout=cout, halves=halves,
                n_src=rows // rb, n_dst=rows_out // rb, transpose=False)


def _transpose_cast_job(src, lead, rows, cin, rb):
    n = -(-rows // rb)
    return dict(src=src, lead=tuple(lead), rb=rb, cin=cin, cout=cin, halves=1, rows=rows,
                n_src=n, n_dst=n, transpose=True)


def _cast_job_io(job, grid):
    assert job["n_dst"] <= grid[0] * grid[1]
    lead = job["lead"]
    rb, cin, cout, n_dst, halves = job["rb"], job["cin"], job["cout"], job["n_dst"], job["halves"]

    def src_index(i, j):
        return lead + (jnp.minimum(i * grid[1] + j, job["n_src"] - 1), 0)

    def dst_index(i, j):
        blk = jnp.minimum(i * grid[1] + j, n_dst - 1)
        if job["transpose"]:
            return (0, blk)
        return (0, blk, 0) if halves > 1 else (blk, 0)

    if job["transpose"]:
        dst_block, dst_shape = (cin, rb), (cin, n_dst * rb)
    elif halves > 1:
        dst_block, dst_shape = (halves, rb, cout), (halves, n_dst * rb, cout)
    else:
        dst_block, dst_shape = (rb, cout), (n_dst * rb, cout)
    return (pl.BlockSpec((None,) * len(lead) + (rb, halves * cin), src_index),
            pl.BlockSpec(dst_block, dst_index),
            jax.ShapeDtypeStruct(dst_shape, BF16))


def _run_cast_jobs(jobs, src_refs, dst_refs):
    step = pl.program_id(0) * pl.num_programs(1) + pl.program_id(1)
    for job, src_ref, dst_ref in zip(jobs, src_refs, dst_refs):
        if job["transpose"]:
            first_row = jnp.minimum(step, job["n_src"] - 1) * job["rb"]
            row = lax.broadcasted_iota(jnp.int32, (job["rb"], 1), 0)
            x = jnp.where(row < job["rows"] - first_row, src_ref[...], 0.0)
            dst_ref[...] = jnp.transpose(x).astype(BF16)
            continue
        cin, cout, halves = job["cin"], job["cout"], job["halves"]
        for s in range(halves):
            y = src_ref[:, s * cin:(s + 1) * cin].astype(BF16)
            if job["n_dst"] > job["n_src"]:
                y = jnp.where(step < job["n_src"], y, jnp.zeros_like(y))
            dst = dst_ref.at[s] if halves > 1 else dst_ref
            if cout > cin:
                dst[:, :cin] = y
                dst[:, cin:] = jnp.zeros((job["rb"], cout - cin), BF16)
            else:
                dst[...] = y


def _norm_matmul_kernel(*refs, has_low, jobs):
    n_in = 3 + has_low
    n_out = 1 + has_low
    nj = len(jobs)
    x_ref, nw_ref, w_ref = refs[:3]
    src_refs = refs[n_in:n_in + nj]
    o_ref = refs[n_in + nj]
    dst_refs = refs[n_in + nj + n_out:n_in + nj + n_out + nj]
    hn_ref = refs[-1]

    @pl.when(pl.program_id(1) == 0)
    def _():
        _rmsnorm_rows_to(x_ref, nw_ref, hn_ref, NORM_ROWS)
        if has_low:
            low_ref = refs[n_in + nj + 1]
            low_ref[...] = jnp.dot(hn_ref[...], refs[3][...], preferred_element_type=F32)

    o_ref[...] = jnp.dot(hn_ref[...], w_ref[...], preferred_element_type=F32)
    _run_cast_jobs(jobs, src_refs, dst_refs)


def _norm_matmul(x, nw, w, n, n_low=0, jobs=(), *, tm, tn):
    m, d = x.shape
    assert n % tn == 0 and n + n_low <= w.shape[1]
    w_low = w if n_low else None
    grid = (m // tm, n // tn)
    jobs = list(jobs)
    job_io = [_cast_job_io(job, grid) for job in jobs]
    in_specs = [
        pl.BlockSpec((tm, d), lambda i, j: (i, 0)),
        pl.BlockSpec((1, d), lambda i, j: (0, 0)),
        pl.BlockSpec((d, tn), lambda i, j: (0, j)),
    ]
    out_shape = [jax.ShapeDtypeStruct((m, n), F32)]
    out_specs = [pl.BlockSpec((tm, tn), lambda i, j: (i, j))]
    args = [x, nw.reshape(1, d), w]
    if w_low is not None:
        nl = n_low
        assert n % nl == 0
        in_specs.append(pl.BlockSpec((d, nl), lambda i, j: (0, n // nl)))
        out_shape.append(jax.ShapeDtypeStruct((m, nl), F32))
        out_specs.append(pl.BlockSpec((tm, nl), lambda i, j: (i, 0)))
        args.append(w_low)
    in_specs += [io[0] for io in job_io]
    out_specs += [io[1] for io in job_io]
    out_shape += [io[2] for io in job_io]
    args += [job["src"] for job in jobs]
    outs = pl.pallas_call(
        functools.partial(_norm_matmul_kernel, has_low=w_low is not None, jobs=jobs),
        grid=grid,
        in_specs=in_specs,
        out_specs=out_specs,
        out_shape=out_shape,
        scratch_shapes=[pltpu.VMEM((tm, d), BF16)],
        compiler_params=_cparams(2),
        name="norm_matmul",
    )(*args)
    n_main = 1 + (w_low is not None)
    return outs[0], (outs[1] if w_low is not None else None), list(outs[n_main:])


def _matmul_res_kernel(*refs, n_in):
    a_refs = refs[:n_in]
    w_refs = refs[n_in:2 * n_in]
    res_ref = refs[2 * n_in]
    o_ref = refs[2 * n_in + 1]
    acc = res_ref[...]
    for a_ref, w_ref in zip(a_refs, w_refs):
        acc = acc + jnp.dot(a_ref[...], w_ref[...], preferred_element_type=F32)
    o_ref[...] = acc


def _matmul_res(a_list, w, res, *, tm, tn):
    m, n = res.shape
    ka = a_list[0].shape[1]
    assert all(a.shape[1] == ka for a in a_list) and ka * len(a_list) == w.shape[0]
    grid = (m // tm, n // tn)
    in_specs = [pl.BlockSpec((tm, ka), lambda i, j: (i, 0)) for _ in a_list]
    for rb in range(len(a_list)):
        in_specs.append(pl.BlockSpec((ka, tn), lambda i, j, rb=rb: (rb, j)))
    in_specs.append(pl.BlockSpec((tm, tn), lambda i, j: (i, j)))
    return pl.pallas_call(
        functools.partial(_matmul_res_kernel, n_in=len(a_list)),
        grid=grid,
        in_specs=in_specs,
        out_specs=pl.BlockSpec((tm, tn), lambda i, j: (i, j)),
        out_shape=jax.ShapeDtypeStruct((m, n), F32),
        compiler_params=_cparams(2),
        name="matmul_res",
    )(*a_list, *([w] * len(a_list)), res)


def _local_cumsum(lg):
    c, k = lg.shape
    x = lg.reshape(c // SUBLANES, SUBLANES, k)
    pos = lax.broadcasted_iota(jnp.int32, x.shape, 1)
    s = 1
    while s < SUBLANES:
        x = x + jnp.where(pos >= s, pltpu.roll(x, s, 1), 0.0)
        s *= 2
    per = SUB // SUBLANES
    x = x.reshape(c // SUB, per, SUBLANES, k)
    groups = [x[:, 0]]
    for g in range(1, per):
        groups.append(x[:, g] + groups[-1][:, SUBLANES - 1:SUBLANES, :])
    return jnp.stack(groups, axis=1).reshape(c, k)


def _chunk_operands(q, k, bl, tot, sub):
    ns = q.shape[0] // sub
    a = [jnp.zeros_like(tot[0])]
    for m in range(ns):
        a.append(a[-1] + tot[m])
    b_last = a[ns]

    qt, kbar, khat, qin, kout = [], [], [], [], []
    qd = {d: [] for d in range(2, ns)}
    for m in range(ns):
        sl = slice(m * sub, (m + 1) * sub)
        blm = bl[sl]
        e_in = jnp.exp(blm)
        qtm = q[sl] * e_in
        kbm = k[sl] * (1.0 / e_in)
        khm = kbm * jnp.exp(tot[m])
        qt.append(qtm)
        kbar.append(kbm)
        khat.append(khm)
        qin.append(qtm * jnp.exp(a[m]))
        kout.append(khm * jnp.exp(b_last - a[m + 1]))
        for d in range(2, ns):
            if m >= d:
                qd[d].append(qtm * jnp.exp(a[m] - a[m - d + 1]))
            else:
                qd[d].append(jnp.zeros_like(qtm))

    def cat(xs):
        return jnp.concatenate(xs, axis=0).astype(BF16)

    return dict(qt=cat(qt), kbar=cat(kbar), khat=cat(khat), qin=cat(qin), kout=cat(kout),
                qd=[cat(qd[d]) for d in range(2, ns)], decay=jnp.exp(b_last))


def _chunk_steps(chunks, st_ref, sub=SUB):
    c = chunks[0][0][0].shape[0]
    ns = c // sub
    shift = sub.bit_length() - 1
    assert 1 << shift == sub and c % sub == 0
    ops = [[_chunk_operands(q, k, bl, tot, sub) for (q, k, _, bl, tot) in heads] for heads in chunks]
    raw = []
    for chunk_ops in ops:
        for op in chunk_ops:
            s = [lax.dot_general(op["qt"], op["kbar"], NT_DIMS, preferred_element_type=F32)]
            if ns > 1:
                s.append(lax.dot_general(op["qt"], op["khat"], NT_DIMS, preferred_element_type=F32))
            s += [lax.dot_general(qd, op["khat"], NT_DIMS, preferred_element_type=F32)
                  for qd in op["qd"]]
            raw.append(s)
    ri = lax.broadcasted_iota(jnp.int32, (c, c), 0)
    ci = lax.broadcasted_iota(jnp.int32, (c, c), 1)
    dist = (ri >> shift) - (ci >> shift)
    intra = []
    for s, (_, _, vb, _, _) in zip(raw, [head for heads in chunks for head in heads]):
        scores = jnp.where((dist == 0) & (ri >= ci), s[0], 0.0)
        for d in range(1, ns):
            scores = jnp.where(dist == d, s[d], scores)
        intra.append(jnp.dot(scores.astype(BF16), vb, preferred_element_type=F32))
    outs = []
    for g, (heads, chunk_ops) in enumerate(zip(chunks, ops)):
        outs.append([
            intra[g * len(heads) + h] + lax.dot_general(
                op["qin"], st_ref[h].astype(BF16), NT_DIMS, preferred_element_type=F32)
            for h, op in enumerate(chunk_ops)])
        for h, op in enumerate(chunk_ops):
            st_ref[h] = st_ref[h] * op["decay"] + lax.dot_general(
                heads[h][2], op["kout"], TN_DIMS, preferred_element_type=F32)
    return outs


def _sub_totals(bl):
    return [bl[(m + 1) * SUB - 1:(m + 1) * SUB, :] for m in range(bl.shape[0] // SUB)]


def _state_prologue(s0_ref, st_ref):
    @pl.when(pl.program_id(1) == 0)
    def _():
        for h in range(st_ref.shape[0]):
            if s0_ref is None:
                st_ref[h] = jnp.zeros(st_ref.shape[1:], F32)
            else:
                st_ref[h] = jnp.transpose(s0_ref[0, h])


def _state_epilogue(sn_ref, st_ref):
    @pl.when(pl.program_id(1) == pl.num_programs(1) - 1)
    def _():
        for h in range(st_ref.shape[0]):
            sn_ref[0, h] = jnp.transpose(st_ref[h])


def _chunk_loop(n_rows, chunk, body, together, unroll):
    assert n_rows % chunk == 0
    n_chunks = n_rows // chunk
    together = min(together, n_chunks)
    assert n_chunks % together == 0
    trips = n_chunks // together

    def step(c, carry):
        body([pl.ds(pl.multiple_of((c * together + p) * chunk, chunk), chunk) for p in range(together)])
        return carry

    lax.fori_loop(0, trips, step, 0, unroll=min(unroll, trips))


def _head(ref, rows, h, width):
    return ref[rows, h * width:(h + 1) * width]


def _ret_chunks(chunks, st_ref):
    chunk = chunks[0][2].shape[0]
    pos1 = (lax.broadcasted_iota(jnp.int32, (chunk, 1), 0) + 1).astype(F32)
    half = DK_A // 2
    inputs = []
    for get, _, cos, sin in chunks:
        heads = []
        for h in range(H_A):
            lgam = float(np.log1p(-np.exp2(-5.0 - h)))
            bl = pos1 * lgam
            tot = [jnp.full((1, 1), lgam * chunk, F32)]
            qr, kr = get("q", h), get("k", h)
            q = qr * cos + pltpu.roll(qr, half, 1) * sin
            k = (kr * cos + pltpu.roll(kr, half, 1) * sin) * (DK_A ** -0.5)
            heads.append((q, k, get("v", h).astype(BF16), bl, tot))
        inputs.append(heads)
    for (get, put, _, _), outs in zip(chunks, _chunk_steps(inputs, st_ref, sub=chunk)):
        for h, o in enumerate(outs):
            mu = jnp.mean(o, axis=-1, keepdims=True)
            oc = o - mu
            var = jnp.mean(oc * oc, axis=-1, keepdims=True)
            put(h, ((oc * lax.rsqrt(var + EPS)) * _silu(get("g", h))).astype(BF16))


def _hgrn_lower_bound(lbp_ref, layer):
    lbp = lbp_ref[...]
    e = jnp.exp(lbp - jnp.max(lbp, axis=0, keepdims=True))
    sm = e / jnp.sum(e, axis=0, keepdims=True)
    return jnp.sum(sm[:layer + 1], axis=0, keepdims=True)


def _gated_rmsnorm_out(chunks, outs_per_chunk, gn):
    for (get, put, *_), outs in zip(chunks, outs_per_chunk):
        for h, o in enumerate(outs):
            ms = jnp.mean(o * o, axis=-1, keepdims=True)
            put(h, (((o * lax.rsqrt(ms + EPS)) * gn) * _silu(get("g", h))).astype(BF16))


def _hgrn_chunks(chunks, lb_all, gn, st_ref):
    inputs = []
    for get, _ in chunks:
        heads = []
        for h in range(H_B):
            lb = lb_all[:, h * DK_B:(h + 1) * DK_B]
            q = _silu(get("q", h)) * (DK_B ** -0.5)
            f = lb + (1.0 - lb) * _sigmoid(get("f", h))
            bl = _local_cumsum(jnp.log(f))
            heads.append((q, 1.0 - f, get("i", h).astype(BF16), bl, _sub_totals(bl)))
        inputs.append(heads)
    _gated_rmsnorm_out(chunks, _chunk_steps(inputs, st_ref), gn)


def _gla_chunks(chunks, wgk, bgk, gn, st_ref):
    inputs = []
    for get, _, low in chunks:
        x = jnp.dot(low.astype(BF16), wgk, preferred_element_type=F32) + bgk
        lg_all = (jnp.minimum(x, 0.0) - jnp.log1p(jnp.exp(-jnp.abs(x)))) * (1.0 / GATE_NORMALIZER)
        heads = []
        for h in range(H_C):
            bl = _local_cumsum(lg_all[:, h * DK_C:(h + 1) * DK_C])
            q = get("q", h) * (DK_C ** -0.5)
            heads.append((q, get("k", h), get("v", h).astype(BF16), bl, _sub_totals(bl)))
        inputs.append(heads)
    _gated_rmsnorm_out(chunks, _chunk_steps(inputs, st_ref), gn)


def _ref_getter(refs, widths, rows):
    return lambda name, h: _head(refs[name], rows, h, widths[name])


def _ref_putter(o_ref, width, rows):
    def put(h, x):
        o_ref[rows, h * width:(h + 1) * width] = x
    return put


def _ret_kernel(*refs, has_s0):
    if has_s0:
        q_ref, k_ref, v_ref, g_ref, cos_ref, sin_ref, s0_ref, o_ref, sn_ref, st_ref = refs
    else:
        q_ref, k_ref, v_ref, g_ref, cos_ref, sin_ref, o_ref, sn_ref, st_ref = refs
        s0_ref = None
    _state_prologue(s0_ref, st_ref)
    srcs = {"q": q_ref, "k": k_ref, "v": v_ref, "g": g_ref}
    widths = {"q": DK_A, "k": DK_A, "v": DV_A, "g": DV_A}

    def body(rows_list):
        _ret_chunks([(_ref_getter(srcs, widths, rows), _ref_putter(o_ref, DV_A, rows),
                      cos_ref[rows, :], sin_ref[rows, :]) for rows in rows_list], st_ref)

    n_rows = q_ref.shape[0]
    _chunk_loop(n_rows, min(RET_CHUNK, n_rows), body, together=2, unroll=1)
    _state_epilogue(sn_ref, st_ref)


def _hgrn_kernel(*refs, has_s0, layer, jobs):
    q_ref, f_ref, i_ref, g_ref, lbp_ref, gn_ref = refs[:6]
    pos = 6
    s0_ref = refs[pos] if has_s0 else None
    pos += has_s0
    src_refs = refs[pos:pos + len(jobs)]
    o_ref, sn_ref = refs[pos + len(jobs):pos + len(jobs) + 2]
    dst_refs = refs[pos + len(jobs) + 2:pos + 2 * len(jobs) + 2]
    st_ref = refs[-1]
    _run_cast_jobs(jobs, src_refs, dst_refs)
    _state_prologue(s0_ref, st_ref)
    lb_all = _hgrn_lower_bound(lbp_ref, layer)
    gn = gn_ref[...]
    srcs = {"q": q_ref, "f": f_ref, "i": i_ref, "g": g_ref}
    widths = {"q": DK_B, "f": DK_B, "i": DV_B, "g": DV_B}

    def body(rows_list):
        _hgrn_chunks([(_ref_getter(srcs, widths, rows), _ref_putter(o_ref, DV_B, rows))
                      for rows in rows_list], lb_all, gn, st_ref)

    _chunk_loop(q_ref.shape[0], CHUNK, body, together=1, unroll=2)
    _state_epilogue(sn_ref, st_ref)


def _gla_kernel(*refs, has_s0):
    if has_s0:
        q_ref, k_ref, v_ref, g_ref, low_ref, wgk_ref, bgk_ref, gn_ref, s0_ref, o_ref, sn_ref, st_ref = refs
    else:
        q_ref, k_ref, v_ref, g_ref, low_ref, wgk_ref, bgk_ref, gn_ref, o_ref, sn_ref, st_ref = refs
        s0_ref = None
    _state_prologue(s0_ref, st_ref)
    gn = gn_ref[...]
    bgk = bgk_ref[...]
    srcs = {"q": q_ref, "k": k_ref, "v": v_ref, "g": g_ref}
    widths = {"q": DK_C, "k": DK_C, "v": DV_C, "g": DV_C}

    def body(rows_list):
        _gla_chunks([(_ref_getter(srcs, widths, rows), _ref_putter(o_ref, DV_C, rows), low_ref[rows, :])
                     for rows in rows_list], wgk_ref[...], bgk, gn, st_ref)

    _chunk_loop(q_ref.shape[0], CHUNK, body, together=4, unroll=1)
    _state_epilogue(sn_ref, st_ref)


def _row_spec(tt, width, nt, block):
    return pl.BlockSpec((tt, width), lambda b, t: (b * nt + t, block))


def _recurrence_call(body, in_specs, args, s0, bsz, seq, heads, dk, dv, tt, jobs=()):
    nt = seq // tt
    if s0 is not None:
        in_specs = in_specs + [pl.BlockSpec((1, heads, dk, dv), lambda b, t: (b, 0, 0, 0))]
        args = args + [s0]
    job_io = [_cast_job_io(job, (bsz, nt)) for job in jobs]
    outs = pl.pallas_call(
        body,
        grid=(bsz, nt),
        in_specs=in_specs + [io[0] for io in job_io],
        out_specs=[
            pl.BlockSpec((tt, heads * dv), lambda b, t: (b * nt + t, 0)),
            pl.BlockSpec((1, heads, dk, dv), lambda b, t: (b, 0, 0, 0)),
        ] + [io[1] for io in job_io],
        out_shape=[
            jax.ShapeDtypeStruct((bsz * seq, heads * dv), BF16),
            jax.ShapeDtypeStruct((bsz, heads, dk, dv), F32),
        ] + [io[2] for io in job_io],
        scratch_shapes=[pltpu.VMEM((heads, dv, dk), F32)],
        compiler_params=_cparams(2),
        name="recurrence",
    )(*args, *[job["src"] for job in jobs])
    return outs[0], outs[1], list(outs[2:])


def _retention(p, cos2, sin2, s0, bsz, seq, tt):
    nt = seq // tt
    qa, va = H_A * DK_A, H_A * DV_A
    in_specs = [
        _row_spec(tt, qa, nt, 0),
        _row_spec(tt, qa, nt, 1),
        _row_spec(tt, va, nt, 2 * qa // va),
        _row_spec(tt, va, nt, (2 * qa + va) // va),
        pl.BlockSpec((tt, DK_A), lambda b, t: (t, 0)),
        pl.BlockSpec((tt, DK_A), lambda b, t: (t, 0)),
    ]
    args = [p, p, p, p, cos2, sin2]
    body = functools.partial(_ret_kernel, has_s0=s0 is not None)
    return _recurrence_call(body, in_specs, args, s0, bsz, seq, H_A, DK_A, DV_A, tt)[:2]


def _hgrn(p, lbp, gnorm, s0, bsz, seq, tt, layer, jobs=()):
    nt = seq // tt
    base = 2 * H_A * DK_A + 2 * H_A * DV_A
    qb = H_B * DK_B
    assert base % qb == 0 and H_B * DV_B == qb
    in_specs = [_row_spec(tt, qb, nt, base // qb + n) for n in range(4)]
    in_specs += [
        pl.BlockSpec(lbp.shape, lambda b, t: (0, 0)),
        pl.BlockSpec((1, DV_B), lambda b, t: (0, 0)),
    ]
    args = [p, p, p, p, lbp, gnorm.reshape(1, DV_B)]
    jobs = list(jobs)
    body = functools.partial(_hgrn_kernel, has_s0=s0 is not None, layer=layer, jobs=jobs)
    return _recurrence_call(body, in_specs, args, s0, bsz, seq, H_B, DK_B, DV_B, tt, jobs)


def _gla(p, low, wgk, bgk, gnorm, s0, bsz, seq, tt):
    nt = seq // tt
    qc, vc = H_C * DK_C, H_C * DV_C
    in_specs = [
        _row_spec(tt, qc, nt, 0),
        _row_spec(tt, qc, nt, 1),
        _row_spec(tt, vc, nt, 2 * qc // vc),
        _row_spec(tt, vc, nt, (2 * qc + vc) // vc),
        _row_spec(tt, low.shape[1], nt, 0),
        pl.BlockSpec(wgk.shape, lambda b, t: (0, 0)),
        pl.BlockSpec((1, qc), lambda b, t: (0, 0)),
        pl.BlockSpec((1, DV_C), lambda b, t: (0, 0)),
    ]
    args = [p, p, p, p, low, wgk, bgk.reshape(1, qc), gnorm.reshape(1, DV_C)]
    body = functools.partial(_gla_kernel, has_s0=s0 is not None)
    return _recurrence_call(body, in_specs, args, s0, bsz, seq, H_C, DK_C, DV_C, tt)[:2]


def _ffn_kernel(*refs, seq_rows, tiles_per_seq, has_cache, final_norm, jobs):
    refs = list(refs)
    h_ref, nw_ref, up_ref, conv_ref, wd_ref = refs[:5]
    pos = 5
    cache_ref = None
    nf_ref = None
    if has_cache:
        cache_ref = refs[pos]
        pos += 1
    if final_norm:
        nf_ref = refs[pos]
        pos += 1
    nj = len(jobs)
    src_refs = refs[pos:pos + nj]
    out_ref, newc_ref = refs[pos + nj:pos + nj + 2]
    dst_refs = refs[pos + nj + 2:pos + 2 * nj + 2]
    hn_ref, carry_ref = refs[pos + 2 * nj + 2:]

    i = pl.program_id(0)
    j = pl.program_id(1)
    tm, tf = h_ref.shape[0], wd_ref.shape[0]

    @pl.when(j == 0)
    def _():
        _rmsnorm_rows_to(h_ref, nw_ref, hn_ref, NORM_ROWS, copy_ref=out_ref)

    if tiles_per_seq > 1:
        @pl.when(i % tiles_per_seq == 0)
        def _():
            carry_ref[j] = jnp.zeros((CONV_W - 1, tf), F32)

    hn = hn_ref[...]
    a = jnp.dot(hn, up_ref[0], preferred_element_type=F32)
    u = jnp.dot(hn, up_ref[1], preferred_element_type=F32)

    nseq = tm // seq_rows
    r8 = lax.broadcasted_iota(jnp.int32, (SUBLANES, 1), 0)
    sh1 = pltpu.roll(a, 1, 0)
    sh2 = pltpu.roll(a, 2, 0)
    a1_parts, a2_parts = [], []
    for s in range(nseq):
        if tiles_per_seq > 1:
            p0, p1 = carry_ref[j, 0:1, :], carry_ref[j, 1:2, :]
        elif has_cache:
            p0, p1 = cache_ref[s, 0:1, :], cache_ref[s, 1:2, :]
        else:
            p0 = p1 = jnp.zeros((1, tf), F32)
        lo, hi = s * seq_rows, (s + 1) * seq_rows
        head = slice(lo, lo + SUBLANES)
        a1_parts += [jnp.where(r8 == 0, p1, sh1[head]), sh1[lo + SUBLANES:hi]]
        a2_parts += [jnp.where(r8 == 0, p0, jnp.where(r8 == 1, p1, sh2[head])),
                     sh2[lo + SUBLANES:hi]]
    a1 = jnp.concatenate(a1_parts, axis=0)
    a2 = jnp.concatenate(a2_parts, axis=0)
    cw = conv_ref[...]
    ac = cw[CONV_W:CONV_W + 1, :] + a2 * cw[0:1, :] + a1 * cw[1:2, :] + a * cw[2:3, :]

    if tiles_per_seq > 1:
        carry_ref[j] = a[tm - (CONV_W - 1):, :]
    for s in range(nseq):
        newc_ref[s] = a[(s + 1) * seq_rows - (CONV_W - 1):(s + 1) * seq_rows, :]

    y = (_silu(ac) * u).astype(BF16)
    out_ref[...] += jnp.dot(y, wd_ref[...], preferred_element_type=F32)
    _run_cast_jobs(jobs, src_refs, dst_refs)

    if final_norm:
        @pl.when(j == pl.num_programs(1) - 1)
        def _():
            rows = min(NORM_ROWS, tm)

            def body(c, carry):
                r = pl.multiple_of(c * rows, rows)
                x = out_ref[pl.ds(r, rows), :]
                ms = jnp.mean(x * x, axis=-1, keepdims=True)
                out_ref[pl.ds(r, rows), :] = (x * lax.rsqrt(ms + EPS)) * nf_ref[...]
                return carry

            lax.fori_loop(0, tm // rows, body, 0)


def _ffn(h, nw, up, conv, wd, cache, nf, jobs=(), *, bsz, seq, tm, tf):
    m, d = h.shape
    dffp = up.shape[2]
    nj = dffp // tf
    jobs = list(jobs)
    if seq >= tm:
        tiles_per_seq = seq // tm
        seq_rows = tm
        seqs_per_tile = 1
    else:
        tiles_per_seq = 1
        seq_rows = seq
        seqs_per_tile = tm // seq
    assert seq_rows & (seq_rows - 1) == 0
    grid = (m // tm, nj)
    in_specs = [
        pl.BlockSpec((tm, d), lambda i, j: (i, 0)),
        pl.BlockSpec((1, d), lambda i, j: (0, 0)),
        pl.BlockSpec((2, d, tf), lambda i, j: (0, 0, j)),
        pl.BlockSpec((CONV_W + 1, tf), lambda i, j: (0, j)),
        pl.BlockSpec((tf, d), lambda i, j: (j, 0)),
    ]
    args = [h, nw.reshape(1, d), up, conv, wd]
    if cache is not None:
        assert tiles_per_seq == 1
        in_specs.append(pl.BlockSpec((seqs_per_tile, CONV_W - 1, tf), lambda i, j: (i, 0, j)))
        args.append(cache)
    if nf is not None:
        in_specs.append(pl.BlockSpec((1, d), lambda i, j: (0, 0)))
        args.append(nf.reshape(1, d))
    newc_spec = pl.BlockSpec((seqs_per_tile, CONV_W - 1, tf), lambda i, j: (i, 0, j))
    job_io = [_cast_job_io(job, grid) for job in jobs]
    in_specs += [io[0] for io in job_io]
    args += [job["src"] for job in jobs]
    body = functools.partial(
        _ffn_kernel, seq_rows=seq_rows, tiles_per_seq=tiles_per_seq,
        has_cache=cache is not None, final_norm=nf is not None, jobs=jobs)
    outs = pl.pallas_call(
        body,
        grid=grid,
        in_specs=in_specs,
        out_specs=[pl.BlockSpec((tm, d), lambda i, j: (i, 0)), newc_spec] + [io[1] for io in job_io],
        out_shape=[
            jax.ShapeDtypeStruct((m, d), F32),
            jax.ShapeDtypeStruct((grid[0] * seqs_per_tile, CONV_W - 1, dffp), F32),
        ] + [io[2] for io in job_io],
        scratch_shapes=[
            pltpu.VMEM((tm, d), BF16),
            pltpu.VMEM((nj, CONV_W - 1, tf), F32),
        ],
        compiler_params=_cparams(2),
        name="ffn",
    )(*args)
    out, newc = outs[0], outs[1]
    if tiles_per_seq > 1:
        newc = newc.reshape(bsz, tiles_per_seq, CONV_W - 1, dffp)[:, -1]
    return out, newc, list(outs[2:])


def _pad_to(x, axis, size):
    pad = size - x.shape[axis]
    if pad == 0:
        return x
    widths = [(0, 0)] * x.ndim
    widths[axis] = (0, pad)
    return jnp.pad(x, widths)


def _rotary_tables(pos):
    half = DK_A // 2
    inv = ROPE_BASE ** (-jnp.arange(half, dtype=F32) / half)
    ang = pos.astype(F32)[:, None] * inv[None, :]
    cos, sin = jnp.cos(ang), jnp.sin(ang)
    return jnp.concatenate([cos, cos], axis=-1), jnp.concatenate([-sin, sin], axis=-1)


FFN_TILE = 512


def _in_proj_job(raw, l, n_steps):
    if l % 2 == 0:
        src = raw["w_in_even"]
        rb = _row_block(src.shape[1], src.shape[1], n_steps)
        return _cast_job(src, (l // 2,), src.shape[1], src.shape[2], rb)
    src = jnp.swapaxes(raw["w_in_odd"], 1, 2)
    return _transpose_cast_job(src, (l // 2,), src.shape[1], src.shape[2], LANES)


def _initial_weights(raw):
    w = {("in", 0): raw["w_in_even"][0].astype(BF16)}
    depth = raw["norm_mix"].shape[0]
    for l in range(1, depth, 2):
        w[("gk2", l)] = _pad_to(raw["w_gk2"][l // 2], 0, LANES).astype(BF16)
    return w


def _cast_now(job):
    src = job["src"][job["lead"]]
    if job["transpose"]:
        return _pad_to(src.T.astype(BF16), 1, job["n_dst"] * job["rb"])
    rows_out = job["n_dst"] * job["rb"]
    parts = [_pad_to(_pad_to(src[:, s * job["cin"]:(s + 1) * job["cin"]].astype(BF16), 1, job["cout"]),
                     0, rows_out) for s in range(job["halves"])]
    return jnp.stack(parts) if job["halves"] > 1 else parts[0]


def _missing_jobs(w, wanted, n_steps):
    keys, jobs = [], []
    for key, job in wanted:
        if key in w:
            continue
        if job["n_dst"] <= n_steps:
            keys.append(key)
            jobs.append(job)
        else:
            w[key] = _cast_now(job)
    return keys, jobs


def _row_block(rows, rows_out, n_steps):
    candidates = [BF16_TILE_ROWS << s for s in range(4)]
    fits = [rb for rb in candidates if rows % rb == 0 and rows_out % rb == 0]
    return next((rb for rb in fits if rows_out // rb <= n_steps), fits[-1])


def _layer_weight_jobs(raw, l, dffp, n_steps):
    w_out = raw["w_out_even"] if l % 2 == 0 else raw["w_out_odd"]
    up, down = raw["ffn_w_up"], raw["ffn_w_down"]
    dff = down.shape[1]
    d = up.shape[1]
    rb, rb_down = _row_block(d, d, n_steps), _row_block(dff, dffp, n_steps)
    return [
        (("out", l), _cast_job(w_out, (l // 2,), w_out.shape[1], w_out.shape[2], rb)),
        (("up", l), _cast_job(up, (l,), d, dff, rb, halves=2, cout=dffp)),
        (("down", l), _cast_job(down, (l,), dff, down.shape[2], rb_down, rows_out=dffp)),
    ]


def _tiles(m, seq):
    return min(1024, m), 1024, min(512, m), 2048, min(512, m), min(512, seq)


def _run_group(x, pos0, s_ret, s_hgrn, s_gla, s_conv, raw, w):
    bsz, seq, d = x.shape
    m = bsz * seq
    depth = raw["norm_mix"].shape[0]
    dff = raw["ffn_w_down"].shape[1]
    dffp = -(-dff // FFN_TILE) * FFN_TILE
    tm, tn, tm_out, tn_out, tm_ffn, tt = _tiles(m, seq)
    cos2, sin2 = _rotary_tables(pos0 + jnp.arange(seq, dtype=jnp.int32))
    conv = _pad_to(jnp.concatenate([raw["ffn_conv_w"], raw["ffn_conv_b"][:, None, :]], axis=1), 2, dffp)

    h = x.reshape(m, d)
    new_ret, new_hgrn, new_gla, new_conv = [], [], [], []
    for l in range(depth):
        i = l // 2
        w_in = w[("in", l)]
        n_low = 0 if l % 2 == 0 else LANES
        n = w_in.shape[1] - n_low
        n_steps = (m // tm) * (n // tn)
        rec_steps = bsz * (seq // tt)
        wanted = dict(_layer_weight_jobs(raw, l, dffp, n_steps))
        wanted_rec = dict(_layer_weight_jobs(raw, l, dffp, rec_steps))
        in_rec = [("up", l)] if l % 2 == 0 and wanted_rec[("up", l)]["n_dst"] <= rec_steps else []
        keys, jobs = _missing_jobs(w, [kv for kv in wanted.items() if kv[0] not in in_rec], n_steps)
        p, low, casts = _norm_matmul(h, raw["norm_mix"][l], w_in, n, n_low, jobs, tm=tm, tn=tn)
        w.update(zip(keys, casts))
        if l % 2 == 0:
            o_a, sr = _retention(p, cos2, sin2, None if s_ret is None else s_ret[i], bsz, seq, tt)
            keys, jobs = _missing_jobs(w, [(key, wanted_rec[key]) for key in in_rec], rec_steps)
            o_b, sh, casts = _hgrn(p, raw["hgrn_lb"], raw["hgrn_gnorm"][i],
                                   None if s_hgrn is None else s_hgrn[i], bsz, seq, tt, l, jobs)
            w.update(zip(keys, casts))
            o_list = [o_a, o_b]
            new_ret.append(sr)
            new_hgrn.append(sh)
        else:
            o_c, sg = _gla(p, low, w[("gk2", l)], raw["b_gk2"][i], raw["gla_gnorm"][i],
                           None if s_gla is None else s_gla[i], bsz, seq, tt)
            o_list = [o_c]
            new_gla.append(sg)
        h = _matmul_res(o_list, w[("out", l)], h, tm=tm_out, tn=tn_out)

        n_steps = (m // tm_ffn) * (dffp // FFN_TILE)
        wanted = []
        if l + 1 < depth:
            wanted = ([(("in", l + 1), _in_proj_job(raw, l + 1, n_steps))]
                      + _layer_weight_jobs(raw, l + 1, dffp, n_steps))
        keys, jobs = _missing_jobs(w, wanted, n_steps)
        cache = None if s_conv is None else _pad_to(s_conv[l], 2, dffp)
        h, nc, casts = _ffn(h, raw["norm_ffn"][l], w[("up", l)], conv[l], w[("down", l)], cache,
                            raw["norm_final"] if l == depth - 1 else None, jobs,
                            bsz=bsz, seq=seq, tm=tm_ffn, tf=FFN_TILE)
        w.update(zip(keys, casts))
        new_conv.append(nc[:, :, :dff])
    return (h.reshape(bsz, seq, d), jnp.stack(new_ret), jnp.stack(new_hgrn),
            jnp.stack(new_gla), jnp.stack(new_conv))


def kernel(x_prompt, x_sample, state_ret, state_hgrn, state_gla, cache_ffn_conv, norm_mix, norm_ffn, norm_final, w_in_even, w_out_even, hgrn_lb, hgrn_gnorm, w_in_odd, w_gk2, b_gk2, gla_gnorm, w_out_odd, ffn_w_up, ffn_conv_w, ffn_conv_b, ffn_w_down):
    raw = dict(norm_mix=norm_mix, norm_ffn=norm_ffn, norm_final=norm_final, w_in_even=w_in_even,
               w_out_even=w_out_even, hgrn_lb=hgrn_lb, hgrn_gnorm=hgrn_gnorm, w_in_odd=w_in_odd,
               w_gk2=w_gk2, b_gk2=b_gk2, gla_gnorm=gla_gnorm, w_out_odd=w_out_odd, ffn_w_up=ffn_w_up,
               ffn_conv_w=ffn_conv_w, ffn_conv_b=ffn_conv_b, ffn_w_down=ffn_w_down)
    w = _initial_weights(raw)
    y_p, ret_p, hgrn_p, gla_p, conv_p = _run_group(x_prompt, 0, None, None, None, None, raw, w)
    y_s, ret_s, hgrn_s, gla_s, conv_s = _run_group(
        x_sample, PAST_LEN, state_ret, state_hgrn, state_gla, cache_ffn_conv, raw, w)
    return (y_p, y_s, ret_p, ret_s, hgrn_p, hgrn_s, gla_p, gla_s, conv_p, conv_s)
```

```python
import functools

import numpy as np
import jax
import jax.numpy as jnp
from jax import lax
from jax.experimental import pallas as pl
from jax.experimental.pallas import tpu as pltpu

F32 = jnp.float32
BF16 = jnp.bfloat16

EPS = 1e-6
ROPE_BASE = 10000.0
GATE_NORMALIZER = 16.0
PAST_LEN = 1024

H_A, DK_A, DV_A = 4, 128, 256
H_B, DK_B, DV_B = 8, 128, 128
H_C, DK_C, DV_C = 4, 256, 512
GK_RANK = 16
CONV_W = 3

LANES = 128
SUBLANES = 8
BF16_TILE_ROWS = 16
VMEM_LIMIT_BYTES = 56 * 1024 * 1024
NORM_ROWS = 128

CHUNK = 64
SUB = 16
RET_CHUNK = 128
assert CHUNK % SUB == 0

NT_DIMS = (((1,), (1,)), ((), ()))
TN_DIMS = (((0,), (0,)), ((), ()))


def _cparams(n_axes):
    return pltpu.CompilerParams(
        dimension_semantics=("arbitrary",) * n_axes,
        vmem_limit_bytes=VMEM_LIMIT_BYTES,
    )


def _sigmoid(x):
    return 1.0 / (1.0 + jnp.exp(-x))


def _silu(x):
    return x * _sigmoid(x)


def _rmsnorm_rows_to(x_ref, nw_ref, dst_ref, rows, copy_ref=None):
    rows = min(rows, x_ref.shape[0])
    assert x_ref.shape[0] % rows == 0
    n = x_ref.shape[0] // rows

    def body(c, carry):
        r = pl.multiple_of(c * rows, rows)
        x = x_ref[pl.ds(r, rows), :]
        if copy_ref is not None:
            copy_ref[pl.ds(r, rows), :] = x
        ms = jnp.mean(x * x, axis=-1, keepdims=True)
        dst_ref[pl.ds(r, rows), :] = ((x * lax.rsqrt(ms + EPS)) * nw_ref[...]).astype(BF16)
        return carry

    lax.fori_loop(0, n, body, 0)


def _cast_job(src, lead, rows, cin, rb, *, halves=1, cout=None, rows_out=None):
    cout = cin if cout is None else cout
    rows_out = rows if rows_out is None else rows_out
    assert rows % rb == 0 and rows_out % rb == 0
    return dict(src=src, lead=tuple(lead), rb=rb, cin=cin, cout=cout, halves=halves,
                n_src=rows // rb, n_dst=rows_out // rb, transpose=False)


def _transpose_cast_job(src, lead, rows, cin, rb):
    n = -(-rows // rb)
    return dict(src=src, lead=tuple(lead), rb=rb, cin=cin, cout=cin, halves=1, rows=rows,
                n_src=n, n_dst=n, transpose=True)


def _cast_job_io(job, grid):
    assert job["n_dst"] <= grid[0] * grid[1]
    lead = job["lead"]
    rb, cin, cout, n_dst, halves = job["rb"], job["cin"], job["cout"], job["n_dst"], job["halves"]

    def src_index(i, j):
        return lead + (jnp.minimum(i * grid[1] + j, job["n_src"] - 1), 0)

    def dst_index(i, j):
        blk = jnp.minimum(i * grid[1] + j, n_dst - 1)
        if job["transpose"]:
            return (0, blk)
        return (0, blk, 0) if halves > 1 else (blk, 0)

    if job["transpose"]:
        dst_block, dst_shape = (cin, rb), (cin, n_dst * rb)
    elif halves > 1:
        dst_block, dst_shape = (halves, rb, cout), (halves, n_dst * rb, cout)
    else:
        dst_block, dst_shape = (rb, cout), (n_dst * rb, cout)
    return (pl.BlockSpec((None,) * len(lead) + (rb, halves * cin), src_index),
            pl.BlockSpec(dst_block, dst_index),
            jax.ShapeDtypeStruct(dst_shape, BF16))


def _run_cast_jobs(jobs, src_refs, dst_refs):
    step = pl.program_id(0) * pl.num_programs(1) + pl.program_id(1)
    for job, src_ref, dst_ref in zip(jobs, src_refs, dst_refs):
        if job["transpose"]:
            first_row = jnp.minimum(step, job["n_src"] - 1) * job["rb"]
            row = lax.broadcasted_iota(jnp.int32, (job["rb"], 1), 0)
            x = jnp.where(row < job["rows"] - first_row, src_ref[...], 0.0)
            dst_ref[...] = jnp.transpose(x).astype(BF16)
            continue
        cin, cout, halves = job["cin"], job["cout"], job["halves"]
        for s in range(halves):
            y = src_ref[:, s * cin:(s + 1) * cin].astype(BF16)
            if job["n_dst"] > job["n_src"]:
                y = jnp.where(step < job["n_src"], y, jnp.zeros_like(y))
            dst = dst_ref.at[s] if halves > 1 else dst_ref
            if cout > cin:
                dst[:, :cin] = y
                dst[:, cin:] = jnp.zeros((job["rb"], cout - cin), BF16)
            else:
                dst[...] = y


def _norm_matmul_kernel(*refs, has_low, jobs):
    n_in = 3 + has_low
    n_out = 1 + has_low
    nj = len(jobs)
    x_ref, nw_ref, w_ref = refs[:3]
    src_refs = refs[n_in:n_in + nj]
    o_ref = refs[n_in + nj]
    dst_refs = refs[n_in + nj + n_out:n_in + nj + n_out + nj]
    hn_ref = refs[-1]

    @pl.when(pl.program_id(1) == 0)
    def _():
        _rmsnorm_rows_to(x_ref, nw_ref, hn_ref, NORM_ROWS)
        if has_low:
            low_ref = refs[n_in + nj + 1]
            low_ref[...] = jnp.dot(hn_ref[...], refs[3][...], preferred_element_type=F32)

    o_ref[...] = jnp.dot(hn_ref[...], w_ref[...], preferred_element_type=F32)
    _run_cast_jobs(jobs, src_refs, dst_refs)


def _norm_matmul_f32w_kernel(x_ref, nw_ref, w_ref, o_ref, wb_ref, hn_ref):
    @pl.when(pl.program_id(0) == 0)
    def _():
        _rmsnorm_rows_to(x_ref, nw_ref, hn_ref, NORM_ROWS)

    wb = w_ref[...].astype(BF16)
    wb_ref[...] = wb
    o_ref[...] = jnp.dot(hn_ref[...], wb, preferred_element_type=F32)


def _norm_matmul_f32w(x, nw, w, *, tn):
    m, d = x.shape
    n = w.shape[1]
    assert n % tn == 0
    return pl.pallas_call(
        _norm_matmul_f32w_kernel,
        grid=(n // tn,),
        in_specs=[
            pl.BlockSpec((m, d), lambda j: (0, 0)),
            pl.BlockSpec((1, d), lambda j: (0, 0)),
            pl.BlockSpec((d, tn), lambda j: (0, j)),
        ],
        out_specs=[pl.BlockSpec((m, tn), lambda j: (0, j)), pl.BlockSpec((d, tn), lambda j: (0, j))],
        out_shape=[jax.ShapeDtypeStruct((m, n), F32), jax.ShapeDtypeStruct((d, n), BF16)],
        scratch_shapes=[pltpu.VMEM((m, d), BF16)],
        compiler_params=_cparams(1),
        name="norm_matmul_f32w",
    )(x, nw.reshape(1, d), w)


def _norm_matmul(x, nw, w, n, n_low=0, jobs=(), *, tm, tn):
    m, d = x.shape
    assert n % tn == 0 and n + n_low <= w.shape[1]
    w_low = w if n_low else None
    grid = (m // tm, n // tn)
    jobs = list(jobs)
    job_io = [_cast_job_io(job, grid) for job in jobs]
    in_specs = [
        pl.BlockSpec((tm, d), lambda i, j: (i, 0)),
        pl.BlockSpec((1, d), lambda i, j: (0, 0)),
        pl.BlockSpec((d, tn), lambda i, j: (0, j)),
    ]
    out_shape = [jax.ShapeDtypeStruct((m, n), F32)]
    out_specs = [pl.BlockSpec((tm, tn), lambda i, j: (i, j))]
    args = [x, nw.reshape(1, d), w]
    if w_low is not None:
        nl = n_low
        assert n % nl == 0
        in_specs.append(pl.BlockSpec((d, nl), lambda i, j: (0, n // nl)))
        out_shape.append(jax.ShapeDtypeStruct((m, nl), F32))
        out_specs.append(pl.BlockSpec((tm, nl), lambda i, j: (i, 0)))
        args.append(w_low)
    in_specs += [io[0] for io in job_io]
    out_specs += [io[1] for io in job_io]
    out_shape += [io[2] for io in job_io]
    args += [job["src"] for job in jobs]
    outs = pl.pallas_call(
        functools.partial(_norm_matmul_kernel, has_low=w_low is not None, jobs=jobs),
        grid=grid,
        in_specs=in_specs,
        out_specs=out_specs,
        out_shape=out_shape,
        scratch_shapes=[pltpu.VMEM((tm, d), BF16)],
        compiler_params=_cparams(2),
        name="norm_matmul",
    )(*args)
    n_main = 1 + (w_low is not None)
    return outs[0], (outs[1] if w_low is not None else None), list(outs[n_main:])


def _matmul_res_kernel(*refs, n_in):
    a_refs = refs[:n_in]
    w_refs = refs[n_in:2 * n_in]
    res_ref = refs[2 * n_in]
    o_ref = refs[2 * n_in + 1]
    acc = res_ref[...]
    for a_ref, w_ref in zip(a_refs, w_refs):
        acc = acc + jnp.dot(a_ref[...], w_ref[...], preferred_element_type=F32)
    o_ref[...] = acc


def _matmul_res(a_list, w, res, *, tm, tn):
    m, n = res.shape
    ka = a_list[0].shape[1]
    assert all(a.shape[1] == ka for a in a_list) and ka * len(a_list) == w.shape[0]
    grid = (m // tm, n // tn)
    in_specs = [pl.BlockSpec((tm, ka), lambda i, j: (i, 0)) for _ in a_list]
    for rb in range(len(a_list)):
        in_specs.append(pl.BlockSpec((ka, tn), lambda i, j, rb=rb: (rb, j)))
    in_specs.append(pl.BlockSpec((tm, tn), lambda i, j: (i, j)))
    return pl.pallas_call(
        functools.partial(_matmul_res_kernel, n_in=len(a_list)),
        grid=grid,
        in_specs=in_specs,
        out_specs=pl.BlockSpec((tm, tn), lambda i, j: (i, j)),
        out_shape=jax.ShapeDtypeStruct((m, n), F32),
        compiler_params=_cparams(2),
        name="matmul_res",
    )(*a_list, *([w] * len(a_list)), res)


def _local_cumsum(lg):
    c, k = lg.shape
    x = lg.reshape(c // SUBLANES, SUBLANES, k)
    pos = lax.broadcasted_iota(jnp.int32, x.shape, 1)
    s = 1
    while s < SUBLANES:
        x = x + jnp.where(pos >= s, pltpu.roll(x, s, 1), 0.0)
        s *= 2
    per = SUB // SUBLANES
    x = x.reshape(c // SUB, per, SUBLANES, k)
    groups = [x[:, 0]]
    for g in range(1, per):
        groups.append(x[:, g] + groups[-1][:, SUBLANES - 1:SUBLANES, :])
    return jnp.stack(groups, axis=1).reshape(c, k)


def _chunk_operands(q, k, bl, tot, sub):
    ns = q.shape[0] // sub
    a = [jnp.zeros_like(tot[0])]
    for m in range(ns):
        a.append(a[-1] + tot[m])
    b_last = a[ns]

    qt, kbar, khat, qin, kout = [], [], [], [], []
    qd = {d: [] for d in range(2, ns)}
    for m in range(ns):
        sl = slice(m * sub, (m + 1) * sub)
        blm = bl[sl]
        e_in = jnp.exp(blm)
        qtm = q[sl] * e_in
        kbm = k[sl] * (1.0 / e_in)
        khm = kbm * jnp.exp(tot[m])
        qt.append(qtm)
        kbar.append(kbm)
        khat.append(khm)
        qin.append(qtm * jnp.exp(a[m]))
        kout.append(khm * jnp.exp(b_last - a[m + 1]))
        for d in range(2, ns):
            if m >= d:
                qd[d].append(qtm * jnp.exp(a[m] - a[m - d + 1]))
            else:
                qd[d].append(jnp.zeros_like(qtm))

    def cat(xs):
        return jnp.concatenate(xs, axis=0).astype(BF16)

    return dict(qt=cat(qt), kbar=cat(kbar), khat=cat(khat), qin=cat(qin), kout=cat(kout),
                qd=[cat(qd[d]) for d in range(2, ns)], decay=jnp.exp(b_last))


def _chunk_steps(chunks, st_ref, sub=SUB):
    c = chunks[0][0][0].shape[0]
    ns = c // sub
    shift = sub.bit_length() - 1
    assert 1 << shift == sub and c % sub == 0
    ops = [[_chunk_operands(q, k, bl, tot, sub) for (q, k, _, bl, tot) in heads] for heads in chunks]
    raw = []
    for chunk_ops in ops:
        for op in chunk_ops:
            s = [lax.dot_general(op["qt"], op["kbar"], NT_DIMS, preferred_element_type=F32)]
            if ns > 1:
                s.append(lax.dot_general(op["qt"], op["khat"], NT_DIMS, preferred_element_type=F32))
            s += [lax.dot_general(qd, op["khat"], NT_DIMS, preferred_element_type=F32)
                  for qd in op["qd"]]
            raw.append(s)
    ri = lax.broadcasted_iota(jnp.int32, (c, c), 0)
    ci = lax.broadcasted_iota(jnp.int32, (c, c), 1)
    dist = (ri >> shift) - (ci >> shift)
    intra = []
    for s, (_, _, vb, _, _) in zip(raw, [head for heads in chunks for head in heads]):
        scores = jnp.where((dist == 0) & (ri >= ci), s[0], 0.0)
        for d in range(1, ns):
            scores = jnp.where(dist == d, s[d], scores)
        intra.append(jnp.dot(scores.astype(BF16), vb, preferred_element_type=F32))
    outs = []
    for g, (heads, chunk_ops) in enumerate(zip(chunks, ops)):
        outs.append([
            intra[g * len(heads) + h] + lax.dot_general(
                op["qin"], st_ref[h].astype(BF16), NT_DIMS, preferred_element_type=F32)
            for h, op in enumerate(chunk_ops)])
        for h, op in enumerate(chunk_ops):
            st_ref[h] = st_ref[h] * op["decay"] + lax.dot_general(
                heads[h][2], op["kout"], TN_DIMS, preferred_element_type=F32)
    return outs


def _sub_totals(bl):
    return [bl[(m + 1) * SUB - 1:(m + 1) * SUB, :] for m in range(bl.shape[0] // SUB)]


def _state_prologue(s0_ref, st_ref):
    @pl.when(pl.program_id(1) == 0)
    def _():
        for h in range(st_ref.shape[0]):
            if s0_ref is None:
                st_ref[h] = jnp.zeros(st_ref.shape[1:], F32)
            else:
                st_ref[h] = jnp.transpose(s0_ref[0, h])


def _state_epilogue(sn_ref, st_ref):
    @pl.when(pl.program_id(1) == pl.num_programs(1) - 1)
    def _():
        for h in range(st_ref.shape[0]):
            sn_ref[0, h] = jnp.transpose(st_ref[h])


def _chunk_loop(n_rows, chunk, body, together, unroll):
    assert n_rows % chunk == 0
    n_chunks = n_rows // chunk
    together = min(together, n_chunks)
    assert n_chunks % together == 0
    trips = n_chunks // together

    def step(c, carry):
        body([pl.ds(pl.multiple_of((c * together + p) * chunk, chunk), chunk) for p in range(together)])
        return carry

    lax.fori_loop(0, trips, step, 0, unroll=min(unroll, trips))


def _head(ref, rows, h, width):
    return ref[rows, h * width:(h + 1) * width]


def _ret_chunks(chunks, st_ref):
    chunk = chunks[0][2].shape[0]
    pos1 = (lax.broadcasted_iota(jnp.int32, (chunk, 1), 0) + 1).astype(F32)
    half = DK_A // 2
    inputs = []
    for get, _, cos, sin in chunks:
        heads = []
        for h in range(H_A):
            lgam = float(np.log1p(-np.exp2(-5.0 - h)))
            bl = pos1 * lgam
            tot = [jnp.full((1, 1), lgam * chunk, F32)]
            qr, kr = get("q", h), get("k", h)
            q = qr * cos + pltpu.roll(qr, half, 1) * sin
            k = (kr * cos + pltpu.roll(kr, half, 1) * sin) * (DK_A ** -0.5)
            heads.append((q, k, get("v", h).astype(BF16), bl, tot))
        inputs.append(heads)
    for (get, put, _, _), outs in zip(chunks, _chunk_steps(inputs, st_ref, sub=chunk)):
        for h, o in enumerate(outs):
            mu = jnp.mean(o, axis=-1, keepdims=True)
            oc = o - mu
            var = jnp.mean(oc * oc, axis=-1, keepdims=True)
            put(h, ((oc * lax.rsqrt(var + EPS)) * _silu(get("g", h))).astype(BF16))


def _hgrn_lower_bound(lbp_ref, layer):
    lbp = lbp_ref[...]
    e = jnp.exp(lbp - jnp.max(lbp, axis=0, keepdims=True))
    sm = e / jnp.sum(e, axis=0, keepdims=True)
    return jnp.sum(sm[:layer + 1], axis=0, keepdims=True)


def _gated_rmsnorm_out(chunks, outs_per_chunk, gn):
    for (get, put, *_), outs in zip(chunks, outs_per_chunk):
        for h, o in enumerate(outs):
            ms = jnp.mean(o * o, axis=-1, keepdims=True)
            put(h, (((o * lax.rsqrt(ms + EPS)) * gn) * _silu(get("g", h))).astype(BF16))


def _hgrn_chunks(chunks, lb_all, gn, st_ref):
    inputs = []
    for get, _ in chunks:
        heads = []
        for h in range(H_B):
            lb = lb_all[:, h * DK_B:(h + 1) * DK_B]
            q = _silu(get("q", h)) * (DK_B ** -0.5)
            f = lb + (1.0 - lb) * _sigmoid(get("f", h))
            bl = _local_cumsum(jnp.log(f))
            heads.append((q, 1.0 - f, get("i", h).astype(BF16), bl, _sub_totals(bl)))
        inputs.append(heads)
    _gated_rmsnorm_out(chunks, _chunk_steps(inputs, st_ref), gn)


def _gla_chunks(chunks, wgk, bgk, gn, st_ref):
    inputs = []
    for get, _, low in chunks:
        x = jnp.dot(low.astype(BF16), wgk, preferred_element_type=F32) + bgk
        lg_all = (jnp.minimum(x, 0.0) - jnp.log1p(jnp.exp(-jnp.abs(x)))) * (1.0 / GATE_NORMALIZER)
        heads = []
        for h in range(H_C):
            bl = _local_cumsum(lg_all[:, h * DK_C:(h + 1) * DK_C])
            q = get("q", h) * (DK_C ** -0.5)
            heads.append((q, get("k", h), get("v", h).astype(BF16), bl, _sub_totals(bl)))
        inputs.append(heads)
    _gated_rmsnorm_out(chunks, _chunk_steps(inputs, st_ref), gn)


def _ref_getter(refs, widths, rows):
    return lambda name, h: _head(refs[name], rows, h, widths[name])


def _ref_putter(o_ref, width, rows):
    def put(h, x):
        o_ref[rows, h * width:(h + 1) * width] = x
    return put


def _ret_kernel(*refs, has_s0):
    if has_s0:
        q_ref, k_ref, v_ref, g_ref, cos_ref, sin_ref, s0_ref, o_ref, sn_ref, st_ref = refs
    else:
        q_ref, k_ref, v_ref, g_ref, cos_ref, sin_ref, o_ref, sn_ref, st_ref = refs
        s0_ref = None
    _state_prologue(s0_ref, st_ref)
    srcs = {"q": q_ref, "k": k_ref, "v": v_ref, "g": g_ref}
    widths = {"q": DK_A, "k": DK_A, "v": DV_A, "g": DV_A}

    def body(rows_list):
        _ret_chunks([(_ref_getter(srcs, widths, rows), _ref_putter(o_ref, DV_A, rows),
                      cos_ref[rows, :], sin_ref[rows, :]) for rows in rows_list], st_ref)

    n_rows = q_ref.shape[0]
    _chunk_loop(n_rows, min(RET_CHUNK, n_rows), body, together=2, unroll=1)
    _state_epilogue(sn_ref, st_ref)


def _hgrn_kernel(*refs, has_s0, layer, jobs):
    q_ref, f_ref, i_ref, g_ref, lbp_ref, gn_ref = refs[:6]
    pos = 6
    s0_ref = refs[pos] if has_s0 else None
    pos += has_s0
    src_refs = refs[pos:pos + len(jobs)]
    o_ref, sn_ref = refs[pos + len(jobs):pos + len(jobs) + 2]
    dst_refs = refs[pos + len(jobs) + 2:pos + 2 * len(jobs) + 2]
    st_ref = refs[-1]
    _run_cast_jobs(jobs, src_refs, dst_refs)
    _state_prologue(s0_ref, st_ref)
    lb_all = _hgrn_lower_bound(lbp_ref, layer)
    gn = gn_ref[...]
    srcs = {"q": q_ref, "f": f_ref, "i": i_ref, "g": g_ref}
    widths = {"q": DK_B, "f": DK_B, "i": DV_B, "g": DV_B}

    def body(rows_list):
        _hgrn_chunks([(_ref_getter(srcs, widths, rows), _ref_putter(o_ref, DV_B, rows))
                      for rows in rows_list], lb_all, gn, st_ref)

    _chunk_loop(q_ref.shape[0], CHUNK, body, together=1, unroll=2)
    _state_epilogue(sn_ref, st_ref)


def _gla_kernel(*refs, has_s0):
    if has_s0:
        q_ref, k_ref, v_ref, g_ref, low_ref, wgk_ref, bgk_ref, gn_ref, s0_ref, o_ref, sn_ref, st_ref = refs
    else:
        q_ref, k_ref, v_ref, g_ref, low_ref, wgk_ref, bgk_ref, gn_ref, o_ref, sn_ref, st_ref = refs
        s0_ref = None
    _state_prologue(s0_ref, st_ref)
    gn = gn_ref[...]
    bgk = bgk_ref[...]
    srcs = {"q": q_ref, "k": k_ref, "v": v_ref, "g": g_ref}
    widths = {"q": DK_C, "k": DK_C, "v": DV_C, "g": DV_C}

    def body(rows_list):
        _gla_chunks([(_ref_getter(srcs, widths, rows), _ref_putter(o_ref, DV_C, rows), low_ref[rows, :])
                     for rows in rows_list], wgk_ref[...], bgk, gn, st_ref)

    _chunk_loop(q_ref.shape[0], CHUNK, body, together=4, unroll=1)
    _state_epilogue(sn_ref, st_ref)


def _row_spec(tt, width, nt, block):
    return pl.BlockSpec((tt, width), lambda b, t: (b * nt + t, block))


def _recurrence_call(body, in_specs, args, s0, bsz, seq, heads, dk, dv, tt, jobs=()):
    nt = seq // tt
    if s0 is not None:
        in_specs = in_specs + [pl.BlockSpec((1, heads, dk, dv), lambda b, t: (b, 0, 0, 0))]
        args = args + [s0]
    job_io = [_cast_job_io(job, (bsz, nt)) for job in jobs]
    outs = pl.pallas_call(
        body,
        grid=(bsz, nt),
        in_specs=in_specs + [io[0] for io in job_io],
        out_specs=[
            pl.BlockSpec((tt, heads * dv), lambda b, t: (b * nt + t, 0)),
            pl.BlockSpec((1, heads, dk, dv), lambda b, t: (b, 0, 0, 0)),
        ] + [io[1] for io in job_io],
        out_shape=[
            jax.ShapeDtypeStruct((bsz * seq, heads * dv), BF16),
            jax.ShapeDtypeStruct((bsz, heads, dk, dv), F32),
        ] + [io[2] for io in job_io],
        scratch_shapes=[pltpu.VMEM((heads, dv, dk), F32)],
        compiler_params=_cparams(2),
        name="recurrence",
    )(*args, *[job["src"] for job in jobs])
    return outs[0], outs[1], list(outs[2:])


def _retention(p, cos2, sin2, s0, bsz, seq, tt):
    nt = seq // tt
    qa, va = H_A * DK_A, H_A * DV_A
    in_specs = [
        _row_spec(tt, qa, nt, 0),
        _row_spec(tt, qa, nt, 1),
        _row_spec(tt, va, nt, 2 * qa // va),
        _row_spec(tt, va, nt, (2 * qa + va) // va),
        pl.BlockSpec((tt, DK_A), lambda b, t: (t, 0)),
        pl.BlockSpec((tt, DK_A), lambda b, t: (t, 0)),
    ]
    args = [p, p, p, p, cos2, sin2]
    body = functools.partial(_ret_kernel, has_s0=s0 is not None)
    return _recurrence_call(body, in_specs, args, s0, bsz, seq, H_A, DK_A, DV_A, tt)[:2]


def _hgrn(p, lbp, gnorm, s0, bsz, seq, tt, layer, jobs=()):
    nt = seq // tt
    base = 2 * H_A * DK_A + 2 * H_A * DV_A
    qb = H_B * DK_B
    assert base % qb == 0 and H_B * DV_B == qb
    in_specs = [_row_spec(tt, qb, nt, base // qb + n) for n in range(4)]
    in_specs += [
        pl.BlockSpec(lbp.shape, lambda b, t: (0, 0)),
        pl.BlockSpec((1, DV_B), lambda b, t: (0, 0)),
    ]
    args = [p, p, p, p, lbp, gnorm.reshape(1, DV_B)]
    jobs = list(jobs)
    body = functools.partial(_hgrn_kernel, has_s0=s0 is not None, layer=layer, jobs=jobs)
    return _recurrence_call(body, in_specs, args, s0, bsz, seq, H_B, DK_B, DV_B, tt, jobs)


def _gla(p, low, wgk, bgk, gnorm, s0, bsz, seq, tt):
    nt = seq // tt
    qc, vc = H_C * DK_C, H_C * DV_C
    in_specs = [
        _row_spec(tt, qc, nt, 0),
        _row_spec(tt, qc, nt, 1),
        _row_spec(tt, vc, nt, 2 * qc // vc),
        _row_spec(tt, vc, nt, (2 * qc + vc) // vc),
        _row_spec(tt, low.shape[1], nt, 0),
        pl.BlockSpec(wgk.shape, lambda b, t: (0, 0)),
        pl.BlockSpec((1, qc), lambda b, t: (0, 0)),
        pl.BlockSpec((1, DV_C), lambda b, t: (0, 0)),
    ]
    args = [p, p, p, p, low, wgk, bgk.reshape(1, qc), gnorm.reshape(1, DV_C)]
    body = functools.partial(_gla_kernel, has_s0=s0 is not None)
    return _recurrence_call(body, in_specs, args, s0, bsz, seq, H_C, DK_C, DV_C, tt)[:2]


def _ffn_kernel(*refs, seq_rows, tiles_per_seq, has_cache, final_norm, jobs):
    refs = list(refs)
    h_ref, nw_ref, up_ref, conv_ref, wd_ref = refs[:5]
    pos = 5
    cache_ref = None
    nf_ref = None
    if has_cache:
        cache_ref = refs[pos]
        pos += 1
    if final_norm:
        nf_ref = refs[pos]
        pos += 1
    nj = len(jobs)
    src_refs = refs[pos:pos + nj]
    out_ref, newc_ref = refs[pos + nj:pos + nj + 2]
    dst_refs = refs[pos + nj + 2:pos + 2 * nj + 2]
    hn_ref, carry_ref = refs[pos + 2 * nj + 2:]

    i = pl.program_id(0)
    j = pl.program_id(1)
    tm, tf = h_ref.shape[0], wd_ref.shape[0]

    @pl.when(j == 0)
    def _():
        _rmsnorm_rows_to(h_ref, nw_ref, hn_ref, NORM_ROWS, copy_ref=out_ref)

    if tiles_per_seq > 1:
        @pl.when(i % tiles_per_seq == 0)
        def _():
            carry_ref[j] = jnp.zeros((CONV_W - 1, tf), F32)

    hn = hn_ref[...]
    a = jnp.dot(hn, up_ref[0], preferred_element_type=F32)
    u = jnp.dot(hn, up_ref[1], preferred_element_type=F32)

    nseq = tm // seq_rows
    r8 = lax.broadcasted_iota(jnp.int32, (SUBLANES, 1), 0)
    sh1 = pltpu.roll(a, 1, 0)
    sh2 = pltpu.roll(a, 2, 0)
    a1_parts, a2_parts = [], []
    for s in range(nseq):
        if tiles_per_seq > 1:
            p0, p1 = carry_ref[j, 0:1, :], carry_ref[j, 1:2, :]
        elif has_cache:
            p0, p1 = cache_ref[s, 0:1, :], cache_ref[s, 1:2, :]
        else:
            p0 = p1 = jnp.zeros((1, tf), F32)
        lo, hi = s * seq_rows, (s + 1) * seq_rows
        head = slice(lo, lo + SUBLANES)
        a1_parts += [jnp.where(r8 == 0, p1, sh1[head]), sh1[lo + SUBLANES:hi]]
        a2_parts += [jnp.where(r8 == 0, p0, jnp.where(r8 == 1, p1, sh2[head])),
                     sh2[lo + SUBLANES:hi]]
    a1 = jnp.concatenate(a1_parts, axis=0)
    a2 = jnp.concatenate(a2_parts, axis=0)
    cw = conv_ref[...]
    ac = cw[CONV_W:CONV_W + 1, :] + a2 * cw[0:1, :] + a1 * cw[1:2, :] + a * cw[2:3, :]

    if tiles_per_seq > 1:
        carry_ref[j] = a[tm - (CONV_W - 1):, :]
    for s in range(nseq):
        newc_ref[s] = a[(s + 1) * seq_rows - (CONV_W - 1):(s + 1) * seq_rows, :]

    y = (_silu(ac) * u).astype(BF16)
    out_ref[...] += jnp.dot(y, wd_ref[...], preferred_element_type=F32)
    _run_cast_jobs(jobs, src_refs, dst_refs)

    if final_norm:
        @pl.when(j == pl.num_programs(1) - 1)
        def _():
            rows = min(NORM_ROWS, tm)

            def body(c, carry):
                r = pl.multiple_of(c * rows, rows)
                x = out_ref[pl.ds(r, rows), :]
                ms = jnp.mean(x * x, axis=-1, keepdims=True)
                out_ref[pl.ds(r, rows), :] = (x * lax.rsqrt(ms + EPS)) * nf_ref[...]
                return carry

            lax.fori_loop(0, tm // rows, body, 0)


def _ffn(h, nw, up, conv, wd, cache, nf, jobs=(), *, bsz, seq, tm, tf):
    m, d = h.shape
    dffp = up.shape[2]
    nj = dffp // tf
    jobs = list(jobs)
    if seq >= tm:
        tiles_per_seq = seq // tm
        seq_rows = tm
        seqs_per_tile = 1
    else:
        tiles_per_seq = 1
        seq_rows = seq
        seqs_per_tile = tm // seq
    assert seq_rows & (seq_rows - 1) == 0
    grid = (m // tm, nj)
    in_specs = [
        pl.BlockSpec((tm, d), lambda i, j: (i, 0)),
        pl.BlockSpec((1, d), lambda i, j: (0, 0)),
        pl.BlockSpec((2, d, tf), lambda i, j: (0, 0, j)),
        pl.BlockSpec((CONV_W + 1, tf), lambda i, j: (0, j)),
        pl.BlockSpec((tf, d), lambda i, j: (j, 0)),
    ]
    args = [h, nw.reshape(1, d), up, conv, wd]
    if cache is not None:
        assert tiles_per_seq == 1
        in_specs.append(pl.BlockSpec((seqs_per_tile, CONV_W - 1, tf), lambda i, j: (i, 0, j)))
        args.append(cache)
    if nf is not None:
        in_specs.append(pl.BlockSpec((1, d), lambda i, j: (0, 0)))
        args.append(nf.reshape(1, d))
    newc_spec = pl.BlockSpec((seqs_per_tile, CONV_W - 1, tf), lambda i, j: (i, 0, j))
    job_io = [_cast_job_io(job, grid) for job in jobs]
    in_specs += [io[0] for io in job_io]
    args += [job["src"] for job in jobs]
    body = functools.partial(
        _ffn_kernel, seq_rows=seq_rows, tiles_per_seq=tiles_per_seq,
        has_cache=cache is not None, final_norm=nf is not None, jobs=jobs)
    outs = pl.pallas_call(
        body,
        grid=grid,
        in_specs=in_specs,
        out_specs=[pl.BlockSpec((tm, d), lambda i, j: (i, 0)), newc_spec] + [io[1] for io in job_io],
        out_shape=[
            jax.ShapeDtypeStruct((m, d), F32),
            jax.ShapeDtypeStruct((grid[0] * seqs_per_tile, CONV_W - 1, dffp), F32),
        ] + [io[2] for io in job_io],
        scratch_shapes=[
            pltpu.VMEM((tm, d), BF16),
            pltpu.VMEM((nj, CONV_W - 1, tf), F32),
        ],
        compiler_params=_cparams(2),
        name="ffn",
    )(*args)
    out, newc = outs[0], outs[1]
    if tiles_per_seq > 1:
        newc = newc.reshape(bsz, tiles_per_seq, CONV_W - 1, dffp)[:, -1]
    return out, newc, list(outs[2:])


def _pad_to(x, axis, size):
    pad = size - x.shape[axis]
    if pad == 0:
        return x
    widths = [(0, 0)] * x.ndim
    widths[axis] = (0, pad)
    return jnp.pad(x, widths)


def _rotary_tables(pos):
    half = DK_A // 2
    inv = ROPE_BASE ** (-jnp.arange(half, dtype=F32) / half)
    ang = pos.astype(F32)[:, None] * inv[None, :]
    cos, sin = jnp.cos(ang), jnp.sin(ang)
    return jnp.concatenate([cos, cos], axis=-1), jnp.concatenate([-sin, sin], axis=-1)


FFN_TILE = 512


def _in_proj_job(raw, l, n_steps):
    if l % 2 == 0:
        src = raw["w_in_even"]
        rb = _row_block(src.shape[1], src.shape[1], n_steps)
        return _cast_job(src, (l // 2,), src.shape[1], src.shape[2], rb)
    src = jnp.swapaxes(raw["w_in_odd"], 1, 2)
    return _transpose_cast_job(src, (l // 2,), src.shape[1], src.shape[2], LANES)


def _initial_weights(raw):
    w = {}
    depth = raw["norm_mix"].shape[0]
    for l in range(1, depth, 2):
        w[("gk2", l)] = _pad_to(raw["w_gk2"][l // 2], 0, LANES).astype(BF16)
    return w


def _cast_now(job):
    src = job["src"][job["lead"]]
    if job["transpose"]:
        return _pad_to(src.T.astype(BF16), 1, job["n_dst"] * job["rb"])
    rows_out = job["n_dst"] * job["rb"]
    parts = [_pad_to(_pad_to(src[:, s * job["cin"]:(s + 1) * job["cin"]].astype(BF16), 1, job["cout"]),
                     0, rows_out) for s in range(job["halves"])]
    return jnp.stack(parts) if job["halves"] > 1 else parts[0]


def _missing_jobs(w, wanted, n_steps):
    keys, jobs = [], []
    for key, job in wanted:
        if key in w:
            continue
        if job["n_dst"] <= n_steps:
            keys.append(key)
            jobs.append(job)
        else:
            w[key] = _cast_now(job)
    return keys, jobs


def _row_block(rows, rows_out, n_steps):
    candidates = [BF16_TILE_ROWS << s for s in range(4)]
    fits = [rb for rb in candidates if rows % rb == 0 and rows_out % rb == 0]
    return next((rb for rb in fits if rows_out // rb <= n_steps), fits[-1])


def _layer_weight_jobs(raw, l, dffp, n_steps):
    w_out = raw["w_out_even"] if l % 2 == 0 else raw["w_out_odd"]
    up, down = raw["ffn_w_up"], raw["ffn_w_down"]
    dff = down.shape[1]
    d = up.shape[1]
    rb, rb_down = _row_block(d, d, n_steps), _row_block(dff, dffp, n_steps)
    return [
        (("out", l), _cast_job(w_out, (l // 2,), w_out.shape[1], w_out.shape[2], rb)),
        (("up", l), _cast_job(up, (l,), d, dff, rb, halves=2, cout=dffp)),
        (("down", l), _cast_job(down, (l,), dff, down.shape[2], rb_down, rows_out=dffp)),
    ]


def _tiles(m, seq):
    return min(1024, m), 1024, min(512, m), 2048, min(512, m), min(512, seq)


def _first_projection(x, raw, w):
    bsz, seq, d = x.shape
    _, tn, *_ = _tiles(bsz * seq, seq)
    p, w[("in", 0)] = _norm_matmul_f32w(x.reshape(bsz * seq, d), raw["norm_mix"][0],
                                        raw["w_in_even"][0], tn=tn)
    return p


def _run_group(x, pos0, s_ret, s_hgrn, s_gla, s_conv, raw, w, p0=None):
    if ("in", 0) not in w:
        w[("in", 0)] = raw["w_in_even"][0].astype(BF16)
    bsz, seq, d = x.shape
    m = bsz * seq
    depth = raw["norm_mix"].shape[0]
    dff = raw["ffn_w_down"].shape[1]
    dffp = -(-dff // FFN_TILE) * FFN_TILE
    tm, tn, tm_out, tn_out, tm_ffn, tt = _tiles(m, seq)
    cos2, sin2 = _rotary_tables(pos0 + jnp.arange(seq, dtype=jnp.int32))
    conv = _pad_to(jnp.concatenate([raw["ffn_conv_w"], raw["ffn_conv_b"][:, None, :]], axis=1), 2, dffp)

    h = x.reshape(m, d)
    new_ret, new_hgrn, new_gla, new_conv = [], [], [], []
    for l in range(depth):
        i = l // 2
        w_in = w[("in", l)]
        n_low = 0 if l % 2 == 0 else LANES
        n = w_in.shape[1] - n_low
        n_steps = (m // tm) * (n // tn)
        rec_steps = bsz * (seq // tt)
        wanted = dict(_layer_weight_jobs(raw, l, dffp, n_steps))
        wanted_rec = dict(_layer_weight_jobs(raw, l, dffp, rec_steps))
        in_rec = [("up", l)] if l % 2 == 0 and wanted_rec[("up", l)]["n_dst"] <= rec_steps else []
        keys, jobs = _missing_jobs(w, [kv for kv in wanted.items() if kv[0] not in in_rec], n_steps)
        if l == 0 and p0 is not None:
            assert not jobs
            p, low = p0, None
        else:
            p, low, casts = _norm_matmul(h, raw["norm_mix"][l], w_in, n, n_low, jobs, tm=tm, tn=tn)
            w.update(zip(keys, casts))
        if l % 2 == 0:
            o_a, sr = _retention(p, cos2, sin2, None if s_ret is None else s_ret[i], bsz, seq, tt)
            keys, jobs = _missing_jobs(w, [(key, wanted_rec[key]) for key in in_rec], rec_steps)
            o_b, sh, casts = _hgrn(p, raw["hgrn_lb"], raw["hgrn_gnorm"][i],
                                   None if s_hgrn is None else s_hgrn[i], bsz, seq, tt, l, jobs)
            w.update(zip(keys, casts))
            o_list = [o_a, o_b]
            new_ret.append(sr)
            new_hgrn.append(sh)
        else:
            o_c, sg = _gla(p, low, w[("gk2", l)], raw["b_gk2"][i], raw["gla_gnorm"][i],
                           None if s_gla is None else s_gla[i], bsz, seq, tt)
            o_list = [o_c]
            new_gla.append(sg)
        h = _matmul_res(o_list, w[("out", l)], h, tm=tm_out, tn=tn_out)

        n_steps = (m // tm_ffn) * (dffp // FFN_TILE)
        wanted = []
        if l + 1 < depth:
            wanted = ([(("in", l + 1), _in_proj_job(raw, l + 1, n_steps))]
                      + _layer_weight_jobs(raw, l + 1, dffp, n_steps))
        keys, jobs = _missing_jobs(w, wanted, n_steps)
        cache = None if s_conv is None else _pad_to(s_conv[l], 2, dffp)
        h, nc, casts = _ffn(h, raw["norm_ffn"][l], w[("up", l)], conv[l], w[("down", l)], cache,
                            raw["norm_final"] if l == depth - 1 else None, jobs,
                            bsz=bsz, seq=seq, tm=tm_ffn, tf=FFN_TILE)
        w.update(zip(keys, casts))
        new_conv.append(nc[:, :, :dff])
    return (h.reshape(bsz, seq, d), jnp.stack(new_ret), jnp.stack(new_hgrn),
            jnp.stack(new_gla), jnp.stack(new_conv))


def kernel(x_prompt, x_sample, state_ret, state_hgrn, state_gla, cache_ffn_conv, norm_mix, norm_ffn, norm_final, w_in_even, w_out_even, hgrn_lb, hgrn_gnorm, w_in_odd, w_gk2, b_gk2, gla_gnorm, w_out_odd, ffn_w_up, ffn_conv_w, ffn_conv_b, ffn_w_down):
    raw = dict(norm_mix=norm_mix, norm_ffn=norm_ffn, norm_final=norm_final, w_in_even=w_in_even,
               w_out_even=w_out_even, hgrn_lb=hgrn_lb, hgrn_gnorm=hgrn_gnorm, w_in_odd=w_in_odd,
               w_gk2=w_gk2, b_gk2=b_gk2, gla_gnorm=gla_gnorm, w_out_odd=w_out_odd, ffn_w_up=ffn_w_up,
               ffn_conv_w=ffn_conv_w, ffn_conv_b=ffn_conv_b, ffn_w_down=ffn_w_down)
    w = _initial_weights(raw)
    p0_sample = _first_projection(x_sample, raw, w)
    y_p, ret_p, hgrn_p, gla_p, conv_p = _run_group(x_prompt, 0, None, None, None, None, raw, w)
    y_s, ret_s, hgrn_s, gla_s, conv_s = _run_group(
        x_sample, PAST_LEN, state_ret, state_hgrn, state_gla, cache_ffn_conv, raw, w, p0=p0_sample)
    return (y_p, y_s, ret_p, ret_s, hgrn_p, hgrn_s, gla_p, gla_s, conv_p, conv_s)
```

```python
import functools

import numpy as np
import jax
import jax.numpy as jnp
from jax import lax
from jax.experimental import pallas as pl
from jax.experimental.pallas import tpu as pltpu

F32 = jnp.float32
BF16 = jnp.bfloat16

EPS = 1e-6
ROPE_BASE = 10000.0
GATE_NORMALIZER = 16.0
PAST_LEN = 1024

H_A, DK_A, DV_A = 4, 128, 256
H_B, DK_B, DV_B = 8, 128, 128
H_C, DK_C, DV_C = 4, 256, 512
GK_RANK = 16
CONV_W = 3

LANES = 128
SUBLANES = 8
BF16_TILE_ROWS = 16
VMEM_LIMIT_BYTES = 56 * 1024 * 1024
NORM_ROWS = 128

CHUNK = 64
SUB = 16
HGRN_HEAD_GROUP = 4
RET_CHUNK = 128
assert CHUNK % SUB == 0 and H_B % HGRN_HEAD_GROUP == 0

NT_DIMS = (((1,), (1,)), ((), ()))
TN_DIMS = (((0,), (0,)), ((), ()))


def _cparams(n_axes):
    return pltpu.CompilerParams(
        dimension_semantics=("arbitrary",) * n_axes,
        vmem_limit_bytes=VMEM_LIMIT_BYTES,
    )


def _sigmoid(x):
    return 1.0 / (1.0 + jnp.exp(-x))


def _silu(x):
    return x * _sigmoid(x)


def _rmsnorm_rows_to(x_ref, nw_ref, dst_ref, rows, copy_ref=None):
    rows = min(rows, x_ref.shape[0])
    assert x_ref.shape[0] % rows == 0
    n = x_ref.shape[0] // rows

    def body(c, carry):
        r = pl.multiple_of(c * rows, rows)
        x = x_ref[pl.ds(r, rows), :]
        if copy_ref is not None:
            copy_ref[pl.ds(r, rows), :] = x
        ms = jnp.mean(x * x, axis=-1, keepdims=True)
        dst_ref[pl.ds(r, rows), :] = ((x * lax.rsqrt(ms + EPS)) * nw_ref[...]).astype(BF16)
        return carry

    lax.fori_loop(0, n, body, 0)


def _cast_job(src, lead, rows, cin, rb, *, halves=1, cout=None, rows_out=None):
    cout = cin if cout is None else cout
    rows_out = rows if rows_out is None else rows_out
    assert rows % rb == 0 and rows_out % rb == 0
    return dict(src=src, lead=tuple(lead), rb=rb, cin=cin, cout=cout, halves=halves,
                n_src=rows // rb, n_dst=rows_out // rb, transpose=False)


def _transpose_cast_job(src, lead, rows, cin, rb):
    n = -(-rows // rb)
    return dict(src=src, lead=tuple(lead), rb=rb, cin=cin, cout=cin, halves=1, rows=rows,
                n_src=n, n_dst=n, transpose=True)


def _cast_job_io(job, grid):
    assert job["n_dst"] <= grid[0] * grid[1]
    lead = job["lead"]
    rb, cin, cout, n_dst, halves = job["rb"], job["cin"], job["cout"], job["n_dst"], job["halves"]

    def src_index(i, j):
        return lead + (jnp.minimum(i * grid[1] + j, job["n_src"] - 1), 0)

    def dst_index(i, j):
        blk = jnp.minimum(i * grid[1] + j, n_dst - 1)
        if job["transpose"]:
            return (0, blk)
        return (0, blk, 0) if halves > 1 else (blk, 0)

    if job["transpose"]:
        dst_block, dst_shape = (cin, rb), (cin, n_dst * rb)
    elif halves > 1:
        dst_block, dst_shape = (halves, rb, cout), (halves, n_dst * rb, cout)
    else:
        dst_block, dst_shape = (rb, cout), (n_dst * rb, cout)
    return (pl.BlockSpec((None,) * len(lead) + (rb, halves * cin), src_index),
            pl.BlockSpec(dst_block, dst_index),
            jax.ShapeDtypeStruct(dst_shape, BF16))


def _run_cast_jobs(jobs, src_refs, dst_refs):
    step = pl.program_id(0) * pl.num_programs(1) + pl.program_id(1)
    for job, src_ref, dst_ref in zip(jobs, src_refs, dst_refs):
        if job["transpose"]:
            first_row = jnp.minimum(step, job["n_src"] - 1) * job["rb"]
            row = lax.broadcasted_iota(jnp.int32, (job["rb"], 1), 0)
            x = jnp.where(row < job["rows"] - first_row, src_ref[...], 0.0)
            dst_ref[...] = jnp.transpose(x).astype(BF16)
            continue
        cin, cout, halves = job["cin"], job["cout"], job["halves"]
        for s in range(halves):
            y = src_ref[:, s * cin:(s + 1) * cin].astype(BF16)
            if job["n_dst"] > job["n_src"]:
                y = jnp.where(step < job["n_src"], y, jnp.zeros_like(y))
            dst = dst_ref.at[s] if halves > 1 else dst_ref
            if cout > cin:
                dst[:, :cin] = y
                dst[:, cin:] = jnp.zeros((job["rb"], cout - cin), BF16)
            else:
                dst[...] = y


def _norm_matmul_kernel(*refs, has_low, jobs):
    n_in = 3 + has_low
    n_out = 1 + has_low
    nj = len(jobs)
    x_ref, nw_ref, w_ref = refs[:3]
    src_refs = refs[n_in:n_in + nj]
    o_ref = refs[n_in + nj]
    dst_refs = refs[n_in + nj + n_out:n_in + nj + n_out + nj]
    hn_ref = refs[-1]

    @pl.when(pl.program_id(1) == 0)
    def _():
        _rmsnorm_rows_to(x_ref, nw_ref, hn_ref, NORM_ROWS)
        if has_low:
            low_ref = refs[n_in + nj + 1]
            low_ref[...] = jnp.dot(hn_ref[...], refs[3][...], preferred_element_type=F32)

    o_ref[...] = jnp.dot(hn_ref[...], w_ref[...], preferred_element_type=F32)
    _run_cast_jobs(jobs, src_refs, dst_refs)


def _norm_matmul_f32w_kernel(x_ref, nw_ref, w_ref, o_ref, wb_ref, hn_ref):
    @pl.when(pl.program_id(0) == 0)
    def _():
        _rmsnorm_rows_to(x_ref, nw_ref, hn_ref, NORM_ROWS)

    wb = w_ref[...].astype(BF16)
    wb_ref[...] = wb
    o_ref[...] = jnp.dot(hn_ref[...], wb, preferred_element_type=F32)


def _norm_matmul_f32w(x, nw, w, *, tn):
    m, d = x.shape
    n = w.shape[1]
    assert n % tn == 0
    return pl.pallas_call(
        _norm_matmul_f32w_kernel,
        grid=(n // tn,),
        in_specs=[
            pl.BlockSpec((m, d), lambda j: (0, 0)),
            pl.BlockSpec((1, d), lambda j: (0, 0)),
            pl.BlockSpec((d, tn), lambda j: (0, j)),
        ],
        out_specs=[pl.BlockSpec((m, tn), lambda j: (0, j)), pl.BlockSpec((d, tn), lambda j: (0, j))],
        out_shape=[jax.ShapeDtypeStruct((m, n), F32), jax.ShapeDtypeStruct((d, n), BF16)],
        scratch_shapes=[pltpu.VMEM((m, d), BF16)],
        compiler_params=_cparams(1),
        name="norm_matmul_f32w",
    )(x, nw.reshape(1, d), w)


def _norm_matmul(x, nw, w, n, n_low=0, jobs=(), *, tm, tn):
    m, d = x.shape
    assert n % tn == 0 and n + n_low <= w.shape[1]
    w_low = w if n_low else None
    grid = (m // tm, n // tn)
    jobs = list(jobs)
    job_io = [_cast_job_io(job, grid) for job in jobs]
    in_specs = [
        pl.BlockSpec((tm, d), lambda i, j: (i, 0)),
        pl.BlockSpec((1, d), lambda i, j: (0, 0)),
        pl.BlockSpec((d, tn), lambda i, j: (0, j)),
    ]
    out_shape = [jax.ShapeDtypeStruct((m, n), F32)]
    out_specs = [pl.BlockSpec((tm, tn), lambda i, j: (i, j))]
    args = [x, nw.reshape(1, d), w]
    if w_low is not None:
        nl = n_low
        assert n % nl == 0
        in_specs.append(pl.BlockSpec((d, nl), lambda i, j: (0, n // nl)))
        out_shape.append(jax.ShapeDtypeStruct((m, nl), F32))
        out_specs.append(pl.BlockSpec((tm, nl), lambda i, j: (i, 0)))
        args.append(w_low)
    in_specs += [io[0] for io in job_io]
    out_specs += [io[1] for io in job_io]
    out_shape += [io[2] for io in job_io]
    args += [job["src"] for job in jobs]
    outs = pl.pallas_call(
        functools.partial(_norm_matmul_kernel, has_low=w_low is not None, jobs=jobs),
        grid=grid,
        in_specs=in_specs,
        out_specs=out_specs,
        out_shape=out_shape,
        scratch_shapes=[pltpu.VMEM((tm, d), BF16)],
        compiler_params=_cparams(2),
        name="norm_matmul",
    )(*args)
    n_main = 1 + (w_low is not None)
    return outs[0], (outs[1] if w_low is not None else None), list(outs[n_main:])


def _matmul_res_kernel(*refs, n_in):
    a_refs = refs[:n_in]
    w_refs = refs[n_in:2 * n_in]
    res_ref = refs[2 * n_in]
    o_ref = refs[2 * n_in + 1]
    acc = res_ref[...]
    for a_ref, w_ref in zip(a_refs, w_refs):
        acc = acc + jnp.dot(a_ref[...], w_ref[...], preferred_element_type=F32)
    o_ref[...] = acc


def _matmul_res(a_list, w, res, *, tm, tn):
    m, n = res.shape
    ka = a_list[0].shape[1]
    assert all(a.shape[1] == ka for a in a_list) and ka * len(a_list) == w.shape[0]
    grid = (m // tm, n // tn)
    in_specs = [pl.BlockSpec((tm, ka), lambda i, j: (i, 0)) for _ in a_list]
    for rb in range(len(a_list)):
        in_specs.append(pl.BlockSpec((ka, tn), lambda i, j, rb=rb: (rb, j)))
    in_specs.append(pl.BlockSpec((tm, tn), lambda i, j: (i, j)))
    return pl.pallas_call(
        functools.partial(_matmul_res_kernel, n_in=len(a_list)),
        grid=grid,
        in_specs=in_specs,
        out_specs=pl.BlockSpec((tm, tn), lambda i, j: (i, j)),
        out_shape=jax.ShapeDtypeStruct((m, n), F32),
        compiler_params=_cparams(2),
        name="matmul_res",
    )(*a_list, *([w] * len(a_list)), res)


def _local_cumsum(lg):
    c, k = lg.shape
    x = lg.reshape(c // SUBLANES, SUBLANES, k)
    pos = lax.broadcasted_iota(jnp.int32, x.shape, 1)
    s = 1
    while s < SUBLANES:
        x = x + jnp.where(pos >= s, pltpu.roll(x, s, 1), 0.0)
        s *= 2
    per = SUB // SUBLANES
    x = x.reshape(c // SUB, per, SUBLANES, k)
    groups = [x[:, 0]]
    for g in range(1, per):
        groups.append(x[:, g] + groups[-1][:, SUBLANES - 1:SUBLANES, :])
    return jnp.stack(groups, axis=1).reshape(c, k)


def _chunk_operands(q, k, bl, tot, sub):
    ns = q.shape[0] // sub
    a = [jnp.zeros_like(tot[0])]
    for m in range(ns):
        a.append(a[-1] + tot[m])
    b_last = a[ns]

    qt, kbar, khat, qin, kout = [], [], [], [], []
    qd = {d: [] for d in range(2, ns)}
    for m in range(ns):
        sl = slice(m * sub, (m + 1) * sub)
        blm = bl[sl]
        e_in = jnp.exp(blm)
        qtm = q[sl] * e_in
        kbm = k[sl] * (1.0 / e_in)
        khm = kbm * jnp.exp(tot[m])
        qt.append(qtm)
        kbar.append(kbm)
        khat.append(khm)
        qin.append(qtm * jnp.exp(a[m]))
        kout.append(khm * jnp.exp(b_last - a[m + 1]))
        for d in range(2, ns):
            if m >= d:
                qd[d].append(qtm * jnp.exp(a[m] - a[m - d + 1]))
            else:
                qd[d].append(jnp.zeros_like(qtm))

    def cat(xs):
        return jnp.concatenate(xs, axis=0).astype(BF16)

    return dict(qt=cat(qt), kbar=cat(kbar), khat=cat(khat), qin=cat(qin), kout=cat(kout),
                qd=[cat(qd[d]) for d in range(2, ns)], decay=jnp.exp(b_last))


def _chunk_steps(chunks, st_ref, sub=SUB):
    c = chunks[0][0][0].shape[0]
    ns = c // sub
    shift = sub.bit_length() - 1
    assert 1 << shift == sub and c % sub == 0
    ops = [[_chunk_operands(q, k, bl, tot, sub) for (q, k, _, bl, tot) in heads] for heads in chunks]
    raw = []
    for chunk_ops in ops:
        for op in chunk_ops:
            s = [lax.dot_general(op["qt"], op["kbar"], NT_DIMS, preferred_element_type=F32)]
            if ns > 1:
                s.append(lax.dot_general(op["qt"], op["khat"], NT_DIMS, preferred_element_type=F32))
            s += [lax.dot_general(qd, op["khat"], NT_DIMS, preferred_element_type=F32)
                  for qd in op["qd"]]
            raw.append(s)
    ri = lax.broadcasted_iota(jnp.int32, (c, c), 0)
    ci = lax.broadcasted_iota(jnp.int32, (c, c), 1)
    dist = (ri >> shift) - (ci >> shift)
    intra = []
    for s, (_, _, vb, _, _) in zip(raw, [head for heads in chunks for head in heads]):
        scores = jnp.where((dist == 0) & (ri >= ci), s[0], 0.0)
        for d in range(1, ns):
            scores = jnp.where(dist == d, s[d], scores)
        intra.append(jnp.dot(scores.astype(BF16), vb, preferred_element_type=F32))
    outs = []
    for g, (heads, chunk_ops) in enumerate(zip(chunks, ops)):
        outs.append([
            intra[g * len(heads) + h] + lax.dot_general(
                op["qin"], st_ref[h].astype(BF16), NT_DIMS, preferred_element_type=F32)
            for h, op in enumerate(chunk_ops)])
        for h, op in enumerate(chunk_ops):
            st_ref[h] = st_ref[h] * op["decay"] + lax.dot_general(
                heads[h][2], op["kout"], TN_DIMS, preferred_element_type=F32)
    return outs


def _sub_totals(bl):
    return [bl[(m + 1) * SUB - 1:(m + 1) * SUB, :] for m in range(bl.shape[0] // SUB)]


def _state_prologue(s0_ref, st_ref):
    @pl.when(pl.program_id(1) == 0)
    def _():
        for h in range(st_ref.shape[0]):
            if s0_ref is None:
                st_ref[h] = jnp.zeros(st_ref.shape[1:], F32)
            else:
                st_ref[h] = jnp.transpose(s0_ref[0, h])


def _state_epilogue(sn_ref, st_ref):
    @pl.when(pl.program_id(1) == pl.num_programs(1) - 1)
    def _():
        for h in range(st_ref.shape[0]):
            sn_ref[0, h] = jnp.transpose(st_ref[h])


def _chunk_loop(n_rows, chunk, body, together, unroll):
    assert n_rows % chunk == 0
    n_chunks = n_rows // chunk
    together = min(together, n_chunks)
    assert n_chunks % together == 0
    trips = n_chunks // together

    def step(c, carry):
        body([pl.ds(pl.multiple_of((c * together + p) * chunk, chunk), chunk) for p in range(together)])
        return carry

    lax.fori_loop(0, trips, step, 0, unroll=min(unroll, trips))


def _head(ref, rows, h, width):
    return ref[rows, h * width:(h + 1) * width]


def _ret_chunks(chunks, st_ref):
    chunk = chunks[0][2].shape[0]
    pos1 = (lax.broadcasted_iota(jnp.int32, (chunk, 1), 0) + 1).astype(F32)
    half = DK_A // 2
    inputs = []
    for get, _, cos, sin in chunks:
        heads = []
        for h in range(H_A):
            lgam = float(np.log1p(-np.exp2(-5.0 - h)))
            bl = pos1 * lgam
            tot = [jnp.full((1, 1), lgam * chunk, F32)]
            qr, kr = get("q", h), get("k", h)
            q = qr * cos + pltpu.roll(qr, half, 1) * sin
            k = (kr * cos + pltpu.roll(kr, half, 1) * sin) * (DK_A ** -0.5)
            heads.append((q, k, get("v", h).astype(BF16), bl, tot))
        inputs.append(heads)
    for (get, put, _, _), outs in zip(chunks, _chunk_steps(inputs, st_ref, sub=chunk)):
        for h, o in enumerate(outs):
            mu = jnp.mean(o, axis=-1, keepdims=True)
            oc = o - mu
            var = jnp.mean(oc * oc, axis=-1, keepdims=True)
            put(h, ((oc * lax.rsqrt(var + EPS)) * _silu(get("g", h))).astype(BF16))


def _hgrn_lower_bound(lbp_ref, layer):
    lbp = lbp_ref[...]
    e = jnp.exp(lbp - jnp.max(lbp, axis=0, keepdims=True))
    sm = e / jnp.sum(e, axis=0, keepdims=True)
    return jnp.sum(sm[:layer + 1], axis=0, keepdims=True)


def _gated_rmsnorm_out(chunks, outs_per_chunk, gn):
    for (get, put, *_), outs in zip(chunks, outs_per_chunk):
        for h, o in enumerate(outs):
            ms = jnp.mean(o * o, axis=-1, keepdims=True)
            put(h, (((o * lax.rsqrt(ms + EPS)) * gn) * _silu(get("g", h))).astype(BF16))


def _hgrn_chunks(chunks, lb_all, gn, st_ref):
    for h0 in range(0, H_B, HGRN_HEAD_GROUP):
        group = [(lambda name, h, get=get: get(name, h0 + h), lambda h, x, put=put: put(h0 + h, x))
                 for get, put in chunks]
        inputs = []
        for get, _ in group:
            heads = []
            for h in range(HGRN_HEAD_GROUP):
                lb = lb_all[:, (h0 + h) * DK_B:(h0 + h + 1) * DK_B]
                q = _silu(get("q", h)) * (DK_B ** -0.5)
                f = lb + (1.0 - lb) * _sigmoid(get("f", h))
                bl = _local_cumsum(jnp.log(f))
                heads.append((q, 1.0 - f, get("i", h).astype(BF16), bl, _sub_totals(bl)))
            inputs.append(heads)
        _gated_rmsnorm_out(group, _chunk_steps(inputs, st_ref.at[h0:h0 + HGRN_HEAD_GROUP]), gn)


def _gla_chunks(chunks, wgk, bgk, gn, st_ref):
    inputs = []
    for get, _, low in chunks:
        x = jnp.dot(low.astype(BF16), wgk, preferred_element_type=F32) + bgk
        lg_all = (jnp.minimum(x, 0.0) - jnp.log1p(jnp.exp(-jnp.abs(x)))) * (1.0 / GATE_NORMALIZER)
        heads = []
        for h in range(H_C):
            bl = _local_cumsum(lg_all[:, h * DK_C:(h + 1) * DK_C])
            q = get("q", h) * (DK_C ** -0.5)
            heads.append((q, get("k", h), get("v", h).astype(BF16), bl, _sub_totals(bl)))
        inputs.append(heads)
    _gated_rmsnorm_out(chunks, _chunk_steps(inputs, st_ref), gn)


def _ref_getter(refs, widths, rows):
    return lambda name, h: _head(refs[name], rows, h, widths[name])


def _ref_putter(o_ref, width, rows):
    def put(h, x):
        o_ref[rows, h * width:(h + 1) * width] = x
    return put


def _ret_kernel(*refs, has_s0):
    if has_s0:
        q_ref, k_ref, v_ref, g_ref, cos_ref, sin_ref, s0_ref, o_ref, sn_ref, st_ref = refs
    else:
        q_ref, k_ref, v_ref, g_ref, cos_ref, sin_ref, o_ref, sn_ref, st_ref = refs
        s0_ref = None
    _state_prologue(s0_ref, st_ref)
    srcs = {"q": q_ref, "k": k_ref, "v": v_ref, "g": g_ref}
    widths = {"q": DK_A, "k": DK_A, "v": DV_A, "g": DV_A}

    def body(rows_list):
        _ret_chunks([(_ref_getter(srcs, widths, rows), _ref_putter(o_ref, DV_A, rows),
                      cos_ref[rows, :], sin_ref[rows, :]) for rows in rows_list], st_ref)

    n_rows = q_ref.shape[0]
    _chunk_loop(n_rows, min(RET_CHUNK, n_rows), body, together=2, unroll=1)
    _state_epilogue(sn_ref, st_ref)


def _hgrn_kernel(*refs, has_s0, layer, jobs):
    q_ref, f_ref, i_ref, g_ref, lbp_ref, gn_ref = refs[:6]
    pos = 6
    s0_ref = refs[pos] if has_s0 else None
    pos += has_s0
    src_refs = refs[pos:pos + len(jobs)]
    o_ref, sn_ref = refs[pos + len(jobs):pos + len(jobs) + 2]
    dst_refs = refs[pos + len(jobs) + 2:pos + 2 * len(jobs) + 2]
    st_ref = refs[-1]
    _run_cast_jobs(jobs, src_refs, dst_refs)
    _state_prologue(s0_ref, st_ref)
    lb_all = _hgrn_lower_bound(lbp_ref, layer)
    gn = gn_ref[...]
    srcs = {"q": q_ref, "f": f_ref, "i": i_ref, "g": g_ref}
    widths = {"q": DK_B, "f": DK_B, "i": DV_B, "g": DV_B}

    def body(rows_list):
        _hgrn_chunks([(_ref_getter(srcs, widths, rows), _ref_putter(o_ref, DV_B, rows))
                      for rows in rows_list], lb_all, gn, st_ref)

    _chunk_loop(q_ref.shape[0], CHUNK, body, together=1, unroll=2)
    _state_epilogue(sn_ref, st_ref)


def _gla_kernel(*refs, has_s0):
    if has_s0:
        q_ref, k_ref, v_ref, g_ref, low_ref, wgk_ref, bgk_ref, gn_ref, s0_ref, o_ref, sn_ref, st_ref = refs
    else:
        q_ref, k_ref, v_ref, g_ref, low_ref, wgk_ref, bgk_ref, gn_ref, o_ref, sn_ref, st_ref = refs
        s0_ref = None
    _state_prologue(s0_ref, st_ref)
    gn = gn_ref[...]
    bgk = bgk_ref[...]
    srcs = {"q": q_ref, "k": k_ref, "v": v_ref, "g": g_ref}
    widths = {"q": DK_C, "k": DK_C, "v": DV_C, "g": DV_C}

    def body(rows_list):
        _gla_chunks([(_ref_getter(srcs, widths, rows), _ref_putter(o_ref, DV_C, rows), low_ref[rows, :])
                     for rows in rows_list], wgk_ref[...], bgk, gn, st_ref)

    _chunk_loop(q_ref.shape[0], CHUNK, body, together=4, unroll=1)
    _state_epilogue(sn_ref, st_ref)


def _row_spec(tt, width, nt, block):
    return pl.BlockSpec((tt, width), lambda b, t: (b * nt + t, block))


def _recurrence_call(body, in_specs, args, s0, bsz, seq, heads, dk, dv, tt, jobs=()):
    nt = seq // tt
    if s0 is not None:
        in_specs = in_specs + [pl.BlockSpec((1, heads, dk, dv), lambda b, t: (b, 0, 0, 0))]
        args = args + [s0]
    job_io = [_cast_job_io(job, (bsz, nt)) for job in jobs]
    outs = pl.pallas_call(
        body,
        grid=(bsz, nt),
        in_specs=in_specs + [io[0] for io in job_io],
        out_specs=[
            pl.BlockSpec((tt, heads * dv), lambda b, t: (b * nt + t, 0)),
            pl.BlockSpec((1, heads, dk, dv), lambda b, t: (b, 0, 0, 0)),
        ] + [io[1] for io in job_io],
        out_shape=[
            jax.ShapeDtypeStruct((bsz * seq, heads * dv), BF16),
            jax.ShapeDtypeStruct((bsz, heads, dk, dv), F32),
        ] + [io[2] for io in job_io],
        scratch_shapes=[pltpu.VMEM((heads, dv, dk), F32)],
        compiler_params=_cparams(2),
        name="recurrence",
    )(*args, *[job["src"] for job in jobs])
    return outs[0], outs[1], list(outs[2:])


def _retention(p, cos2, sin2, s0, bsz, seq, tt):
    nt = seq // tt
    qa, va = H_A * DK_A, H_A * DV_A
    in_specs = [
        _row_spec(tt, qa, nt, 0),
        _row_spec(tt, qa, nt, 1),
        _row_spec(tt, va, nt, 2 * qa // va),
        _row_spec(tt, va, nt, (2 * qa + va) // va),
        pl.BlockSpec((tt, DK_A), lambda b, t: (t, 0)),
        pl.BlockSpec((tt, DK_A), lambda b, t: (t, 0)),
    ]
    args = [p, p, p, p, cos2, sin2]
    body = functools.partial(_ret_kernel, has_s0=s0 is not None)
    return _recurrence_call(body, in_specs, args, s0, bsz, seq, H_A, DK_A, DV_A, tt)[:2]


def _hgrn(p, lbp, gnorm, s0, bsz, seq, tt, layer, jobs=()):
    nt = seq // tt
    base = 2 * H_A * DK_A + 2 * H_A * DV_A
    qb = H_B * DK_B
    assert base % qb == 0 and H_B * DV_B == qb
    in_specs = [_row_spec(tt, qb, nt, base // qb + n) for n in range(4)]
    in_specs += [
        pl.BlockSpec(lbp.shape, lambda b, t: (0, 0)),
        pl.BlockSpec((1, DV_B), lambda b, t: (0, 0)),
    ]
    args = [p, p, p, p, lbp, gnorm.reshape(1, DV_B)]
    jobs = list(jobs)
    body = functools.partial(_hgrn_kernel, has_s0=s0 is not None, layer=layer, jobs=jobs)
    return _recurrence_call(body, in_specs, args, s0, bsz, seq, H_B, DK_B, DV_B, tt, jobs)


def _gla(p, low, wgk, bgk, gnorm, s0, bsz, seq, tt):
    nt = seq // tt
    qc, vc = H_C * DK_C, H_C * DV_C
    in_specs = [
        _row_spec(tt, qc, nt, 0),
        _row_spec(tt, qc, nt, 1),
        _row_spec(tt, vc, nt, 2 * qc // vc),
        _row_spec(tt, vc, nt, (2 * qc + vc) // vc),
        _row_spec(tt, low.shape[1], nt, 0),
        pl.BlockSpec(wgk.shape, lambda b, t: (0, 0)),
        pl.BlockSpec((1, qc), lambda b, t: (0, 0)),
        pl.BlockSpec((1, DV_C), lambda b, t: (0, 0)),
    ]
    args = [p, p, p, p, low, wgk, bgk.reshape(1, qc), gnorm.reshape(1, DV_C)]
    body = functools.partial(_gla_kernel, has_s0=s0 is not None)
    return _recurrence_call(body, in_specs, args, s0, bsz, seq, H_C, DK_C, DV_C, tt)[:2]


def _ffn_kernel(*refs, seq_rows, tiles_per_seq, has_cache, final_norm, jobs):
    refs = list(refs)
    h_ref, nw_ref, up_ref, conv_ref, wd_ref = refs[:5]
    pos = 5
    cache_ref = None
    nf_ref = None
    if has_cache:
        cache_ref = refs[pos]
        pos += 1
    if final_norm:
        nf_ref = refs[pos]
        pos += 1
    nj = len(jobs)
    src_refs = refs[pos:pos + nj]
    out_ref, newc_ref = refs[pos + nj:pos + nj + 2]
    dst_refs = refs[pos + nj + 2:pos + 2 * nj + 2]
    hn_ref, carry_ref = refs[pos + 2 * nj + 2:]

    i = pl.program_id(0)
    j = pl.program_id(1)
    tm, tf = h_ref.shape[0], wd_ref.shape[0]

    @pl.when(j == 0)
    def _():
        _rmsnorm_rows_to(h_ref, nw_ref, hn_ref, NORM_ROWS, copy_ref=out_ref)

    if tiles_per_seq > 1:
        @pl.when(i % tiles_per_seq == 0)
        def _():
            carry_ref[j] = jnp.zeros((CONV_W - 1, tf), F32)

    hn = hn_ref[...]
    a = jnp.dot(hn, up_ref[0], preferred_element_type=F32)
    u = jnp.dot(hn, up_ref[1], preferred_element_type=F32)

    nseq = tm // seq_rows
    r8 = lax.broadcasted_iota(jnp.int32, (SUBLANES, 1), 0)
    sh1 = pltpu.roll(a, 1, 0)
    sh2 = pltpu.roll(a, 2, 0)
    a1_parts, a2_parts = [], []
    for s in range(nseq):
        if tiles_per_seq > 1:
            p0, p1 = carry_ref[j, 0:1, :], carry_ref[j, 1:2, :]
        elif has_cache:
            p0, p1 = cache_ref[s, 0:1, :], cache_ref[s, 1:2, :]
        else:
            p0 = p1 = jnp.zeros((1, tf), F32)
        lo, hi = s * seq_rows, (s + 1) * seq_rows
        head = slice(lo, lo + SUBLANES)
        a1_parts += [jnp.where(r8 == 0, p1, sh1[head]), sh1[lo + SUBLANES:hi]]
        a2_parts += [jnp.where(r8 == 0, p0, jnp.where(r8 == 1, p1, sh2[head])),
                     sh2[lo + SUBLANES:hi]]
    a1 = jnp.concatenate(a1_parts, axis=0)
    a2 = jnp.concatenate(a2_parts, axis=0)
    cw = conv_ref[...]
    ac = cw[CONV_W:CONV_W + 1, :] + a2 * cw[0:1, :] + a1 * cw[1:2, :] + a * cw[2:3, :]

    if tiles_per_seq > 1:
        carry_ref[j] = a[tm - (CONV_W - 1):, :]
    for s in range(nseq):
        newc_ref[s] = a[(s + 1) * seq_rows - (CONV_W - 1):(s + 1) * seq_rows, :]

    y = (_silu(ac) * u).astype(BF16)
    out_ref[...] += jnp.dot(y, wd_ref[...], preferred_element_type=F32)
    _run_cast_jobs(jobs, src_refs, dst_refs)

    if final_norm:
        @pl.when(j == pl.num_programs(1) - 1)
        def _():
            rows = min(NORM_ROWS, tm)

            def body(c, carry):
                r = pl.multiple_of(c * rows, rows)
                x = out_ref[pl.ds(r, rows), :]
                ms = jnp.mean(x * x, axis=-1, keepdims=True)
                out_ref[pl.ds(r, rows), :] = (x * lax.rsqrt(ms + EPS)) * nf_ref[...]
                return carry

            lax.fori_loop(0, tm // rows, body, 0)


def _ffn(h, nw, up, conv, wd, cache, nf, jobs=(), *, bsz, seq, tm, tf):
    m, d = h.shape
    dffp = up.shape[2]
    nj = dffp // tf
    jobs = list(jobs)
    if seq >= tm:
        tiles_per_seq = seq // tm
        seq_rows = tm
        seqs_per_tile = 1
    else:
        tiles_per_seq = 1
        seq_rows = seq
        seqs_per_tile = tm // seq
    assert seq_rows & (seq_rows - 1) == 0
    grid = (m // tm, nj)
    in_specs = [
        pl.BlockSpec((tm, d), lambda i, j: (i, 0)),
        pl.BlockSpec((1, d), lambda i, j: (0, 0)),
        pl.BlockSpec((2, d, tf), lambda i, j: (0, 0, j)),
        pl.BlockSpec((CONV_W + 1, tf), lambda i, j: (0, j)),
        pl.BlockSpec((tf, d), lambda i, j: (j, 0)),
    ]
    args = [h, nw.reshape(1, d), up, conv, wd]
    if cache is not None:
        assert tiles_per_seq == 1
        in_specs.append(pl.BlockSpec((seqs_per_tile, CONV_W - 1, tf), lambda i, j: (i, 0, j)))
        args.append(cache)
    if nf is not None:
        in_specs.append(pl.BlockSpec((1, d), lambda i, j: (0, 0)))
        args.append(nf.reshape(1, d))
    newc_spec = pl.BlockSpec((seqs_per_tile, CONV_W - 1, tf), lambda i, j: (i, 0, j))
    job_io = [_cast_job_io(job, grid) for job in jobs]
    in_specs += [io[0] for io in job_io]
    args += [job["src"] for job in jobs]
    body = functools.partial(
        _ffn_kernel, seq_rows=seq_rows, tiles_per_seq=tiles_per_seq,
        has_cache=cache is not None, final_norm=nf is not None, jobs=jobs)
    outs = pl.pallas_call(
        body,
        grid=grid,
        in_specs=in_specs,
        out_specs=[pl.BlockSpec((tm, d), lambda i, j: (i, 0)), newc_spec] + [io[1] for io in job_io],
        out_shape=[
            jax.ShapeDtypeStruct((m, d), F32),
            jax.ShapeDtypeStruct((grid[0] * seqs_per_tile, CONV_W - 1, dffp), F32),
        ] + [io[2] for io in job_io],
        scratch_shapes=[
            pltpu.VMEM((tm, d), BF16),
            pltpu.VMEM((nj, CONV_W - 1, tf), F32),
        ],
        compiler_params=_cparams(2),
        name="ffn",
    )(*args)
    out, newc = outs[0], outs[1]
    if tiles_per_seq > 1:
        newc = newc.reshape(bsz, tiles_per_seq, CONV_W - 1, dffp)[:, -1]
    return out, newc, list(outs[2:])


def _pad_to(x, axis, size):
    pad = size - x.shape[axis]
    if pad == 0:
        return x
    widths = [(0, 0)] * x.ndim
    widths[axis] = (0, pad)
    return jnp.pad(x, widths)


def _rotary_tables(pos):
    half = DK_A // 2
    inv = ROPE_BASE ** (-jnp.arange(half, dtype=F32) / half)
    ang = pos.astype(F32)[:, None] * inv[None, :]
    cos, sin = jnp.cos(ang), jnp.sin(ang)
    return jnp.concatenate([cos, cos], axis=-1), jnp.concatenate([-sin, sin], axis=-1)


FFN_TILE = 512


def _in_proj_job(raw, l, n_steps):
    if l % 2 == 0:
        src = raw["w_in_even"]
        rb = _row_block(src.shape[1], src.shape[1], n_steps)
        return _cast_job(src, (l // 2,), src.shape[1], src.shape[2], rb)
    src = jnp.swapaxes(raw["w_in_odd"], 1, 2)
    return _transpose_cast_job(src, (l // 2,), src.shape[1], src.shape[2], LANES)


def _initial_weights(raw):
    w = {}
    depth = raw["norm_mix"].shape[0]
    for l in range(1, depth, 2):
        w[("gk2", l)] = _pad_to(raw["w_gk2"][l // 2], 0, LANES).astype(BF16)
    return w


def _cast_now(job):
    src = job["src"][job["lead"]]
    if job["transpose"]:
        return _pad_to(src.T.astype(BF16), 1, job["n_dst"] * job["rb"])
    rows_out = job["n_dst"] * job["rb"]
    parts = [_pad_to(_pad_to(src[:, s * job["cin"]:(s + 1) * job["cin"]].astype(BF16), 1, job["cout"]),
                     0, rows_out) for s in range(job["halves"])]
    return jnp.stack(parts) if job["halves"] > 1 else parts[0]


def _missing_jobs(w, wanted, n_steps):
    keys, jobs = [], []
    for key, job in wanted:
        if key in w:
            continue
        if job["n_dst"] <= n_steps:
            keys.append(key)
            jobs.append(job)
        else:
            w[key] = _cast_now(job)
    return keys, jobs


def _row_block(rows, rows_out, n_steps):
    candidates = [BF16_TILE_ROWS << s for s in range(4)]
    fits = [rb for rb in candidates if rows % rb == 0 and rows_out % rb == 0]
    return next((rb for rb in fits if rows_out // rb <= n_steps), fits[-1])


def _layer_weight_jobs(raw, l, dffp, n_steps):
    w_out = raw["w_out_even"] if l % 2 == 0 else raw["w_out_odd"]
    up, down = raw["ffn_w_up"], raw["ffn_w_down"]
    dff = down.shape[1]
    d = up.shape[1]
    rb, rb_down = _row_block(d, d, n_steps), _row_block(dff, dffp, n_steps)
    return [
        (("out", l), _cast_job(w_out, (l // 2,), w_out.shape[1], w_out.shape[2], rb)),
        (("up", l), _cast_job(up, (l,), d, dff, rb, halves=2, cout=dffp)),
        (("down", l), _cast_job(down, (l,), dff, down.shape[2], rb_down, rows_out=dffp)),
    ]


def _tiles(m, seq):
    return min(1024, m), 1024, min(512, m), 2048, min(512, m), min(512, seq)


def _first_projection(x, raw, w):
    bsz, seq, d = x.shape
    _, tn, *_ = _tiles(bsz * seq, seq)
    p, w[("in", 0)] = _norm_matmul_f32w(x.reshape(bsz * seq, d), raw["norm_mix"][0],
                                        raw["w_in_even"][0], tn=tn)
    return p


def _run_group(x, pos0, s_ret, s_hgrn, s_gla, s_conv, raw, w, p0=None):
    if ("in", 0) not in w:
        w[("in", 0)] = raw["w_in_even"][0].astype(BF16)
    bsz, seq, d = x.shape
    m = bsz * seq
    depth = raw["norm_mix"].shape[0]
    dff = raw["ffn_w_down"].shape[1]
    dffp = -(-dff // FFN_TILE) * FFN_TILE
    tm, tn, tm_out, tn_out, tm_ffn, tt = _tiles(m, seq)
    cos2, sin2 = _rotary_tables(pos0 + jnp.arange(seq, dtype=jnp.int32))
    conv = _pad_to(jnp.concatenate([raw["ffn_conv_w"], raw["ffn_conv_b"][:, None, :]], axis=1), 2, dffp)

    h = x.reshape(m, d)
    new_ret, new_hgrn, new_gla, new_conv = [], [], [], []
    for l in range(depth):
        i = l // 2
        w_in = w[("in", l)]
        n_low = 0 if l % 2 == 0 else LANES
        n = w_in.shape[1] - n_low
        n_steps = (m // tm) * (n // tn)
        rec_steps = bsz * (seq // tt)
        wanted = dict(_layer_weight_jobs(raw, l, dffp, n_steps))
        wanted_rec = dict(_layer_weight_jobs(raw, l, dffp, rec_steps))
        in_rec = [("up", l)] if l % 2 == 0 and wanted_rec[("up", l)]["n_dst"] <= rec_steps else []
        keys, jobs = _missing_jobs(w, [kv for kv in wanted.items() if kv[0] not in in_rec], n_steps)
        if l == 0 and p0 is not None:
            assert not jobs
            p, low = p0, None
        else:
            p, low, casts = _norm_matmul(h, raw["norm_mix"][l], w_in, n, n_low, jobs, tm=tm, tn=tn)
            w.update(zip(keys, casts))
        if l % 2 == 0:
            o_a, sr = _retention(p, cos2, sin2, None if s_ret is None else s_ret[i], bsz, seq, tt)
            keys, jobs = _missing_jobs(w, [(key, wanted_rec[key]) for key in in_rec], rec_steps)
            o_b, sh, casts = _hgrn(p, raw["hgrn_lb"], raw["hgrn_gnorm"][i],
                                   None if s_hgrn is None else s_hgrn[i], bsz, seq, tt, l, jobs)
            w.update(zip(keys, casts))
            o_list = [o_a, o_b]
            new_ret.append(sr)
            new_hgrn.append(sh)
        else:
            o_c, sg = _gla(p, low, w[("gk2", l)], raw["b_gk2"][i], raw["gla_gnorm"][i],
                           None if s_gla is None else s_gla[i], bsz, seq, tt)
            o_list = [o_c]
            new_gla.append(sg)
        h = _matmul_res(o_list, w[("out", l)], h, tm=tm_out, tn=tn_out)

        n_steps = (m // tm_ffn) * (dffp // FFN_TILE)
        wanted = []
        if l + 1 < depth:
            wanted = ([(("in", l + 1), _in_proj_job(raw, l + 1, n_steps))]
                      + _layer_weight_jobs(raw, l + 1, dffp, n_steps))
        keys, jobs = _missing_jobs(w, wanted, n_steps)
        cache = None if s_conv is None else _pad_to(s_conv[l], 2, dffp)
        h, nc, casts = _ffn(h, raw["norm_ffn"][l], w[("up", l)], conv[l], w[("down", l)], cache,
                            raw["norm_final"] if l == depth - 1 else None, jobs,
                            bsz=bsz, seq=seq, tm=tm_ffn, tf=FFN_TILE)
        w.update(zip(keys, casts))
        new_conv.append(nc[:, :, :dff])
    return (h.reshape(bsz, seq, d), jnp.stack(new_ret), jnp.stack(new_hgrn),
            jnp.stack(new_gla), jnp.stack(new_conv))


def kernel(x_prompt, x_sample, state_ret, state_hgrn, state_gla, cache_ffn_conv, norm_mix, norm_ffn, norm_final, w_in_even, w_out_even, hgrn_lb, hgrn_gnorm, w_in_odd, w_gk2, b_gk2, gla_gnorm, w_out_odd, ffn_w_up, ffn_conv_w, ffn_conv_b, ffn_w_down):
    raw = dict(norm_mix=norm_mix, norm_ffn=norm_ffn, norm_final=norm_final, w_in_even=w_in_even,
               w_out_even=w_out_even, hgrn_lb=hgrn_lb, hgrn_gnorm=hgrn_gnorm, w_in_odd=w_in_odd,
               w_gk2=w_gk2, b_gk2=b_gk2, gla_gnorm=gla_gnorm, w_out_odd=w_out_odd, ffn_w_up=ffn_w_up,
               ffn_conv_w=ffn_conv_w, ffn_conv_b=ffn_conv_b, ffn_w_down=ffn_w_down)
    w = _initial_weights(raw)
    p0_sample = _first_projection(x_sample, raw, w)
    y_p, ret_p, hgrn_p, gla_p, conv_p = _run_group(x_prompt, 0, None, None, None, None, raw, w)
    y_s, ret_s, hgrn_s, gla_s, conv_s = _run_group(
        x_sample, PAST_LEN, state_ret, state_hgrn, state_gla, cache_ffn_conv, raw, w, p0=p0_sample)
    return (y_p, y_s, ret_p, ret_s, hgrn_p, hgrn_s, gla_p, gla_s, conv_p, conv_s)
```
